```python
import jax, jax.numpy as jnp
from jax import lax
import numpy as np

D_MODEL = 1024
BATCH = 32
SEQ = 2048
DEPTH = 1
DEC_BATCH = 128
DEC_SEQ = 4
PAST_LEN = 8192
PAGE_SIZE = 128

D_CONV = 512
CONV_WIDTH = 3
N_HEADS = 8
N_KV_HEADS = 2
HEAD_DIM = 64
IDX_HEADS = 8
IDX_DIM = 64
IDX_SCALE = (IDX_DIM ** -0.5) * (IDX_HEADS ** -0.5)
TOPK_MAX = 256
Q_BLOCK = 128
ROPE_THETA = 10000.0
PEER_HEADS = 8
PEER_KEYS = 128
PEER_N_EXPERTS = PEER_KEYS * PEER_KEYS
PEER_QDIM = 256
PEER_TOPK = 16
PEER_BLOCK = 128
N_MOD = 6
NORM_EPS = 1e-6
NEG_INF = -1e30
PROJ_WIDTH = 3 * D_CONV + (N_HEADS + 2 * N_KV_HEADS) * HEAD_DIM + IDX_HEADS * IDX_DIM + IDX_DIM + IDX_HEADS + 2 * D_MODEL

kernel_name = "hybrid_shortconv_dsa_peer_adaln_step"


def _rmsnorm(x, w):
    xf = x.astype(jnp.float32)
    y = xf * lax.rsqrt(jnp.mean(xf * xf, axis=-1, keepdims=True) + NORM_EPS)
    return (y * w.astype(jnp.float32)).astype(x.dtype)


def _modulate(h, shift, scale):
    return h * (1.0 + scale[:, None, :]) + shift[:, None, :]


def _adaln(c, w_ada, b_ada):
    mod = c @ w_ada + b_ada
    return jnp.split(mod, N_MOD, axis=-1)


def _rope(x, pos):
    d = x.shape[-1]
    half = d // 2
    freqs = ROPE_THETA ** (-jnp.arange(half, dtype=jnp.float32) / half)
    ang = pos.astype(jnp.float32)[:, None] * freqs[None, :]
    cos = jnp.cos(ang)[:, None, :]
    sin = jnp.sin(ang)[:, None, :]
    xf = x.astype(jnp.float32)
    x1, x2 = xf[..., :half], xf[..., half:]
    return jnp.concatenate([x1 * cos - x2 * sin, x2 * cos + x1 * sin], axis=-1).astype(x.dtype)


def _split_proj(z):
    sizes = (D_CONV, D_CONV, D_CONV, N_HEADS * HEAD_DIM, N_KV_HEADS * HEAD_DIM, N_KV_HEADS * HEAD_DIM,
             IDX_HEADS * IDX_DIM, IDX_DIM, IDX_HEADS, D_MODEL, D_MODEL)
    parts, start = [], 0
    for s in sizes:
        parts.append(z[..., start:start + s])
        start += s
    return parts


def _mixer_inputs(x, shift, scale, norm_w, w_in, q_norm_w, k_norm_w, pos):
    h = _modulate(_rmsnorm(x, norm_w), shift, scale)
    z = jnp.einsum('btd,dp->btp', h, w_in)
    xin, b_gate, c_gate, q, k, v, qi, ki, wi, g_a, g_b = _split_proj(z)
    bsz, t = x.shape[0], x.shape[1]
    u = c_gate * xin
    q = _rope(_rmsnorm(q.reshape(bsz, t, N_HEADS, HEAD_DIM), q_norm_w), pos)
    k = _rope(_rmsnorm(k.reshape(bsz, t, N_KV_HEADS, HEAD_DIM), k_norm_w), pos)
    v = v.reshape(bsz, t, N_KV_HEADS, HEAD_DIM)
    qi = _rope(qi.reshape(bsz, t, IDX_HEADS, IDX_DIM), pos)
    ki = _rope(ki[:, :, None, :], pos)[:, :, 0, :]
    wi = wi * IDX_SCALE
    return u, b_gate, q, k, v, qi, ki, wi, g_a, g_b


def _causal_conv(u, past, conv_w):
    full = jnp.concatenate([past.astype(u.dtype), u], axis=1)
    t = u.shape[1]
    y = conv_w[CONV_WIDTH - 1] * full[:, CONV_WIDTH - 1:CONV_WIDTH - 1 + t]
    for j in range(CONV_WIDTH - 1):
        y = y + conv_w[j] * full[:, j:j + t]
    return y, full[:, full.shape[1] - (CONV_WIDTH - 1):]


def _index_select(qi, wi, pos_q, ki, pos_k, topk):
    s = jnp.einsum('qhd,ld->qhl', qi.astype(jnp.float32), ki.astype(jnp.float32))
    score = jnp.einsum('qh,qhl->ql', wi.astype(jnp.float32), jax.nn.relu(s))
    score = jnp.where(pos_k[None, :] <= pos_q[:, None], score, NEG_INF)
    _, sel = lax.top_k(score, topk)
    return sel


def _sparse_attend(q, ks, vs, valid):
    nq = q.shape[0]
    qg = q.reshape(nq, N_KV_HEADS, N_HEADS // N_KV_HEADS, HEAD_DIM).astype(jnp.float32)
    logits = jnp.einsum('qngd,qsnd->qngs', qg, ks.astype(jnp.float32)) * (HEAD_DIM ** -0.5)
    logits = jnp.where(valid[:, None, None, :], logits, NEG_INF)
    p = jax.nn.softmax(logits, axis=-1)
    o = jnp.einsum('qngs,qsnd->qngd', p, vs.astype(jnp.float32))
    return o.reshape(nq, N_HEADS * HEAD_DIM).astype(q.dtype)


def _prompt_attention(q, k, v, qi, ki, wi, pos):
    bsz, t = q.shape[0], q.shape[1]
    nb = t // Q_BLOCK
    topk = min(TOPK_MAX, t // 4)

    def one_block(idx):
        b = idx // nb
        start = (idx % nb) * Q_BLOCK
        row = lambda a: lax.dynamic_index_in_dim(a, b, 0, keepdims=False)
        blk = lambda a: lax.dynamic_slice_in_dim(row(a), start, Q_BLOCK, 0)
        qb, qib, wib = blk(q), blk(qi), blk(wi)
        pos_q = lax.dynamic_slice_in_dim(pos, start, Q_BLOCK, 0)
        kb, vb, kib = row(k), row(v), row(ki)
        sel = _index_select(qib, wib, pos_q, kib, pos, topk)
        valid = pos[sel] <= pos_q[:, None]
        return _sparse_attend(qb, kb[sel], vb[sel], valid)

    out = lax.map(one_block, jnp.arange(bsz * nb, dtype=jnp.int32))
    return out.reshape(bsz, t, N_HEADS * HEAD_DIM)


def _sample_attention(q, k_new, v_new, qi, ki_new, wi, pos_q, pool_k, pool_v, pool_ki, page_table):
    db, ds = q.shape[0], q.shape[1]
    past = page_table.shape[1] * PAGE_SIZE
    total = past + ds
    topk = min(TOPK_MAX, total // 4)
    ki_past = pool_ki[page_table].reshape(db, past, IDX_DIM)
    ki_all = jnp.concatenate([ki_past, ki_new.astype(ki_past.dtype)], axis=1)
    pos_k = jnp.arange(total, dtype=jnp.int32)
    sel = jax.vmap(lambda a, w, kk: _index_select(a, w, pos_q, kk, pos_k, topk))(qi, wi, ki_all)
    in_past = sel < past
    ps = jnp.minimum(sel, past - 1)
    phys = jnp.take_along_axis(page_table, (ps // PAGE_SIZE).reshape(db, -1), axis=1).reshape(sel.shape)
    rows = phys * PAGE_SIZE + ps % PAGE_SIZE
    ns = jnp.clip(sel - past, 0, ds - 1)
    take_new = jax.vmap(lambda arr, idx: arr[idx])
    ks = jnp.where(in_past[..., None, None], pool_k.reshape(-1, N_KV_HEADS, HEAD_DIM)[rows], take_new(k_new, ns))
    vs = jnp.where(in_past[..., None, None], pool_v.reshape(-1, N_KV_HEADS, HEAD_DIM)[rows], take_new(v_new, ns))
    valid = sel <= pos_q[None, :, None]
    return jax.vmap(_sparse_attend)(q, ks, vs, valid)


def _peer(h, w_q, sub_keys, u_tab, v_tab):
    t = h.shape[0]
    pad = (-t) % PEER_BLOCK
    blocks = jnp.pad(h, ((0, pad), (0, 0))).reshape(-1, PEER_BLOCK, D_MODEL)

    def block_fn(hb):
        q = (hb @ w_q).reshape(PEER_BLOCK, PEER_HEADS, 2, PEER_QDIM // 2).astype(jnp.float32)
        s = jnp.einsum('thpd,phkd->thpk', q, sub_keys.astype(jnp.float32))
        s1, i1 = lax.top_k(s[:, :, 0], PEER_TOPK)
        s2, i2 = lax.top_k(s[:, :, 1], PEER_TOPK)
        cand = (s1[..., :, None] + s2[..., None, :]).reshape(PEER_BLOCK, PEER_HEADS, PEER_TOPK * PEER_TOPK)
        cidx = (i1[..., :, None] * PEER_KEYS + i2[..., None, :]).reshape(PEER_BLOCK, PEER_HEADS, PEER_TOPK * PEER_TOPK)
        top_s, top_pos = lax.top_k(cand, PEER_TOPK)
        e = jnp.take_along_axis(cidx, top_pos, axis=-1)
        g = jax.nn.softmax(top_s, axis=-1)
        act = jax.nn.gelu(jnp.einsum('thkd,td->thk', u_tab[e], hb).astype(jnp.float32))
        return jnp.einsum('thk,thkd->td', (g * act).astype(hb.dtype), v_tab[e])

    return lax.map(block_fn, blocks).reshape(-1, D_MODEL)[:t]


def _finish_layer(x, a_pre, attn_o, g_a, g_b, gate1, shift2, scale2, gate2, w_conv_out, w_attn_out, w_out,
                  norm2_w, peer_w_q, peer_sub_keys, peer_u, peer_v):
    y_a = a_pre @ w_conv_out
    y_b = attn_o @ w_attn_out
    mix = jax.nn.sigmoid(g_a) * y_a + jax.nn.sigmoid(g_b) * y_b
    x = x + gate1[:, None, :] * (mix @ w_out)
    h2 = _modulate(_rmsnorm(x, norm2_w), shift2, scale2)
    y = _peer(h2.reshape(-1, D_MODEL), peer_w_q, peer_sub_keys, peer_u, peer_v).reshape(x.shape)
    return x + gate2[:, None, :] * y


def setup_inputs(seed: int = 0) -> dict:
    key = jax.random.key(seed)
    ks = jax.random.split(key, 32)
    f32 = jnp.float32
    n_pages = PAST_LEN // PAGE_SIZE
    n_phys = (DEC_BATCH * n_pages * 5) // 4

    def nrm(k, shape, scale):
        return jax.random.normal(k, shape, f32) * scale

    page_table = jax.random.permutation(ks[8], n_phys)[:DEC_BATCH * n_pages].reshape(DEC_BATCH, n_pages).astype(jnp.int32)
    return {
        "x_prompt": nrm(ks[0], (BATCH, SEQ, D_MODEL), 1.0),
        "x_sample": nrm(ks[1], (DEC_BATCH, DEC_SEQ, D_MODEL), 1.0),
        "cache_k": nrm(ks[4], (DEPTH, n_phys, PAGE_SIZE, N_KV_HEADS, HEAD_DIM), 1.0),
        "cache_v": nrm(ks[5], (DEPTH, n_phys, PAGE_SIZE, N_KV_HEADS, HEAD_DIM), 1.0),
        "cache_kidx": nrm(ks[6], (DEPTH, n_phys, PAGE_SIZE, IDX_DIM), 1.0),
        "state_conv": nrm(ks[7], (DEPTH, DEC_BATCH, CONV_WIDTH - 1, D_CONV), 1.0),
        "page_table": page_table,
        "c_prompt": nrm(ks[2], (BATCH, D_MODEL), 1.0),
        "c_sample": nrm(ks[3], (DEC_BATCH, D_MODEL), 1.0),
        "w_ada": nrm(ks[9], (DEPTH, D_MODEL, N_MOD * D_MODEL), 0.1 * D_MODEL ** -0.5),
        "b_ada": nrm(ks[10], (DEPTH, N_MOD * D_MODEL), 0.01),
        "norm1_w": 1.0 + nrm(ks[11], (DEPTH, D_MODEL), 0.01),
        "w_in": nrm(ks[12], (DEPTH, D_MODEL, PROJ_WIDTH), D_MODEL ** -0.5),
        "conv_w": nrm(ks[13], (DEPTH, CONV_WIDTH, D_CONV), CONV_WIDTH ** -0.5),
        "q_norm_w": 1.0 + nrm(ks[14], (DEPTH, HEAD_DIM), 0.01),
        "k_norm_w": 1.0 + nrm(ks[15], (DEPTH, HEAD_DIM), 0.01),
        "w_conv_out": nrm(ks[16], (DEPTH, D_CONV, D_MODEL), D_CONV ** -0.5),
        "w_attn_out": nrm(ks[17], (DEPTH, N_HEADS * HEAD_DIM, D_MODEL), (N_HEADS * HEAD_DIM) ** -0.5),
        "w_out": nrm(ks[18], (DEPTH, D_MODEL, D_MODEL), D_MODEL ** -0.5),
        "norm2_w": 1.0 + nrm(ks[19], (DEPTH, D_MODEL), 0.01),
        "peer_w_q": nrm(ks[20], (DEPTH, D_MODEL, PEER_HEADS * PEER_QDIM), D_MODEL ** -0.5),
        "peer_sub_keys": nrm(ks[21], (DEPTH, 2, PEER_HEADS, PEER_KEYS, PEER_QDIM // 2), (PEER_QDIM // 2) ** -0.5),
        "peer_u": nrm(ks[22], (DEPTH, PEER_N_EXPERTS, D_MODEL), D_MODEL ** -0.5),
        "peer_v": nrm(ks[23], (DEPTH, PEER_N_EXPERTS, D_MODEL), PEER_HEADS ** -0.5),
    }


def reference(x_prompt, x_sample, cache_k, cache_v, cache_kidx, state_conv, page_table, c_prompt, c_sample,
              w_ada, b_ada, norm1_w, w_in, conv_w, q_norm_w, k_norm_w, w_conv_out, w_attn_out, w_out,
              norm2_w, peer_w_q, peer_sub_keys, peer_u, peer_v):
    pos_p = jnp.arange(SEQ, dtype=jnp.int32)
    pos_s = PAST_LEN + jnp.arange(DEC_SEQ, dtype=jnp.int32)
    xp, xs = x_prompt, x_sample
    kp_l, vp_l, kip_l, cp_l = [], [], [], []
    ks_l, vs_l, kis_l, cs_l = [], [], [], []
    for layer in range(DEPTH):
        sh1, sc1, g1, sh2, sc2, g2 = _adaln(c_prompt, w_ada[layer], b_ada[layer])
        u, bg, q, k, v, qi, ki, wi, ga, gb = _mixer_inputs(xp, sh1, sc1, norm1_w[layer], w_in[layer],
                                                           q_norm_w[layer], k_norm_w[layer], pos_p)
        y_conv, conv_p = _causal_conv(u, jnp.zeros((xp.shape[0], CONV_WIDTH - 1, D_CONV), u.dtype), conv_w[layer])
        attn = _prompt_attention(q, k, v, qi, ki, wi, pos_p)
        xp = _finish_layer(xp, bg * y_conv, attn, ga, gb, g1, sh2, sc2, g2, w_conv_out[layer], w_attn_out[layer],
                           w_out[layer], norm2_w[layer], peer_w_q[layer], peer_sub_keys[layer], peer_u[layer], peer_v[layer])
        kp_l.append(k); vp_l.append(v); kip_l.append(ki); cp_l.append(conv_p)
        sh1, sc1, g1, sh2, sc2, g2 = _adaln(c_sample, w_ada[layer], b_ada[layer])
        u, bg, q, k, v, qi, ki, wi, ga, gb = _mixer_inputs(xs, sh1, sc1, norm1_w[layer], w_in[layer],
                                                           q_norm_w[layer], k_norm_w[layer], pos_s)
        y_conv, conv_s = _causal_conv(u, state_conv[layer], conv_w[layer])
        attn = _sample_attention(q, k, v, qi, ki, wi, pos_s, cache_k[layer], cache_v[layer], cache_kidx[layer], page_table)
        xs = _finish_layer(xs, bg * y_conv, attn, ga, gb, g1, sh2, sc2, g2, w_conv_out[layer], w_attn_out[layer],
                           w_out[layer], norm2_w[layer], peer_w_q[layer], peer_sub_keys[layer], peer_u[layer], peer_v[layer])
        ks_l.append(k); vs_l.append(v); kis_l.append(ki); cs_l.append(conv_s)
    k_prompt, v_prompt = jnp.stack(kp_l), jnp.stack(vp_l)
    kidx_prompt, conv_prompt = jnp.stack(kip_l), jnp.stack(cp_l)
    k_sample, v_sample = jnp.stack(ks_l), jnp.stack(vs_l)
    kidx_sample, conv_sample = jnp.stack(kis_l), jnp.stack(cs_l)
    return (xp, xs, k_prompt, v_prompt, kidx_prompt, conv_prompt, k_sample, v_sample, kidx_sample, conv_sample)
```

```python
import functools

import numpy as np
import jax
import jax.numpy as jnp
from jax import lax
from jax.experimental import pallas as pl
from jax.experimental.pallas import tpu as pltpu

F32 = jnp.float32
BF16 = jnp.bfloat16
I32 = jnp.int32

N_HEADS = 8
N_KV_HEADS = 2
HEAD_DIM = 64
IDX_HEADS = 8
IDX_DIM = 64
IDX_SCALE = (IDX_DIM ** -0.5) * (IDX_HEADS ** -0.5)
TOPK_MAX = 256
Q_BLOCK = 128
ROPE_THETA = 10000.0
PEER_HEADS = 8
PEER_KEYS = 128
PEER_QDIM = 256
PEER_TOPK = 16
N_MOD = 6
NORM_EPS = 1e-6
NEG_INF = -1e30
INT_MIN = -2 ** 31

LANES = 128
SUBLANES = 8
VMEM_LIMIT = 56 * 1024 * 1024


def _cparams(sem):
    return pltpu.CompilerParams(dimension_semantics=sem, vmem_limit_bytes=VMEM_LIMIT)


def _dot(a, b):
    return jnp.dot(a, b, preferred_element_type=F32)


def _dot_nt(a, b):
    return lax.dot_general(a, b, (((1,), (1,)), ((), ())), preferred_element_type=F32)


def _adaln_body(c_ref, w_ref, b_ref, o_ref):
    o_ref[...] = jnp.dot(c_ref[...], w_ref[...], preferred_element_type=F32,
                         precision=lax.Precision.HIGHEST) + b_ref[...]


def _adaln(c, w_ada, b_ada):
    m, d = c.shape
    n = w_ada.shape[1]
    tn = d
    return pl.pallas_call(
        _adaln_body,
        grid=(n // tn,),
        in_specs=[pl.BlockSpec((m, d), lambda j: (0, 0)),
                  pl.BlockSpec((d, tn), lambda j: (0, j)),
                  pl.BlockSpec((1, tn), lambda j: (0, j))],
        out_specs=pl.BlockSpec((m, tn), lambda j: (0, j)),
        out_shape=jax.ShapeDtypeStruct((m, n), F32),
        compiler_params=_cparams(("arbitrary",)),
    )(c, w_ada, b_ada.reshape(1, n))


def _rope_lanes(z, cos, sin_signed):
    w = z.shape[1]
    reps = w // LANES
    cosw = jnp.concatenate([cos] * reps, axis=1) if reps > 1 else cos
    sinw = jnp.concatenate([sin_signed] * reps, axis=1) if reps > 1 else sin_signed
    half = HEAD_DIM // 2
    lane = lax.broadcasted_iota(I32, (1, w), 1)
    first = (lane % HEAD_DIM) < half
    swapped = jnp.where(first, pltpu.roll(z, w - half, 1), pltpu.roll(z, half, 1))
    return z * cosw + swapped * sinw


def _head_rms(z, nw, gsum):
    z2 = z * z
    hi = z2.astype(BF16)
    lo = (z2 - hi.astype(F32)).astype(BF16)
    ssum = _dot(hi, gsum) + _dot(lo, gsum)
    return z * lax.rsqrt(ssum * (1.0 / HEAD_DIM) + NORM_EPS) * nw


def _inproj_body(tm, seq_rows, dc, x_ref, shift_ref, scale_ref, n1w_ref, w_ref, cos_ref, sin_ref, qnw_ref, knw_ref,
                 gsum_ref, convw_ref, wco_ref, ext1_ref, ext2_ref,
                 u_ref, q_ref, k_ref, v_ref, qi_ref, kiw_ref, yag_ref, sgb_ref, carry_ref, buf_ref):
    i = pl.program_id(0)
    hq = N_HEADS * HEAD_DIM
    hk = N_KV_HEADS * HEAD_DIM
    hi_w = IDX_HEADS * IDX_DIM
    d = x_ref.shape[1]
    offs = np.cumsum([0, dc, dc, dc, hq, hk, hk, hi_w, LANES, d, d])

    def proj(s):
        return _dot(hb, w_ref[:, int(offs[s]):int(offs[s + 1])])

    x = x_ref[...]
    ms = jnp.mean(x * x, axis=-1, keepdims=True)
    h = x * lax.rsqrt(ms + NORM_EPS) * n1w_ref[...]
    h = h * (1.0 + scale_ref[...]) + shift_ref[...]
    hb = h.astype(BF16)

    u = proj(2) * proj(0)
    u_ref[...] = u

    @pl.when(i == 0)
    def _():
        carry_ref[...] = jnp.zeros_like(carry_ref)

    buf_ref[0:SUBLANES, :] = carry_ref[...]
    buf_ref[SUBLANES:SUBLANES + tm, :] = u
    carry_ref[...] = u[tm - SUBLANES:tm, :]
    pos = (i * tm + lax.broadcasted_iota(I32, (tm, 1), 0)) % seq_rows
    u1 = jnp.where(pos >= 1, buf_ref[SUBLANES - 1:SUBLANES - 1 + tm, :], ext1_ref[...])
    u2 = jnp.where(pos >= 2, buf_ref[SUBLANES - 2:SUBLANES - 2 + tm, :], ext2_ref[...])
    yc = convw_ref[2:3, :] * u + convw_ref[0:1, :] * u2 + convw_ref[1:2, :] * u1
    a_pre = proj(1) * yc
    ya = _dot(a_pre.astype(BF16), wco_ref[...])
    yag_ref[...] = jax.nn.sigmoid(proj(8)) * ya
    sgb_ref[...] = jax.nn.sigmoid(proj(9))

    cos = cos_ref[...]
    sin = sin_ref[...]
    gsum = gsum_ref[...]
    q_ref[...] = _rope_lanes(_head_rms(proj(3), qnw_ref[...], gsum), cos, sin)
    k_ref[...] = _rope_lanes(_head_rms(proj(4), knw_ref[:, 0:hk], gsum[0:hk, 0:hk]), cos, sin)
    v_ref[...] = proj(5)
    qi_ref[...] = _rope_lanes(proj(6), cos, sin)
    slab = proj(7)
    lane = lax.broadcasted_iota(I32, (1, LANES), 1)
    kiw_ref[...] = jnp.where(lane < IDX_DIM, _rope_lanes(slab, cos, sin), slab * IDX_SCALE)


def _inproj(x2, shift, scale, n1w, w_pad, cos, sin, qnw, knw, gsum, convw, wco, ext1, ext2, *, tm, seq_rows,
            per_row_mod):
    rows, d = x2.shape
    dc = convw.shape[1]
    hq = N_HEADS * HEAD_DIM
    hk = N_KV_HEADS * HEAD_DIM
    hi_w = IDX_HEADS * IDX_DIM
    nt = rows // tm
    pos_tiles = cos.shape[0] // tm
    tiles_per_seq = max(seq_rows // tm, 1)
    if per_row_mod:
        mod_spec = pl.BlockSpec((tm, d), lambda i: (i, 0))
        ext_spec = pl.BlockSpec((tm, dc), lambda i: (i, 0))
    else:
        mod_spec = pl.BlockSpec((None, 1, d), lambda i: (i // tiles_per_seq, 0, 0))
        ext_spec = pl.BlockSpec((1, dc), lambda i: (0, 0))
    const = lambda shape: pl.BlockSpec(shape, lambda i: tuple(0 for _ in shape))
    row = lambda w: pl.BlockSpec((tm, w), lambda i: (i, 0))
    out_widths = [dc, hq, hk, hk, hi_w, LANES, d, d]
    return pl.pallas_call(
        functools.partial(_inproj_body, tm, seq_rows, dc),
        grid=(nt,),
        in_specs=[row(d), mod_spec, mod_spec, const((1, d)), const(w_pad.shape),
                  pl.BlockSpec((tm, LANES), lambda i: (i % pos_tiles, 0)),
                  pl.BlockSpec((tm, LANES), lambda i: (i % pos_tiles, 0)),
                  const((1, hq)), const((1, hq)), const((hq, hq)), const(convw.shape), const(wco.shape),
                  ext_spec, ext_spec],
        out_specs=[row(w) for w in out_widths],
        out_shape=[jax.ShapeDtypeStruct((rows, w), F32) for w in out_widths],
        scratch_shapes=[pltpu.VMEM((SUBLANES, dc), F32), pltpu.VMEM((tm + SUBLANES, dc), F32)],
        compiler_params=_cparams(("arbitrary",)),
    )(x2, shift, scale, n1w, w_pad, cos, sin, qnw, knw, gsum, convw, wco, ext1, ext2)


def _sortable_key(score):
    bits = lax.bitcast_convert_type(score, I32)
    key = jnp.where(bits < 0, bits ^ jnp.int32(0x7FFFFFFF), bits)
    return jnp.where(score == 0.0, jnp.int32(0), key)


def _kth_largest_key(count_ge, shape, k):
    ans = jnp.where(count_ge(jnp.zeros(shape, I32)) >= k, jnp.int32(0), jnp.int32(INT_MIN))

    def step(it, ans):
        cand = ans + jnp.left_shift(jnp.int32(1), jnp.int32(30) - it)
        return jnp.where(count_ge(cand) >= k, cand, ans)

    return lax.fori_loop(0, 31, step, ans)


def _tie_cut(count_eq_below, shape, need, nbits):
    def step(it, c):
        cand = c + jnp.left_shift(jnp.int32(1), jnp.int32(nbits - 1) - it)
        return jnp.where(count_eq_below(cand) < need, cand, c)

    return lax.fori_loop(0, nbits, step, jnp.zeros(shape, I32))


def _pattn_body(tq, t, topk, q_ref, qi_ref, wi_ref, k_ref, v_ref, ki_ref, o_ref, key_ref, cut_ref):
    j = pl.program_id(1)
    qi = qi_ref[...]
    wi = wi_ref[...]
    ki = ki_ref[...].astype(BF16)
    score = jnp.zeros((tq, t), F32)
    for h in range(IDX_HEADS):
        s = _dot_nt(qi[:, h * IDX_DIM:(h + 1) * IDX_DIM].astype(BF16), ki)
        score = score + wi[:, h:h + 1] * jnp.maximum(s, 0.0)
    qpos = j * tq + lax.broadcasted_iota(I32, (tq, 1), 0)
    kpos = lax.broadcasted_iota(I32, (1, t), 1)
    causal = kpos <= qpos
    score = jnp.where(causal, score, NEG_INF)
    key_ref[...] = _sortable_key(score)

    def count_ge(c):
        return jnp.sum((key_ref[...] >= c).astype(F32), axis=1, keepdims=True)

    thr = _kth_largest_key(count_ge, (tq, 1), topk)
    key = key_ref[...]
    gt = key > thr
    eq = key == thr
    need = topk - jnp.sum(gt.astype(F32), axis=1, keepdims=True)
    n_eq = jnp.sum(eq.astype(F32), axis=1, keepdims=True)
    cut_ref[...] = jnp.full(cut_ref.shape, t, I32)

    @pl.when(jnp.max(n_eq - need) > 0)
    def _():
        def count_eq_below(c):
            hit = (key_ref[...] == thr) & (kpos < c)
            return jnp.sum(hit.astype(F32), axis=1, keepdims=True)

        nbits = int(t - 1).bit_length()
        cut = _tie_cut(count_eq_below, (tq, 1), need, nbits)
        cut_ref[...] = jnp.broadcast_to(cut, cut_ref.shape)

    cut = cut_ref[:, 0:1]
    mask = (gt | (eq & (kpos <= cut))) & causal

    q = q_ref[...] * (HEAD_DIM ** -0.5)
    k = k_ref[...].astype(BF16)
    v = v_ref[...].astype(BF16)
    group = N_HEADS // N_KV_HEADS
    outs = []
    for h in range(N_HEADS):
        n = h // group
        logits = _dot_nt(q[:, h * HEAD_DIM:(h + 1) * HEAD_DIM].astype(BF16), k[:, n * HEAD_DIM:(n + 1) * HEAD_DIM])
        logits = jnp.where(mask, logits, NEG_INF)
        m = jnp.max(logits, axis=1, keepdims=True)
        p = jnp.exp(logits - m)
        l = jnp.sum(p, axis=1, keepdims=True)
        o = _dot(p.astype(BF16), v[:, n * HEAD_DIM:(n + 1) * HEAD_DIM])
        outs.append(o / l)
    o_ref[...] = jnp.concatenate(outs, axis=1)


def _pattn(q, qi, wi, k, v, ki):
    b, t, hq = q.shape
    tq = Q_BLOCK
    topk = min(TOPK_MAX, t // 4)
    qspec = lambda w: pl.BlockSpec((None, tq, w), lambda bi, j: (bi, j, 0))
    kspec = lambda w: pl.BlockSpec((None, t, w), lambda bi, j: (bi, 0, 0))
    return pl.pallas_call(
        functools.partial(_pattn_body, tq, t, topk),
        grid=(b, t // tq),
        in_specs=[qspec(hq), qspec(qi.shape[2]), qspec(wi.shape[2]), kspec(k.shape[2]), kspec(v.shape[2]),
                  kspec(ki.shape[2])],
        out_specs=qspec(hq),
        out_shape=jax.ShapeDtypeStruct((b, t, hq), F32),
        scratch_shapes=[pltpu.VMEM((tq, t), I32), pltpu.VMEM((tq, LANES), I32)],
        compiler_params=_cparams(("arbitrary", "arbitrary")),
    )(q, qi, wi, k, v, ki)


def _sattn_body(n_pages, ds, topk, pt_ref, qi_ref, w_ref, q_ref, kin_ref, kn_ref, vn_ref, kic_ref, kc_ref, vc_ref,
                o_ref, score_ref, mask_ref, m_ref, l_ref, acc_ref):
    ph = pl.program_id(1)
    p = pl.program_id(2)
    group = N_HEADS // N_KV_HEADS
    sub = lax.broadcasted_iota(I32, (SUBLANES, LANES), 0)
    lane = lax.broadcasted_iota(I32, (SUBLANES, LANES), 1)
    new_valid = (lane <= sub) & (lane < ds)

    def page_score(ki_page):
        s = _dot_nt(qi_ref[...].astype(BF16), ki_page.astype(BF16))
        w = w_ref[...]
        acc = jnp.zeros((SUBLANES, LANES), F32)
        for h in range(IDX_HEADS):
            r = slice(h * SUBLANES, (h + 1) * SUBLANES)
            acc = acc + w[r, :] * jnp.maximum(s[r, :], 0.0)
        return acc

    @pl.when(ph == 0)
    def _():
        score_ref[p] = page_score(kic_ref[...])

        @pl.when(p == 0)
        def _():
            score_ref[n_pages] = jnp.where(new_valid, page_score(kin_ref[...]), NEG_INF)

    def attend(mask8, k_page, v_page):
        maskg = jnp.concatenate([mask8] * group, axis=0) > 0.5
        kb = k_page.astype(BF16)
        vb = v_page.astype(BF16)
        for n in range(N_KV_HEADS):
            qn = (q_ref[n] * (HEAD_DIM ** -0.5)).astype(BF16)
            logits = _dot_nt(qn, kb[:, n * HEAD_DIM:(n + 1) * HEAD_DIM])
            logits = jnp.where(maskg, logits, NEG_INF)
            m_old = m_ref[n]
            m_new = jnp.maximum(m_old, jnp.max(logits, axis=1, keepdims=True))
            pr = jnp.where(maskg, jnp.exp(logits - m_new), 0.0)
            alpha = jnp.exp(m_old - m_new)
            l_ref[n] = alpha * l_ref[n] + jnp.sum(pr, axis=1, keepdims=True)
            acc_ref[n] = alpha[:, 0:HEAD_DIM] * acc_ref[n] + _dot(pr.astype(BF16), vb[:, n * HEAD_DIM:(n + 1) * HEAD_DIM])
            m_ref[n] = m_new

    @pl.when(ph == 1)
    def _():
        @pl.when(p == 0)
        def _():
            key = _sortable_key(score_ref[...])
            slab = lax.broadcasted_iota(I32, key.shape, 0)
            idx = slab * LANES + lax.broadcasted_iota(I32, key.shape, 2)

            def count(hit):
                c = jnp.sum(hit.astype(F32), axis=0)
                return jnp.sum(c, axis=1, keepdims=True)[None]

            thr = _kth_largest_key(lambda c: count(key >= c), (1, SUBLANES, 1), topk)
            gt = key > thr
            eq = key == thr
            need = topk - count(gt)
            nbits = int((n_pages + 1) * LANES - 1).bit_length()
            cut = _tie_cut(lambda c: count(eq & (idx < c)), (1, SUBLANES, 1), need, nbits)
            sel = gt | (eq & (idx <= cut))
            valid = (slab < n_pages) | new_valid[None]
            mask_ref[...] = (sel & valid).astype(F32)
            m_ref[...] = jnp.full(m_ref.shape, NEG_INF, F32)
            l_ref[...] = jnp.zeros(l_ref.shape, F32)
            acc_ref[...] = jnp.zeros(acc_ref.shape, F32)
            attend(mask_ref[n_pages], kn_ref[...], vn_ref[...])

        attend(mask_ref[p], kc_ref[...], vc_ref[...])

        @pl.when(p == n_pages - 1)
        def _():
            o_ref[...] = acc_ref[...] / l_ref[...][:, :, 0:HEAD_DIM]


def _sattn(page_table, qi_arr, w_arr, q_arr, kin, kn, vn, cache_kidx, cache_k, cache_v, *, ds):
    db, n_pages = page_table.shape
    page = cache_k.shape[1]
    topk = min(TOPK_MAX, (n_pages * page + ds) // 4)
    group = N_HEADS // N_KV_HEADS
    rows_i = IDX_HEADS * SUBLANES
    rows_q = group * SUBLANES
    pt = page_table.reshape(-1)
    per_b = lambda *shape: pl.BlockSpec((None,) + shape, lambda b, ph, p, pt: (b,) + tuple(0 for _ in shape))
    kic_spec = pl.BlockSpec((None, page, IDX_DIM),
                            lambda b, ph, p, pt: (pt[b * n_pages + p * (1 - ph) + (n_pages - 1) * ph], 0, 0))
    kvc_spec = pl.BlockSpec((None, page, N_KV_HEADS * HEAD_DIM), lambda b, ph, p, pt: (pt[b * n_pages + p * ph], 0, 0))
    grid_spec = pltpu.PrefetchScalarGridSpec(
        num_scalar_prefetch=1,
        grid=(db, 2, n_pages),
        in_specs=[per_b(rows_i, IDX_DIM), per_b(rows_i, LANES), per_b(N_KV_HEADS, rows_q, HEAD_DIM),
                  per_b(page, IDX_DIM), per_b(page, N_KV_HEADS * HEAD_DIM), per_b(page, N_KV_HEADS * HEAD_DIM),
                  kic_spec, kvc_spec, kvc_spec],
        out_specs=per_b(N_KV_HEADS, rows_q, HEAD_DIM),
        scratch_shapes=[pltpu.VMEM((n_pages + 1, SUBLANES, LANES), F32),
                        pltpu.VMEM((n_pages + 1, SUBLANES, LANES), F32),
                        pltpu.VMEM((N_KV_HEADS, rows_q, LANES), F32),
                        pltpu.VMEM((N_KV_HEADS, rows_q, LANES), F32),
                        pltpu.VMEM((N_KV_HEADS, rows_q, HEAD_DIM), F32)],
    )
    return pl.pallas_call(
        functools.partial(_sattn_body, n_pages, ds, topk),
        grid_spec=grid_spec,
        out_shape=jax.ShapeDtypeStruct((db, N_KV_HEADS, rows_q, HEAD_DIM), F32),
        compiler_params=_cparams(("arbitrary", "arbitrary", "arbitrary")),
    )(pt, qi_arr, w_arr, q_arr, kin, kn, vn, cache_kidx, cache_k, cache_v)


def _finish_body(x_ref, yag_ref, sgb_ref, attn_ref, gate_ref, shift_ref, scale_ref, n2w_ref, wao_ref, wout_ref, wq_ref,
                 x1_ref, h2_ref, pq_ref):
    yb = _dot(attn_ref[...].astype(BF16), wao_ref[...])
    mix = yag_ref[...] + sgb_ref[...] * yb
    x1 = x_ref[...] + gate_ref[...] * _dot(mix.astype(BF16), wout_ref[...])
    x1_ref[...] = x1
    ms = jnp.mean(x1 * x1, axis=-1, keepdims=True)
    h2 = x1 * lax.rsqrt(ms + NORM_EPS) * n2w_ref[...]
    h2 = (h2 * (1.0 + scale_ref[...]) + shift_ref[...]).astype(BF16)
    h2_ref[...] = h2
    pq_ref[...] = _dot(h2, wq_ref[...]).astype(BF16)


def _finish(x2, yag, sgb, attn, gate, shift, scale, n2w, wao, wout, wq, *, tm, seq_rows, per_row_mod):
    rows, d = x2.shape
    nt = rows // tm
    tiles_per_seq = max(seq_rows // tm, 1)
    if per_row_mod:
        mod_spec = pl.BlockSpec((tm, d), lambda i: (i, 0))
    else:
        mod_spec = pl.BlockSpec((None, 1, d), lambda i: (i // tiles_per_seq, 0, 0))
    const = lambda shape: pl.BlockSpec(shape, lambda i: tuple(0 for _ in shape))
    row = lambda w: pl.BlockSpec((tm, w), lambda i: (i, 0))
    nq = wq.shape[1]
    return pl.pallas_call(
        _finish_body,
        grid=(nt,),
        in_specs=[row(d), row(d), row(d), row(attn.shape[1]), mod_spec, mod_spec, mod_spec, const((1, d)),
                  const(wao.shape), const(wout.shape), const(wq.shape)],
        out_specs=[row(d), row(d), row(nq)],
        out_shape=[jax.ShapeDtypeStruct((rows, d), F32), jax.ShapeDtypeStruct((rows, d), BF16),
                   jax.ShapeDtypeStruct((rows, nq), BF16)],
        compiler_params=_cparams(("arbitrary",)),
    )(x2, yag, sgb, attn, gate, shift, scale, n2w, wao, wout, wq)


def _top_sorted(x, n):
    vals = []
    for _ in range(n):
        m = jnp.max(x, axis=0, keepdims=True)
        vals.append(m)
        x = jnp.where(x >= m, -jnp.inf, x)
    return jnp.concatenate(vals, axis=0)


def _gelu_tanh(x):
    c = np.float32(np.sqrt(2.0 / np.pi))
    return x * (0.5 * (1.0 + jnp.tanh(c * (x + 0.044715 * (x * x * x)))))


def _peer_body(ce, h2_ref, pq_ref, x1_ref, gate_ref, sk_ref, u_ref, vt_ref, out_ref,
               cm_ref, b_ref, ea_ref, eb_ref, yt_ref, p_ref):
    c = pl.program_id(1)
    nc = pl.num_programs(1)
    kk = PEER_KEYS
    half = PEER_QDIM // 2

    @pl.when(c == 0)
    def _():
        yt_ref[...] = jnp.zeros_like(yt_ref)
        for h in range(PEER_HEADS):
            sa = _dot_nt(sk_ref[2 * h], pq_ref[:, (2 * h) * half:(2 * h + 1) * half])
            sb = _dot_nt(sk_ref[2 * h + 1], pq_ref[:, (2 * h + 1) * half:(2 * h + 2) * half])
            ta = _top_sorted(sa, PEER_TOPK + 1)
            tb = _top_sorted(sb, PEER_TOPK + 1)
            tail = jnp.concatenate([ta[0:1] + tb[16:17], ta[16:17] + tb[0:1],
                                    jnp.full((SUBLANES - 2, sa.shape[1]), -jnp.inf, F32)], axis=0)
            cand = jnp.concatenate(
                [ta[0:1] + tb[0:16]] + [ta[r:r + 1] + tb[0:8] for r in range(1, 8)] + [ta[8:16] + tb[0:1], tail],
                axis=0)
            top = _top_sorted(cand, PEER_TOPK + 1)
            thr = 0.5 * (top[PEER_TOPK - 1:PEER_TOPK] + top[PEER_TOPK:PEER_TOPK + 1])
            z = jnp.sum(jnp.exp(top[0:PEER_TOPK] - top[0:1]), axis=0, keepdims=True)
            cm_ref[h] = thr - sa
            b_ref[h] = sb
            ea_ref[h] = jnp.exp(sa - ta[0:1])
            eb_ref[h] = jnp.exp(sb - tb[0:1]) / z

    act = _gelu_tanh(_dot_nt(u_ref[...], h2_ref[...]))
    for ii in range(ce // kk):
        i = c * (ce // kk) + ii
        w = None
        for h in range(PEER_HEADS):
            cmi = cm_ref[h, pl.ds(i, 1), :]
            eai = ea_ref[h, pl.ds(i, 1), :]
            term = jnp.where(b_ref[h] >= cmi, eb_ref[h], 0.0) * eai
            w = term if w is None else w + term
        p_ref[ii * kk:(ii + 1) * kk, :] = (w * act[ii * kk:(ii + 1) * kk, :]).astype(BF16)
    yt_ref[...] += _dot(vt_ref[...], p_ref[...])

    @pl.when(c == nc - 1)
    def _():
        out_ref[...] = x1_ref[...] + gate_ref[...] * yt_ref[...].T


def _peer(h2, pq, x1, gate, sk, u_bf, vt_bf, *, tm, ce, seq_rows, per_row_mod):
    rows, d = x1.shape
    ne = u_bf.shape[0]
    nt = rows // tm
    tiles_per_seq = max(seq_rows // tm, 1)
    if per_row_mod:
        mod_spec = pl.BlockSpec((tm, d), lambda i, c: (i, 0))
    else:
        mod_spec = pl.BlockSpec((None, 1, d), lambda i, c: (i // tiles_per_seq, 0, 0))
    row = lambda w: pl.BlockSpec((tm, w), lambda i, c: (i, 0))
    return pl.pallas_call(
        functools.partial(_peer_body, ce),
        grid=(nt, ne // ce),
        in_specs=[row(d), row(pq.shape[1]), row(d), mod_spec,
                  pl.BlockSpec(sk.shape, lambda i, c: (0, 0, 0)),
                  pl.BlockSpec((ce, d), lambda i, c: (c, 0)),
                  pl.BlockSpec((d, ce), lambda i, c: (0, c))],
        out_specs=row(d),
        out_shape=jax.ShapeDtypeStruct((rows, d), F32),
        scratch_shapes=[pltpu.VMEM((PEER_HEADS, PEER_KEYS, tm), F32) for _ in range(4)]
                       + [pltpu.VMEM((d, tm), F32), pltpu.VMEM((ce, tm), BF16)],
        compiler_params=_cparams(("arbitrary", "arbitrary")),
    )(h2, pq, x1, gate, sk, u_bf, vt_bf)


def _rope_tables(pos):
    half = HEAD_DIM // 2
    freqs = ROPE_THETA ** (-jnp.arange(half, dtype=F32) / half)
    ang = pos.astype(F32)[:, None] * freqs[None, :]
    cos, sin = jnp.cos(ang), jnp.sin(ang)
    cos2 = jnp.concatenate([cos, cos, cos, cos], axis=1)
    sin2 = jnp.concatenate([-sin, sin, -sin, sin], axis=1)
    return cos2, sin2


def _pick_tile(rows, pref):
    t = min(pref, rows)
    while rows % t:
        t //= 2
    return t


def kernel(x_prompt, x_sample, cache_k, cache_v, cache_kidx, state_conv, page_table, c_prompt, c_sample, w_ada, b_ada,
           norm1_w, w_in, conv_w, q_norm_w, k_norm_w, w_conv_out, w_attn_out, w_out, norm2_w, peer_w_q, peer_sub_keys,
           peer_u, peer_v):
    bsz, seq, d = x_prompt.shape
    db, ds, _ = x_sample.shape
    depth = w_ada.shape[0]
    dc = conv_w.shape[2]
    hq = N_HEADS * HEAD_DIM
    hk = N_KV_HEADS * HEAD_DIM
    hi_w = IDX_HEADS * IDX_DIM
    n_pages = page_table.shape[1]
    page = cache_k.shape[2]
    past = n_pages * page
    group = N_HEADS // N_KV_HEADS

    pos_p = jnp.arange(seq, dtype=I32)
    pos_s = past + jnp.arange(ds, dtype=I32)
    cos_p, sin_p = _rope_tables(pos_p)
    cos_s, sin_s = _rope_tables(jnp.tile(pos_s, db))
    gsum = jnp.kron(jnp.eye(N_HEADS, dtype=F32), jnp.ones((HEAD_DIM, HEAD_DIM), F32)).astype(BF16)

    xp = x_prompt.reshape(bsz * seq, d)
    xs = x_sample.reshape(db * ds, d)
    outs = {n: [] for n in ("kp", "vp", "kip", "cp", "ks", "vs", "kis", "cs")}

    for layer in range(depth):
        mod = _adaln(jnp.concatenate([c_prompt, c_sample], axis=0), w_ada[layer], b_ada[layer])
        mod_p = [m.reshape(bsz, 1, d) for m in jnp.split(mod[:bsz], N_MOD, axis=-1)]
        mod_s = [jnp.repeat(m, ds, axis=0) for m in jnp.split(mod[bsz:], N_MOD, axis=-1)]

        w = w_in[layer]
        o_ki = 3 * dc + hq + 2 * hk + hi_w
        o_g = o_ki + IDX_DIM + IDX_HEADS
        slab = jnp.concatenate([w[:, o_ki:o_g], jnp.zeros((d, LANES - IDX_DIM - IDX_HEADS), F32)], axis=1)
        w_pad = jnp.concatenate([w[:, :o_ki], slab, w[:, o_g:]], axis=1).astype(BF16)
        n1w = norm1_w[layer].reshape(1, d)
        qnw = jnp.tile(q_norm_w[layer], N_HEADS).reshape(1, hq)
        knw = jnp.tile(k_norm_w[layer], N_HEADS).reshape(1, hq)
        wco = w_conv_out[layer].astype(BF16)
        wao = w_attn_out[layer].astype(BF16)
        wout = w_out[layer].astype(BF16)
        wq = peer_w_q[layer].astype(BF16)
        n2w = norm2_w[layer].reshape(1, d)
        sk = jnp.transpose(peer_sub_keys[layer], (1, 0, 2, 3)).reshape(2 * PEER_HEADS, PEER_KEYS, PEER_QDIM // 2)
        sk = sk.astype(BF16)
        u_bf = peer_u[layer].astype(BF16)
        vt_bf = peer_v[layer].astype(BF16).T
        ce = min(512, u_bf.shape[0])

        tm_p = _pick_tile(seq, 256)
        zero_ext = jnp.zeros((1, dc), F32)
        u, q, k, v, qi, kiw, yag, sgb = _inproj(
            xp, mod_p[0], mod_p[1], n1w, w_pad, cos_p, sin_p, qnw, knw, gsum, conv_w[layer], wco, zero_ext, zero_ext,
            tm=tm_p, seq_rows=seq, per_row_mod=False)
        ki = kiw[:, :IDX_DIM]
        wi = kiw[:, IDX_DIM:IDX_DIM + IDX_HEADS]
        attn = _pattn(q.reshape(bsz, seq, hq), qi.reshape(bsz, seq, hi_w), wi.reshape(bsz, seq, IDX_HEADS),
                      k.reshape(bsz, seq, hk), v.reshape(bsz, seq, hk), ki.reshape(bsz, seq, IDX_DIM))
        tm_f = _pick_tile(seq, 512)
        x1, h2, pq = _finish(xp, yag, sgb, attn.reshape(bsz * seq, hq), mod_p[2], mod_p[3], mod_p[4], n2w, wao, wout,
                             wq, tm=tm_f, seq_rows=seq, per_row_mod=False)
        xp = _peer(h2, pq, x1, mod_p[5], sk, u_bf, vt_bf, tm=tm_f, ce=ce, seq_rows=seq, per_row_mod=False)
        outs["kp"].append(k.reshape(bsz, seq, N_KV_HEADS, HEAD_DIM))
        outs["vp"].append(v.reshape(bsz, seq, N_KV_HEADS, HEAD_DIM))
        outs["kip"].append(ki.reshape(bsz, seq, IDX_DIM))
        outs["cp"].append(u.reshape(bsz, seq, dc)[:, seq - 2:, :])

        rows_s = db * ds
        st = state_conv[layer]
        posr = jnp.tile(jnp.arange(ds), db)
        ext1 = jnp.repeat(st[:, 1, :], ds, axis=0)
        ext2 = jnp.where((posr == 0)[:, None], jnp.repeat(st[:, 0, :], ds, axis=0), ext1)
        u, q, k, v, qi, kiw, yag, sgb = _inproj(
            xs, mod_s[0], mod_s[1], n1w, w_pad, cos_s, sin_s, qnw, knw, gsum, conv_w[layer], wco, ext1, ext2,
            tm=rows_s, seq_rows=ds, per_row_mod=True)
        ki = kiw[:, :IDX_DIM]
        wi = kiw[:, IDX_DIM:IDX_DIM + IDX_HEADS]
        pad_q = lambda a: jnp.pad(a, ((0, 0), (0, 0), (0, SUBLANES - ds), (0, 0)))
        qi_arr = pad_q(qi.reshape(db, ds, IDX_HEADS, IDX_DIM).transpose(0, 2, 1, 3)).reshape(db, IDX_HEADS * SUBLANES, IDX_DIM)
        w_arr = pad_q(wi.reshape(db, ds, IDX_HEADS, 1).transpose(0, 2, 1, 3)).reshape(db, IDX_HEADS * SUBLANES, 1)
        w_arr = jnp.broadcast_to(w_arr, (db, IDX_HEADS * SUBLANES, LANES))
        q_arr = pad_q(q.reshape(db, ds, N_HEADS, HEAD_DIM).transpose(0, 2, 1, 3)).reshape(db, N_KV_HEADS, group * SUBLANES, HEAD_DIM)
        pad_page = lambda a: jnp.pad(a.reshape(db, ds, -1), ((0, 0), (0, page - ds), (0, 0)))
        o = _sattn(page_table, qi_arr, w_arr, q_arr, pad_page(ki), pad_page(k), pad_page(v),
                   cache_kidx[layer], cache_k[layer].reshape(-1, page, hk), cache_v[layer].reshape(-1, page, hk), ds=ds)
        attn = o.reshape(db, N_HEADS, SUBLANES, HEAD_DIM)[:, :, :ds, :].transpose(0, 2, 1, 3).reshape(rows_s, hq)
        x1, h2, pq = _finish(xs, yag, sgb, attn, mod_s[2], mod_s[3], mod_s[4], n2w, wao, wout, wq,
                             tm=rows_s, seq_rows=ds, per_row_mod=True)
        xs = _peer(h2, pq, x1, mod_s[5], sk, u_bf, vt_bf, tm=rows_s, ce=ce, seq_rows=ds, per_row_mod=True)
        outs["ks"].append(k.reshape(db, ds, N_KV_HEADS, HEAD_DIM))
        outs["vs"].append(v.reshape(db, ds, N_KV_HEADS, HEAD_DIM))
        outs["kis"].append(ki.reshape(db, ds, IDX_DIM))
        full = jnp.concatenate([st, u.reshape(db, ds, dc)], axis=1)
        outs["cs"].append(full[:, full.shape[1] - 2:, :])

    stack = lambda n: jnp.stack(outs[n])
    return (xp.reshape(bsz, seq, d), xs.reshape(db, ds, d), stack("kp"), stack("vp"), stack("kip"), stack("cp"),
            stack("ks"), stack("vs"), stack("kis"), stack("cs"))
```

```python
import functools

import numpy as np
import jax
import jax.numpy as jnp
from jax import lax
from jax.experimental import pallas as pl
from jax.experimental.pallas import tpu as pltpu

F32 = jnp.float32
BF16 = jnp.bfloat16
I32 = jnp.int32

N_HEADS = 8
N_KV_HEADS = 2
HEAD_DIM = 64
IDX_HEADS = 8
IDX_DIM = 64
IDX_SCALE = (IDX_DIM ** -0.5) * (IDX_HEADS ** -0.5)
TOPK_MAX = 256
Q_BLOCK = 128
ROPE_THETA = 10000.0
PEER_HEADS = 8
PEER_KEYS = 128
PEER_QDIM = 256
PEER_TOPK = 16
N_MOD = 6
NORM_EPS = 1e-6
NEG_INF = -1e30
INT_MIN = -2 ** 31

LANES = 128
SUBLANES = 8
VMEM_LIMIT = 56 * 1024 * 1024


def _cparams(sem):
    return pltpu.CompilerParams(dimension_semantics=sem, vmem_limit_bytes=VMEM_LIMIT)


def _dot(a, b):
    return jnp.dot(a, b, preferred_element_type=F32)


def _dot_nt(a, b):
    return lax.dot_general(a, b, (((1,), (1,)), ((), ())), preferred_element_type=F32)


def _adaln_body(c_ref, w_ref, b_ref, o_ref):
    o_ref[...] = jnp.dot(c_ref[...], w_ref[...], preferred_element_type=F32,
                         precision=lax.Precision.HIGHEST) + b_ref[...]


def _adaln(c, w_ada, b_ada):
    m, d = c.shape
    n = w_ada.shape[1]
    tn = d
    return pl.pallas_call(
        _adaln_body,
        grid=(n // tn,),
        in_specs=[pl.BlockSpec((m, d), lambda j: (0, 0)),
                  pl.BlockSpec((d, tn), lambda j: (0, j)),
                  pl.BlockSpec((1, tn), lambda j: (0, j))],
        out_specs=pl.BlockSpec((m, tn), lambda j: (0, j)),
        out_shape=jax.ShapeDtypeStruct((m, n), F32),
        compiler_params=_cparams(("arbitrary",)),
    )(c, w_ada, b_ada.reshape(1, n))


def _rope_lanes(z, cos, sin_signed):
    w = z.shape[1]
    reps = w // LANES
    cosw = jnp.concatenate([cos] * reps, axis=1) if reps > 1 else cos
    sinw = jnp.concatenate([sin_signed] * reps, axis=1) if reps > 1 else sin_signed
    half = HEAD_DIM // 2
    lane = lax.broadcasted_iota(I32, (1, w), 1)
    first = (lane % HEAD_DIM) < half
    swapped = jnp.where(first, pltpu.roll(z, w - half, 1), pltpu.roll(z, half, 1))
    return z * cosw + swapped * sinw


def _head_rms(z, nw, gsum):
    z2 = z * z
    hi = z2.astype(BF16)
    lo = (z2 - hi.astype(F32)).astype(BF16)
    ssum = _dot(hi, gsum) + _dot(lo, gsum)
    return z * lax.rsqrt(ssum * (1.0 / HEAD_DIM) + NORM_EPS) * nw


def _inproj_body(tm, seq_rows, dc, x_ref, shift_ref, scale_ref, n1w_ref, w_ref, cos_ref, sin_ref, qnw_ref, knw_ref,
                 gsum_ref, convw_ref, wco_ref, ext1_ref, ext2_ref,
                 u_ref, q_ref, k_ref, v_ref, qi_ref, kiw_ref, yag_ref, sgb_ref, vt_ref, wit_ref, carry_ref, buf_ref):
    i = pl.program_id(0)
    hq = N_HEADS * HEAD_DIM
    hk = N_KV_HEADS * HEAD_DIM
    hi_w = IDX_HEADS * IDX_DIM
    d = x_ref.shape[1]
    offs = np.cumsum([0, dc, dc, dc, hq, hk, hk, hi_w, LANES, d, d])

    def proj(s):
        return _dot(hb, w_ref[:, int(offs[s]):int(offs[s + 1])])

    x = x_ref[...]
    ms = jnp.mean(x * x, axis=-1, keepdims=True)
    h = x * lax.rsqrt(ms + NORM_EPS) * n1w_ref[...]
    h = h * (1.0 + scale_ref[...]) + shift_ref[...]
    hb = h.astype(BF16)

    u = proj(2) * proj(0)
    u_ref[...] = u

    @pl.when(i == 0)
    def _():
        carry_ref[...] = jnp.zeros_like(carry_ref)

    buf_ref[0:SUBLANES, :] = carry_ref[...]
    buf_ref[SUBLANES:SUBLANES + tm, :] = u
    carry_ref[...] = u[tm - SUBLANES:tm, :]
    pos = (i * tm + lax.broadcasted_iota(I32, (tm, 1), 0)) % seq_rows
    u1 = jnp.where(pos >= 1, buf_ref[SUBLANES - 1:SUBLANES - 1 + tm, :], ext1_ref[...])
    u2 = jnp.where(pos >= 2, buf_ref[SUBLANES - 2:SUBLANES - 2 + tm, :], ext2_ref[...])
    yc = convw_ref[2:3, :] * u + convw_ref[0:1, :] * u2 + convw_ref[1:2, :] * u1
    a_pre = proj(1) * yc
    ya = _dot(a_pre.astype(BF16), wco_ref[...])
    yag_ref[...] = jax.nn.sigmoid(proj(8)) * ya
    sgb_ref[...] = jax.nn.sigmoid(proj(9))

    cos = cos_ref[...]
    sin = sin_ref[...]
    gsum = gsum_ref[...]
    q_ref[...] = _rope_lanes(_head_rms(proj(3), qnw_ref[...], gsum), cos, sin)
    k_ref[...] = _rope_lanes(_head_rms(proj(4), knw_ref[:, 0:hk], gsum[0:hk, 0:hk]), cos, sin)
    v = proj(5)
    v_ref[...] = v
    vt_ref[...] = v.T
    qi_ref[...] = _rope_lanes(proj(6), cos, sin)
    slab = proj(7)
    lane = lax.broadcasted_iota(I32, (1, LANES), 1)
    kiw = jnp.where(lane < IDX_DIM, _rope_lanes(slab, cos, sin), slab * IDX_SCALE)
    kiw_ref[...] = kiw
    wit_ref[...] = kiw.T[IDX_DIM:IDX_DIM + IDX_HEADS, :]


def _inproj(x2, shift, scale, n1w, w_pad, cos, sin, qnw, knw, gsum, convw, wco, ext1, ext2, *, tm, seq_rows,
            per_row_mod):
    rows, d = x2.shape
    dc = convw.shape[1]
    hq = N_HEADS * HEAD_DIM
    hk = N_KV_HEADS * HEAD_DIM
    hi_w = IDX_HEADS * IDX_DIM
    nt = rows // tm
    pos_tiles = cos.shape[0] // tm
    tiles_per_seq = max(seq_rows // tm, 1)
    if per_row_mod:
        mod_spec = pl.BlockSpec((tm, d), lambda i: (i, 0))
        ext_spec = pl.BlockSpec((tm, dc), lambda i: (i, 0))
    else:
        mod_spec = pl.BlockSpec((None, 1, d), lambda i: (i // tiles_per_seq, 0, 0))
        ext_spec = pl.BlockSpec((1, dc), lambda i: (0, 0))
    const = lambda shape: pl.BlockSpec(shape, lambda i: tuple(0 for _ in shape))
    row = lambda w: pl.BlockSpec((tm, w), lambda i: (i, 0))
    out_widths = [dc, hq, hk, hk, hi_w, LANES, d, d]
    return pl.pallas_call(
        functools.partial(_inproj_body, tm, seq_rows, dc),
        grid=(nt,),
        in_specs=[row(d), mod_spec, mod_spec, const((1, d)), const(w_pad.shape),
                  pl.BlockSpec((tm, LANES), lambda i: (i % pos_tiles, 0)),
                  pl.BlockSpec((tm, LANES), lambda i: (i % pos_tiles, 0)),
                  const((1, hq)), const((1, hq)), const((hq, hq)), const(convw.shape), const(wco.shape),
                  ext_spec, ext_spec],
        out_specs=[row(w) for w in out_widths] + [pl.BlockSpec((hk, tm), lambda i: (0, i)),
                                                  pl.BlockSpec((IDX_HEADS, tm), lambda i: (0, i))],
        out_shape=[jax.ShapeDtypeStruct((rows, w), F32) for w in out_widths]
                  + [jax.ShapeDtypeStruct((hk, rows), F32), jax.ShapeDtypeStruct((IDX_HEADS, rows), F32)],
        scratch_shapes=[pltpu.VMEM((SUBLANES, dc), F32), pltpu.VMEM((tm + SUBLANES, dc), F32)],
        compiler_params=_cparams(("arbitrary",)),
    )(x2, shift, scale, n1w, w_pad, cos, sin, qnw, knw, gsum, convw, wco, ext1, ext2)


def _sortable_key(score):
    bits = lax.bitcast_convert_type(score, I32)
    key = jnp.where(bits < 0, bits ^ jnp.int32(0x7FFFFFFF), bits)
    return jnp.where(score == 0.0, jnp.int32(0), key)


def _kth_largest_key(count_ge, shape, k):
    ans = jnp.where(count_ge(jnp.zeros(shape, I32)) >= k, jnp.int32(0), jnp.int32(INT_MIN))

    def step(it, ans):
        cand = ans + jnp.left_shift(jnp.int32(1), jnp.int32(30) - it)
        return jnp.where(count_ge(cand) >= k, cand, ans)

    return lax.fori_loop(0, 31, step, ans)


def _tie_cut(count_eq_below, shape, need, nbits):
    def step(it, c):
        cand = c + jnp.left_shift(jnp.int32(1), jnp.int32(nbits - 1) - it)
        return jnp.where(count_eq_below(cand) < need, cand, c)

    return lax.fori_loop(0, nbits, step, jnp.zeros(shape, I32))


def _col_reduce(x, op, rows=64):
    n = x.shape[0]
    if n > rows and n % rows == 0:
        x = op(x.reshape(n // rows, rows, x.shape[1]), axis=0)
    return op(x, axis=0, keepdims=True)


def _pattn_class(tq, klen, topk, j, q_ref, qi_ref, wit_ref, k_ref, vt_ref, kiw_ref, o_ref, key_ref, cut_ref):
    qi = qi_ref[...]
    wit = wit_ref[...]
    ki = kiw_ref[0:klen, 0:IDX_DIM].astype(BF16)
    score = jnp.zeros((klen, tq), F32)
    hpm = 4
    for h0 in range(0, IDX_HEADS, hpm):
        qs = jnp.concatenate([qi[:, h * IDX_DIM:(h + 1) * IDX_DIM] for h in range(h0, h0 + hpm)], axis=0)
        s = _dot_nt(ki, qs.astype(BF16))
        for g in range(hpm):
            score = score + wit[h0 + g:h0 + g + 1, :] * jnp.maximum(s[:, g * tq:(g + 1) * tq], 0.0)
    kpos = lax.broadcasted_iota(I32, (klen, 1), 0)
    qpos = j * tq + lax.broadcasted_iota(I32, (1, tq), 1)
    causal = kpos <= qpos
    score = jnp.where(causal, score, NEG_INF)
    key_ref[0:klen, :] = _sortable_key(score)

    def count_ge(c):
        return _col_reduce((key_ref[0:klen, :] >= c).astype(F32), jnp.sum)

    thr = _kth_largest_key(count_ge, (1, tq), topk)
    key = key_ref[0:klen, :]
    gt = key > thr
    eq = key == thr
    need = topk - _col_reduce(gt.astype(F32), jnp.sum)
    n_eq = _col_reduce(eq.astype(F32), jnp.sum)
    cut_ref[...] = jnp.full(cut_ref.shape, klen, I32)

    @pl.when(jnp.max(n_eq - need) > 0)
    def _():
        def count_eq_below(c):
            hit = (key_ref[0:klen, :] == thr) & (kpos < c)
            return _col_reduce(hit.astype(F32), jnp.sum)

        cut = _tie_cut(count_eq_below, (1, tq), need, int(klen - 1).bit_length())
        cut_ref[...] = jnp.broadcast_to(cut, cut_ref.shape)

    cut = cut_ref[0:1, :]
    bias = jnp.where((gt | (eq & (kpos <= cut))) & causal, 0.0, NEG_INF)

    q = q_ref[...] * (HEAD_DIM ** -0.5)
    k = k_ref[0:klen, :].astype(BF16)
    group = N_HEADS // N_KV_HEADS
    outs = []
    for h in range(N_HEADS):
        n = h // group
        if h % group == 0:
            qs = jnp.concatenate([q[:, g * HEAD_DIM:(g + 1) * HEAD_DIM] for g in range(h, h + group)], axis=0)
            logits_n = _dot_nt(k[:, n * HEAD_DIM:(n + 1) * HEAD_DIM], qs.astype(BF16))
        logits = logits_n[:, (h % group) * tq:(h % group + 1) * tq] + bias
        m = _col_reduce(logits, jnp.max)
        p = jnp.exp(logits - m)
        l = _col_reduce(p, jnp.sum)
        vt = vt_ref[n * HEAD_DIM:(n + 1) * HEAD_DIM, 0:klen].astype(BF16)
        outs.append(_dot(vt, p.astype(BF16)) / l)
    o_ref[...] = jnp.concatenate(outs, axis=0).T


def _pattn_body(tq, t, topk, n_cls, q_ref, qi_ref, wit_ref, k_ref, vt_ref, kiw_ref, o_ref, key_ref, cut_ref):
    j = pl.program_id(1)
    blocks_per_cls = (t // tq) // n_cls
    for c in range(n_cls):
        @pl.when((j >= c * blocks_per_cls) & (j < (c + 1) * blocks_per_cls))
        def _(c=c):
            _pattn_class(tq, (c + 1) * blocks_per_cls * tq, topk, j, q_ref, qi_ref, wit_ref, k_ref, vt_ref, kiw_ref,
                         o_ref, key_ref, cut_ref)


def _pattn(q, qi, wit, k, vt, kiw, *, bsz, t):
    hq = q.shape[1]
    tq = Q_BLOCK
    nqb = t // tq
    n_cls = 4 if nqb % 4 == 0 else 1
    topk = min(TOPK_MAX, t // 4)
    qspec = lambda w: pl.BlockSpec((tq, w), lambda bi, j: (bi * nqb + j, 0))
    kspec = lambda w: pl.BlockSpec((t, w), lambda bi, j: (bi, 0))
    return pl.pallas_call(
        functools.partial(_pattn_body, tq, t, topk, n_cls),
        grid=(bsz, nqb),
        in_specs=[qspec(hq), qspec(qi.shape[1]),
                  pl.BlockSpec((wit.shape[0], tq), lambda bi, j: (0, bi * nqb + j)),
                  kspec(k.shape[1]),
                  pl.BlockSpec((vt.shape[0], t), lambda bi, j: (0, bi)),
                  kspec(kiw.shape[1])],
        out_specs=qspec(hq),
        out_shape=jax.ShapeDtypeStruct((bsz * t, hq), F32),
        scratch_shapes=[pltpu.VMEM((t, tq), I32), pltpu.VMEM((SUBLANES, tq), I32)],
        compiler_params=_cparams(("arbitrary", "arbitrary")),
    )(q, qi, wit, k, vt, kiw)


def _sattn_body(n_pages, g, ds, topk, pt_ref, qi_ref, w_ref, q_ref, kin_ref, kn_ref, vn_ref, *rest):
    kic, kc, vc = rest[0:g], rest[g:2 * g], rest[2 * g:3 * g]
    o_ref, score_ref, mask_ref, m_ref, l_ref, acc_ref, kicat_ref, kcat_ref, vcat_ref = rest[3 * g:]
    ph = pl.program_id(1)
    p = pl.program_id(2)
    n_steps = n_pages // g
    group = N_HEADS // N_KV_HEADS
    sub = lax.broadcasted_iota(I32, (SUBLANES, LANES), 0)
    lane = lax.broadcasted_iota(I32, (SUBLANES, LANES), 1)
    new_valid = (lane <= sub) & (lane < ds)

    def scores(ki_bf, npg):
        s = _dot_nt(qi_ref[...].astype(BF16), ki_bf)
        w = w_ref[...]
        w = jnp.concatenate([w] * npg, axis=1) if npg > 1 else w
        acc = jnp.zeros((SUBLANES, npg * LANES), F32)
        for h in range(IDX_HEADS):
            r = slice(h * SUBLANES, (h + 1) * SUBLANES)
            acc = acc + w[r, :] * jnp.maximum(s[r, :], 0.0)
        return acc

    @pl.when(ph == 0)
    def _():
        for gi in range(g):
            kicat_ref[gi * LANES:(gi + 1) * LANES, :] = kic[gi][...].astype(BF16)
        sc = scores(kicat_ref[...], g)
        for gi in range(g):
            score_ref[p * g + gi] = sc[:, gi * LANES:(gi + 1) * LANES]

        @pl.when(p == 0)
        def _():
            score_ref[n_pages] = jnp.where(new_valid, scores(kin_ref[...].astype(BF16), 1), NEG_INF)

    def attend(mask8, kb, vb):
        maskg = jnp.concatenate([mask8] * group, axis=0) > 0.5
        for n in range(N_KV_HEADS):
            qn = (q_ref[n] * (HEAD_DIM ** -0.5)).astype(BF16)
            logits = _dot_nt(qn, kb[:, n * HEAD_DIM:(n + 1) * HEAD_DIM])
            logits = jnp.where(maskg, logits, NEG_INF)
            m_old = m_ref[n][:, 0:1]
            m_new = jnp.maximum(m_old, jnp.max(logits, axis=1, keepdims=True))
            pr = jnp.where(maskg, jnp.exp(logits - m_new), 0.0)
            alpha = jnp.exp(m_old - m_new)
            l_new = alpha * l_ref[n][:, 0:1] + jnp.sum(pr, axis=1, keepdims=True)
            acc_ref[n] = alpha * acc_ref[n] + _dot(pr.astype(BF16), vb[:, n * HEAD_DIM:(n + 1) * HEAD_DIM])
            m_ref[n] = jnp.broadcast_to(m_new, m_ref.shape[1:])
            l_ref[n] = jnp.broadcast_to(l_new, l_ref.shape[1:])

    @pl.when(ph == 1)
    def _():
        @pl.when(p == 0)
        def _():
            key = _sortable_key(score_ref[...])
            slab = lax.broadcasted_iota(I32, key.shape, 0)
            idx = slab * LANES + lax.broadcasted_iota(I32, key.shape, 2)

            def count(hit):
                x = hit.astype(F32)
                parts = [jnp.sum(x[s:s + 16], axis=0) for s in range(0, x.shape[0], 16)]
                c = functools.reduce(lambda u, v: u + v, parts)
                return jnp.sum(c, axis=1, keepdims=True)[None]

            thr = _kth_largest_key(lambda c: count(key >= c), (1, SUBLANES, 1), topk)
            gt = key > thr
            eq = key == thr
            need = topk - count(gt)
            nbits = int((n_pages + 1) * LANES - 1).bit_length()
            cut = _tie_cut(lambda c: count(eq & (idx < c)), (1, SUBLANES, 1), need, nbits)
            sel = gt | (eq & (idx <= cut))
            valid = (slab < n_pages) | new_valid[None]
            mask_ref[...] = (sel & valid).astype(F32)
            m_ref[...] = jnp.full(m_ref.shape, NEG_INF, F32)
            l_ref[...] = jnp.zeros(l_ref.shape, F32)
            acc_ref[...] = jnp.zeros(acc_ref.shape, F32)
            attend(mask_ref[n_pages], kn_ref[...].astype(BF16), vn_ref[...].astype(BF16))

        for gi in range(g):
            kcat_ref[gi * LANES:(gi + 1) * LANES, :] = kc[gi][...].astype(BF16)
            vcat_ref[gi * LANES:(gi + 1) * LANES, :] = vc[gi][...].astype(BF16)
        mask = jnp.concatenate([mask_ref[p * g + gi] for gi in range(g)], axis=1)
        attend(mask, kcat_ref[...], vcat_ref[...])

        @pl.when(p == n_steps - 1)
        def _():
            o_ref[...] = acc_ref[...] / l_ref[...][:, :, 0:HEAD_DIM]


def _sattn(page_table, qi_arr, w_arr, q_arr, kin, kn, vn, cache_kidx, cache_k, cache_v, *, ds):
    db, n_pages = page_table.shape
    page = cache_k.shape[1]
    assert page == LANES
    g = 16 if n_pages % 16 == 0 else 1
    n_steps = n_pages // g
    topk = min(TOPK_MAX, (n_pages * page + ds) // 4)
    group = N_HEADS // N_KV_HEADS
    hk = N_KV_HEADS * HEAD_DIM
    rows_i = IDX_HEADS * SUBLANES
    rows_q = group * SUBLANES
    pt = page_table.reshape(-1)
    per_b = lambda *shape: pl.BlockSpec((None,) + shape, lambda b, ph, p, pt: (b,) + tuple(0 for _ in shape))

    def kic_map(gi, b, ph, p, pt):
        return (pt[b * n_pages + (p * (1 - ph) + (n_steps - 1) * ph) * g + gi], 0, 0)

    def kvc_map(gi, b, ph, p, pt):
        return (pt[b * n_pages + p * ph * g + gi], 0, 0)

    kic_specs = [pl.BlockSpec((None, page, IDX_DIM), functools.partial(kic_map, gi)) for gi in range(g)]
    kvc_specs = [pl.BlockSpec((None, page, hk), functools.partial(kvc_map, gi)) for gi in range(g)]
    grid_spec = pltpu.PrefetchScalarGridSpec(
        num_scalar_prefetch=1,
        grid=(db, 2, n_steps),
        in_specs=[per_b(rows_i, IDX_DIM), per_b(rows_i, LANES), per_b(N_KV_HEADS, rows_q, HEAD_DIM),
                  per_b(page, IDX_DIM), per_b(page, hk), per_b(page, hk)] + kic_specs + kvc_specs + kvc_specs,
        out_specs=per_b(N_KV_HEADS, rows_q, HEAD_DIM),
        scratch_shapes=[pltpu.VMEM((n_pages + 1, SUBLANES, LANES), F32),
                        pltpu.VMEM((n_pages + 1, SUBLANES, LANES), F32),
                        pltpu.VMEM((N_KV_HEADS, rows_q, LANES), F32),
                        pltpu.VMEM((N_KV_HEADS, rows_q, LANES), F32),
                        pltpu.VMEM((N_KV_HEADS, rows_q, HEAD_DIM), F32),
                        pltpu.VMEM((g * page, IDX_DIM), BF16),
                        pltpu.VMEM((g * page, hk), BF16),
                        pltpu.VMEM((g * page, hk), BF16)],
    )
    return pl.pallas_call(
        functools.partial(_sattn_body, n_pages, g, ds, topk),
        grid_spec=grid_spec,
        out_shape=jax.ShapeDtypeStruct((db, N_KV_HEADS, rows_q, HEAD_DIM), F32),
        compiler_params=_cparams(("arbitrary", "arbitrary", "arbitrary")),
    )(pt, qi_arr, w_arr, q_arr, kin, kn, vn, *([cache_kidx] * g), *([cache_k] * g), *([cache_v] * g))


def _finish_body(x_ref, yag_ref, sgb_ref, attn_ref, gate_ref, shift_ref, scale_ref, n2w_ref, wao_ref, wout_ref, wq_ref,
                 x1_ref, h2_ref, pq_ref):
    yb = _dot(attn_ref[...].astype(BF16), wao_ref[...])
    mix = yag_ref[...] + sgb_ref[...] * yb
    x1 = x_ref[...] + gate_ref[...] * _dot(mix.astype(BF16), wout_ref[...])
    x1_ref[...] = x1
    ms = jnp.mean(x1 * x1, axis=-1, keepdims=True)
    h2 = x1 * lax.rsqrt(ms + NORM_EPS) * n2w_ref[...]
    h2 = (h2 * (1.0 + scale_ref[...]) + shift_ref[...]).astype(BF16)
    h2_ref[...] = h2
    pq_ref[...] = _dot(h2, wq_ref[...]).astype(BF16)


def _finish(x2, yag, sgb, attn, gate, shift, scale, n2w, wao, wout, wq, *, tm, seq_rows, per_row_mod):
    rows, d = x2.shape
    nt = rows // tm
    tiles_per_seq = max(seq_rows // tm, 1)
    if per_row_mod:
        mod_spec = pl.BlockSpec((tm, d), lambda i: (i, 0))
    else:
        mod_spec = pl.BlockSpec((None, 1, d), lambda i: (i // tiles_per_seq, 0, 0))
    const = lambda shape: pl.BlockSpec(shape, lambda i: tuple(0 for _ in shape))
    row = lambda w: pl.BlockSpec((tm, w), lambda i: (i, 0))
    nq = wq.shape[1]
    return pl.pallas_call(
        _finish_body,
        grid=(nt,),
        in_specs=[row(d), row(d), row(d), row(attn.shape[1]), mod_spec, mod_spec, mod_spec, const((1, d)),
                  const(wao.shape), const(wout.shape), const(wq.shape)],
        out_specs=[row(d), row(d), row(nq)],
        out_shape=[jax.ShapeDtypeStruct((rows, d), F32), jax.ShapeDtypeStruct((rows, d), BF16),
                   jax.ShapeDtypeStruct((rows, nq), BF16)],
        compiler_params=_cparams(("arbitrary",)),
    )(x2, yag, sgb, attn, gate, shift, scale, n2w, wao, wout, wq)


def _top_sorted(x, n):
    vals = []
    for _ in range(n):
        m = jnp.max(x, axis=0, keepdims=True)
        vals.append(m)
        x = jnp.where(x >= m, -jnp.inf, x)
    return jnp.concatenate(vals, axis=0)


def _gelu_tanh(x):
    c = np.float32(np.sqrt(2.0 / np.pi))
    return x * (0.5 * (1.0 + jnp.tanh(c * (x + 0.044715 * (x * x * x)))))


def _peer_select(lg, pq_ref, sk_ref, ct_ref, ea_ref, eb_ref):
    half = PEER_QDIM // 2
    rows = pl.ds(pl.multiple_of(lg * LANES, LANES), LANES)
    for h in range(PEER_HEADS):
        sa = _dot_nt(sk_ref[2 * h], pq_ref[rows, (2 * h) * half:(2 * h + 1) * half])
        sb = _dot_nt(sk_ref[2 * h + 1], pq_ref[rows, (2 * h + 1) * half:(2 * h + 2) * half])
        ta = _top_sorted(sa, PEER_TOPK + 1)
        tb = _top_sorted(sb, PEER_TOPK + 1)
        tail = jnp.concatenate([ta[0:1] + tb[16:17], ta[16:17] + tb[0:1],
                                jnp.full((SUBLANES - 2, LANES), -jnp.inf, F32)], axis=0)
        cand = jnp.concatenate(
            [ta[0:1] + tb[0:16]] + [ta[r:r + 1] + tb[0:8] for r in range(1, 8)] + [ta[8:16] + tb[0:1], tail], axis=0)
        top = _top_sorted(cand, PEER_TOPK + 1)
        thr = 0.5 * (top[PEER_TOPK - 1:PEER_TOPK] + top[PEER_TOPK:PEER_TOPK + 1])
        z = jnp.sum(jnp.exp(top[0:PEER_TOPK] - top[0:1]), axis=0, keepdims=True)
        ct_ref[lg, h] = jnp.exp((thr - tb[0:1]) - sa) / z
        ea_ref[lg, h] = jnp.exp(sa - ta[0:1])
        eb_ref[lg, h] = (jnp.exp(sb - tb[0:1]) / z).reshape(eb_ref.shape[2:])


def _peer_body(ce, tm, nc, h2_ref, pq_ref, x1_ref, gate_ref, sk_ref, u_ref, vt_ref, out_ref,
               ct_ref, ea_ref, eb_ref, yt_ref, p_ref):
    c = pl.program_id(1)
    kk = PEER_KEYS
    n_lg = tm // LANES
    n_sub = ce // kk
    pair = 2 if n_sub % 2 == 0 else 1
    jrows = kk // 2
    n_pairs = n_sub // pair
    prow = pair * kk

    @pl.when(c == 0)
    def _():
        yt_ref[...] = jnp.zeros_like(yt_ref)

        def body(lg, carry):
            _peer_select(lg, pq_ref, sk_ref, ct_ref, ea_ref, eb_ref)
            return carry

        lax.fori_loop(0, n_lg, body, 0)

    h2 = h2_ref[...]

    def expert_act(ip):
        return _dot_nt(u_ref[ip * prow:(ip + 1) * prow, :], h2)

    def accumulate(ip):
        yt_ref[...] += _dot(vt_ref[:, ip * prow:(ip + 1) * prow], p_ref[ip * prow:(ip + 1) * prow, :])

    a_next = expert_act(0)
    for ip in range(n_pairs):
        subs = [pair * ip + t for t in range(pair)]
        a_cur = a_next
        a = [a_cur[t * kk:(t + 1) * kk, :] for t in range(pair)]
        if ip + 1 < n_pairs:
            a_next = expert_act(ip + 1)
        if ip > 0:
            accumulate(ip - 1)
        for lg in range(n_lg):
            cols = slice(lg * LANES, (lg + 1) * LANES)
            for jh in range(2):
                rows = slice(jh * jrows, (jh + 1) * jrows)
                vrows = slice(jh * jrows // SUBLANES, (jh + 1) * jrows // SUBLANES)
                w = [None] * pair
                for h in range(PEER_HEADS):
                    ebh = eb_ref[lg, h, vrows]
                    for t, s in enumerate(subs):
                        i = c * n_sub + s
                        ctv = jnp.broadcast_to(ct_ref[lg, h, pl.ds(i, 1), :], (SUBLANES, LANES))[None]
                        eav = jnp.broadcast_to(ea_ref[lg, h, pl.ds(i, 1), :], (SUBLANES, LANES))[None]
                        term = jnp.where(ebh >= ctv, ebh, 0.0) * eav
                        w[t] = term if w[t] is None else w[t] + term
                for t, s in enumerate(subs):
                    prod = w[t].reshape(jrows, LANES) * _gelu_tanh(a[t][rows, cols])
                    p_ref[s * kk + jh * jrows:s * kk + (jh + 1) * jrows, cols] = prod.astype(BF16)
    accumulate(n_pairs - 1)

    @pl.when(c == nc - 1)
    def _():
        out_ref[...] = x1_ref[...] + gate_ref[...] * yt_ref[...].T


def _peer(h2, pq, x1, gate, sk, u_bf, vt_bf, *, tm, ce, seq_rows, per_row_mod):
    rows, d = x1.shape
    ne = u_bf.shape[0]
    nt = rows // tm
    nc = ne // ce
    assert tm % LANES == 0 and rows % tm == 0 and ne % ce == 0
    tiles_per_seq = max(seq_rows // tm, 1)
    if per_row_mod:
        mod_spec = pl.BlockSpec((tm, d), lambda i, c: (i, 0))
    else:
        mod_spec = pl.BlockSpec((None, 1, d), lambda i, c: (i // tiles_per_seq, 0, 0))
    row = lambda w: pl.BlockSpec((tm, w), lambda i, c: (i, 0))
    sel_shape = (tm // LANES, PEER_HEADS, PEER_KEYS, LANES)
    vreg_shape = (tm // LANES, PEER_HEADS, PEER_KEYS // SUBLANES, SUBLANES, LANES)
    return pl.pallas_call(
        functools.partial(_peer_body, ce, tm, nc),
        grid=(nt, nc),
        in_specs=[row(d), row(pq.shape[1]), row(d), mod_spec,
                  pl.BlockSpec(sk.shape, lambda i, c: (0, 0, 0)),
                  pl.BlockSpec((ce, d), lambda i, c: (c, 0)),
                  pl.BlockSpec((d, ce), lambda i, c: (0, c))],
        out_specs=row(d),
        out_shape=jax.ShapeDtypeStruct((rows, d), F32),
        scratch_shapes=[pltpu.VMEM(sel_shape, F32), pltpu.VMEM(sel_shape, F32), pltpu.VMEM(vreg_shape, F32),
                        pltpu.VMEM((d, tm), F32), pltpu.VMEM((ce, tm), BF16)],
        compiler_params=_cparams(("arbitrary", "arbitrary")),
    )(h2, pq, x1, gate, sk, u_bf, vt_bf)


def _rope_tables(pos):
    half = HEAD_DIM // 2
    freqs = ROPE_THETA ** (-jnp.arange(half, dtype=F32) / half)
    ang = pos.astype(F32)[:, None] * freqs[None, :]
    cos, sin = jnp.cos(ang), jnp.sin(ang)
    cos2 = jnp.concatenate([cos, cos, cos, cos], axis=1)
    sin2 = jnp.concatenate([-sin, sin, -sin, sin], axis=1)
    return cos2, sin2


def _pick_tile(rows, pref):
    t = min(pref, rows)
    while rows % t:
        t //= 2
    return t


def kernel(x_prompt, x_sample, cache_k, cache_v, cache_kidx, state_conv, page_table, c_prompt, c_sample, w_ada, b_ada,
           norm1_w, w_in, conv_w, q_norm_w, k_norm_w, w_conv_out, w_attn_out, w_out, norm2_w, peer_w_q, peer_sub_keys,
           peer_u, peer_v):
    bsz, seq, d = x_prompt.shape
    db, ds, _ = x_sample.shape
    depth = w_ada.shape[0]
    dc = conv_w.shape[2]
    hq = N_HEADS * HEAD_DIM
    hk = N_KV_HEADS * HEAD_DIM
    hi_w = IDX_HEADS * IDX_DIM
    n_pages = page_table.shape[1]
    page = cache_k.shape[2]
    past = n_pages * page
    group = N_HEADS // N_KV_HEADS

    pos_p = jnp.arange(seq, dtype=I32)
    pos_s = past + jnp.arange(ds, dtype=I32)
    cos_p, sin_p = _rope_tables(pos_p)
    cos_s, sin_s = _rope_tables(jnp.tile(pos_s, db))
    gsum = jnp.kron(jnp.eye(N_HEADS, dtype=F32), jnp.ones((HEAD_DIM, HEAD_DIM), F32)).astype(BF16)

    xp = x_prompt.reshape(bsz * seq, d)
    xs = x_sample.reshape(db * ds, d)
    outs = {n: [] for n in ("kp", "vp", "kip", "cp", "ks", "vs", "kis", "cs")}

    for layer in range(depth):
        mod = _adaln(jnp.concatenate([c_prompt, c_sample], axis=0), w_ada[layer], b_ada[layer])
        mod_p = [m.reshape(bsz, 1, d) for m in jnp.split(mod[:bsz], N_MOD, axis=-1)]
        mod_s = [jnp.repeat(m, ds, axis=0) for m in jnp.split(mod[bsz:], N_MOD, axis=-1)]

        w = w_in[layer]
        o_ki = 3 * dc + hq + 2 * hk + hi_w
        o_g = o_ki + IDX_DIM + IDX_HEADS
        slab = jnp.concatenate([w[:, o_ki:o_g], jnp.zeros((d, LANES - IDX_DIM - IDX_HEADS), F32)], axis=1)
        w_pad = jnp.concatenate([w[:, :o_ki], slab, w[:, o_g:]], axis=1).astype(BF16)
        n1w = norm1_w[layer].reshape(1, d)
        qnw = jnp.tile(q_norm_w[layer], N_HEADS).reshape(1, hq)
        knw = jnp.tile(k_norm_w[layer], N_HEADS).reshape(1, hq)
        wco = w_conv_out[layer].astype(BF16)
        wao = w_attn_out[layer].astype(BF16)
        wout = w_out[layer].astype(BF16)
        wq = peer_w_q[layer].astype(BF16)
        n2w = norm2_w[layer].reshape(1, d)
        sk = jnp.transpose(peer_sub_keys[layer], (1, 0, 2, 3)).reshape(2 * PEER_HEADS, PEER_KEYS, PEER_QDIM // 2)
        sk = sk.astype(BF16)
        u_bf = peer_u[layer].astype(BF16)
        vt_bf = peer_v[layer].astype(BF16).T
        ce = min(1024, u_bf.shape[0])

        tm_p = _pick_tile(seq, 256)
        zero_ext = jnp.zeros((1, dc), F32)
        u, q, k, v, qi, kiw, yag, sgb, vt, wit = _inproj(
            xp, mod_p[0], mod_p[1], n1w, w_pad, cos_p, sin_p, qnw, knw, gsum, conv_w[layer], wco, zero_ext, zero_ext,
            tm=tm_p, seq_rows=seq, per_row_mod=False)
        ki = kiw[:, :IDX_DIM]
        attn = _pattn(q, qi, wit, k, vt, kiw, bsz=bsz, t=seq)
        tm_f = _pick_tile(seq, 512)
        x1, h2, pq = _finish(xp, yag, sgb, attn, mod_p[2], mod_p[3], mod_p[4], n2w, wao, wout,
                             wq, tm=tm_f, seq_rows=seq, per_row_mod=False)
        xp = _peer(h2, pq, x1, mod_p[5], sk, u_bf, vt_bf, tm=tm_f, ce=ce, seq_rows=seq, per_row_mod=False)
        outs["kp"].append(k.reshape(bsz, seq, N_KV_HEADS, HEAD_DIM))
        outs["vp"].append(v.reshape(bsz, seq, N_KV_HEADS, HEAD_DIM))
        outs["kip"].append(ki.reshape(bsz, seq, IDX_DIM))
        outs["cp"].append(u.reshape(bsz, seq, dc)[:, seq - 2:, :])

        rows_s = db * ds
        st = state_conv[layer]
        posr = jnp.tile(jnp.arange(ds), db)
        ext1 = jnp.repeat(st[:, 1, :], ds, axis=0)
        ext2 = jnp.where((posr == 0)[:, None], jnp.repeat(st[:, 0, :], ds, axis=0), ext1)
        u, q, k, v, qi, kiw, yag, sgb, _, _ = _inproj(
            xs, mod_s[0], mod_s[1], n1w, w_pad, cos_s, sin_s, qnw, knw, gsum, conv_w[layer], wco, ext1, ext2,
            tm=rows_s, seq_rows=ds, per_row_mod=True)
        ki = kiw[:, :IDX_DIM]
        wi = kiw[:, IDX_DIM:IDX_DIM + IDX_HEADS]
        pad_q = lambda a: jnp.pad(a, ((0, 0), (0, 0), (0, SUBLANES - ds), (0, 0)))
        qi_arr = pad_q(qi.reshape(db, ds, IDX_HEADS, IDX_DIM).transpose(0, 2, 1, 3)).reshape(db, IDX_HEADS * SUBLANES, IDX_DIM)
        w_arr = pad_q(wi.reshape(db, ds, IDX_HEADS, 1).transpose(0, 2, 1, 3)).reshape(db, IDX_HEADS * SUBLANES, 1)
        w_arr = jnp.broadcast_to(w_arr, (db, IDX_HEADS * SUBLANES, LANES))
        q_arr = pad_q(q.reshape(db, ds, N_HEADS, HEAD_DIM).transpose(0, 2, 1, 3)).reshape(db, N_KV_HEADS, group * SUBLANES, HEAD_DIM)
        pad_page = lambda a: jnp.pad(a.reshape(db, ds, -1), ((0, 0), (0, page - ds), (0, 0)))
        o = _sattn(page_table, qi_arr, w_arr, q_arr, pad_page(ki), pad_page(k), pad_page(v),
                   cache_kidx[layer], cache_k[layer].reshape(-1, page, hk), cache_v[layer].reshape(-1, page, hk), ds=ds)
        attn = o.reshape(db, N_HEADS, SUBLANES, HEAD_DIM)[:, :, :ds, :].transpose(0, 2, 1, 3).reshape(rows_s, hq)
        x1, h2, pq = _finish(xs, yag, sgb, attn, mod_s[2], mod_s[3], mod_s[4], n2w, wao, wout, wq,
                             tm=rows_s, seq_rows=ds, per_row_mod=True)
        xs = _peer(h2, pq, x1, mod_s[5], sk, u_bf, vt_bf, tm=rows_s, ce=ce, seq_rows=ds, per_row_mod=True)
        outs["ks"].append(k.reshape(db, ds, N_KV_HEADS, HEAD_DIM))
        outs["vs"].append(v.reshape(db, ds, N_KV_HEADS, HEAD_DIM))
        outs["kis"].append(ki.reshape(db, ds, IDX_DIM))
        full = jnp.concatenate([st, u.reshape(db, ds, dc)], axis=1)
        outs["cs"].append(full[:, full.shape[1] - 2:, :])

    stack = lambda n: jnp.stack(outs[n])
    return (xp.reshape(bsz, seq, d), xs.reshape(db, ds, d), stack("kp"), stack("vp"), stack("kip"), stack("cp"),
            stack("ks"), stack("vs"), stack("kis"), stack("cs"))
```

```python
import functools

import numpy as np
import jax
import jax.numpy as jnp
from jax import lax
from jax.experimental import pallas as pl
from jax.experimental.pallas import tpu as pltpu

F32 = jnp.float32
BF16 = jnp.bfloat16
I32 = jnp.int32

N_HEADS = 8
N_KV_HEADS = 2
HEAD_DIM = 64
IDX_HEADS = 8
IDX_DIM = 64
IDX_SCALE = (IDX_DIM ** -0.5) * (IDX_HEADS ** -0.5)
TOPK_MAX = 256
Q_BLOCK = 128
ROPE_THETA = 10000.0
PEER_HEADS = 8
PEER_KEYS = 128
PEER_QDIM = 256
PEER_TOPK = 16
N_MOD = 6
NORM_EPS = 1e-6
NEG_INF = -1e30
INT_MIN = -2 ** 31

LANES = 128
SUBLANES = 8
VMEM_LIMIT = 56 * 1024 * 1024


def _cparams(sem):
    return pltpu.CompilerParams(dimension_semantics=sem, vmem_limit_bytes=VMEM_LIMIT)


def _dot(a, b):
    return jnp.dot(a, b, preferred_element_type=F32)


def _dot_nt(a, b):
    return lax.dot_general(a, b, (((1,), (1,)), ((), ())), preferred_element_type=F32)


def _adaln_body(c_ref, w_ref, b_ref, o_ref):
    o_ref[...] = jnp.dot(c_ref[...], w_ref[...], preferred_element_type=F32,
                         precision=lax.Precision.HIGHEST) + b_ref[...]


def _adaln(c, w_ada, b_ada):
    m, d = c.shape
    n = w_ada.shape[1]
    tn = d
    return pl.pallas_call(
        _adaln_body,
        grid=(n // tn,),
        in_specs=[pl.BlockSpec((m, d), lambda j: (0, 0)),
                  pl.BlockSpec((d, tn), lambda j: (0, j)),
                  pl.BlockSpec((1, tn), lambda j: (0, j))],
        out_specs=pl.BlockSpec((m, tn), lambda j: (0, j)),
        out_shape=jax.ShapeDtypeStruct((m, n), F32),
        compiler_params=_cparams(("arbitrary",)),
    )(c, w_ada, b_ada.reshape(1, n))


def _rope_lanes(z, cos, sin_signed):
    w = z.shape[1]
    reps = w // LANES
    cosw = jnp.concatenate([cos] * reps, axis=1) if reps > 1 else cos
    sinw = jnp.concatenate([sin_signed] * reps, axis=1) if reps > 1 else sin_signed
    half = HEAD_DIM // 2
    lane = lax.broadcasted_iota(I32, (1, w), 1)
    first = (lane % HEAD_DIM) < half
    swapped = jnp.where(first, pltpu.roll(z, w - half, 1), pltpu.roll(z, half, 1))
    return z * cosw + swapped * sinw


def _head_rms(z, nw, gsum):
    z2 = z * z
    hi = z2.astype(BF16)
    lo = (z2 - hi.astype(F32)).astype(BF16)
    ssum = _dot(hi, gsum) + _dot(lo, gsum)
    return z * lax.rsqrt(ssum * (1.0 / HEAD_DIM) + NORM_EPS) * nw


def _inproj_body(tm, seq_rows, dc, x_ref, shift_ref, scale_ref, n1w_ref, w_ref, cos_ref, sin_ref, qnw_ref, knw_ref,
                 gsum_ref, convw_ref, wco_ref, ext1_ref, ext2_ref,
                 u_ref, q_ref, k_ref, v_ref, qi_ref, kiw_ref, yag_ref, sgb_ref, vt_ref, wit_ref, carry_ref, buf_ref):
    i = pl.program_id(0)
    hq = N_HEADS * HEAD_DIM
    hk = N_KV_HEADS * HEAD_DIM
    hi_w = IDX_HEADS * IDX_DIM
    d = x_ref.shape[1]
    offs = np.cumsum([0, dc, dc, dc, hq, hk, hk, hi_w, LANES, d, d])

    def proj(s):
        return _dot(hb, w_ref[:, int(offs[s]):int(offs[s + 1])])

    x = x_ref[...]
    ms = jnp.mean(x * x, axis=-1, keepdims=True)
    h = x * lax.rsqrt(ms + NORM_EPS) * n1w_ref[...]
    h = h * (1.0 + scale_ref[...]) + shift_ref[...]
    hb = h.astype(BF16)

    u = proj(2) * proj(0)
    u_ref[...] = u

    @pl.when(i == 0)
    def _():
        carry_ref[...] = jnp.zeros_like(carry_ref)

    buf_ref[0:SUBLANES, :] = carry_ref[...]
    buf_ref[SUBLANES:SUBLANES + tm, :] = u
    carry_ref[...] = u[tm - SUBLANES:tm, :]
    pos = (i * tm + lax.broadcasted_iota(I32, (tm, 1), 0)) % seq_rows
    u1 = jnp.where(pos >= 1, buf_ref[SUBLANES - 1:SUBLANES - 1 + tm, :], ext1_ref[...])
    u2 = jnp.where(pos >= 2, buf_ref[SUBLANES - 2:SUBLANES - 2 + tm, :], ext2_ref[...])
    yc = convw_ref[2:3, :] * u + convw_ref[0:1, :] * u2 + convw_ref[1:2, :] * u1
    a_pre = proj(1) * yc
    ya = _dot(a_pre.astype(BF16), wco_ref[...])
    yag_ref[...] = jax.nn.sigmoid(proj(8)) * ya
    sgb_ref[...] = jax.nn.sigmoid(proj(9))

    cos = cos_ref[...]
    sin = sin_ref[...]
    gsum = gsum_ref[...]
    q_ref[...] = _rope_lanes(_head_rms(proj(3), qnw_ref[...], gsum), cos, sin)
    k_ref[...] = _rope_lanes(_head_rms(proj(4), knw_ref[:, 0:hk], gsum[0:hk, 0:hk]), cos, sin)
    v = proj(5)
    v_ref[...] = v
    vt_ref[...] = v.T
    qi_ref[...] = _rope_lanes(proj(6), cos, sin)
    slab = proj(7)
    lane = lax.broadcasted_iota(I32, (1, LANES), 1)
    kiw = jnp.where(lane < IDX_DIM, _rope_lanes(slab, cos, sin), slab * IDX_SCALE)
    kiw_ref[...] = kiw
    wit_ref[...] = kiw.T[IDX_DIM:IDX_DIM + IDX_HEADS, :]


def _inproj(x2, shift, scale, n1w, w_pad, cos, sin, qnw, knw, gsum, convw, wco, ext1, ext2, *, tm, seq_rows,
            per_row_mod):
    rows, d = x2.shape
    dc = convw.shape[1]
    hq = N_HEADS * HEAD_DIM
    hk = N_KV_HEADS * HEAD_DIM
    hi_w = IDX_HEADS * IDX_DIM
    nt = rows // tm
    pos_tiles = cos.shape[0] // tm
    tiles_per_seq = max(seq_rows // tm, 1)
    if per_row_mod:
        mod_spec = pl.BlockSpec((tm, d), lambda i: (i, 0))
        ext_spec = pl.BlockSpec((tm, dc), lambda i: (i, 0))
    else:
        mod_spec = pl.BlockSpec((None, 1, d), lambda i: (i // tiles_per_seq, 0, 0))
        ext_spec = pl.BlockSpec((1, dc), lambda i: (0, 0))
    const = lambda shape: pl.BlockSpec(shape, lambda i: tuple(0 for _ in shape))
    row = lambda w: pl.BlockSpec((tm, w), lambda i: (i, 0))
    out_widths = [dc, hq, hk, hk, hi_w, LANES, d, d]
    return pl.pallas_call(
        functools.partial(_inproj_body, tm, seq_rows, dc),
        grid=(nt,),
        in_specs=[row(d), mod_spec, mod_spec, const((1, d)), const(w_pad.shape),
                  pl.BlockSpec((tm, LANES), lambda i: (i % pos_tiles, 0)),
                  pl.BlockSpec((tm, LANES), lambda i: (i % pos_tiles, 0)),
                  const((1, hq)), const((1, hq)), const((hq, hq)), const(convw.shape), const(wco.shape),
                  ext_spec, ext_spec],
        out_specs=[row(w) for w in out_widths] + [pl.BlockSpec((hk, tm), lambda i: (0, i)),
                                                  pl.BlockSpec((IDX_HEADS, tm), lambda i: (0, i))],
        out_shape=[jax.ShapeDtypeStruct((rows, w), F32) for w in out_widths]
                  + [jax.ShapeDtypeStruct((hk, rows), F32), jax.ShapeDtypeStruct((IDX_HEADS, rows), F32)],
        scratch_shapes=[pltpu.VMEM((SUBLANES, dc), F32), pltpu.VMEM((tm + SUBLANES, dc), F32)],
        compiler_params=_cparams(("arbitrary",)),
    )(x2, shift, scale, n1w, w_pad, cos, sin, qnw, knw, gsum, convw, wco, ext1, ext2)


def _sortable_key(score):
    bits = lax.bitcast_convert_type(score, I32)
    key = jnp.where(bits < 0, bits ^ jnp.int32(0x7FFFFFFF), bits)
    return jnp.where(score == 0.0, jnp.int32(0), key)


def _kth_largest_key(count_ge, shape, k):
    ans = jnp.where(count_ge(jnp.zeros(shape, I32)) >= k, jnp.int32(0), jnp.int32(INT_MIN))

    def step(it, ans):
        cand = ans + jnp.left_shift(jnp.int32(1), jnp.int32(30) - it)
        return jnp.where(count_ge(cand) >= k, cand, ans)

    return lax.fori_loop(0, 31, step, ans)


def _tie_cut(count_eq_below, shape, need, nbits):
    def step(it, c):
        cand = c + jnp.left_shift(jnp.int32(1), jnp.int32(nbits - 1) - it)
        return jnp.where(count_eq_below(cand) < need, cand, c)

    return lax.fori_loop(0, nbits, step, jnp.zeros(shape, I32))


def _col_reduce(x, op, rows=64):
    n = x.shape[0]
    if n > rows and n % rows == 0:
        x = op(x.reshape(n // rows, rows, x.shape[1]), axis=0)
    return op(x, axis=0, keepdims=True)


def _pattn_class(tq, klen, topk, j, q_ref, qi_ref, wit_ref, k_ref, vt_ref, kiw_ref, o_ref, key_ref, cut_ref):
    qi = qi_ref[...]
    wit = wit_ref[...]
    ki = kiw_ref[0:klen, 0:IDX_DIM].astype(BF16)
    score = jnp.zeros((klen, tq), F32)
    hpm = 4
    for h0 in range(0, IDX_HEADS, hpm):
        qs = jnp.concatenate([qi[:, h * IDX_DIM:(h + 1) * IDX_DIM] for h in range(h0, h0 + hpm)], axis=0)
        s = _dot_nt(ki, qs.astype(BF16))
        for g in range(hpm):
            score = score + wit[h0 + g:h0 + g + 1, :] * jnp.maximum(s[:, g * tq:(g + 1) * tq], 0.0)
    kpos = lax.broadcasted_iota(I32, (klen, 1), 0)
    qpos = j * tq + lax.broadcasted_iota(I32, (1, tq), 1)
    causal = kpos <= qpos
    score = jnp.where(causal, score, NEG_INF)
    key_ref[0:klen, :] = _sortable_key(score)

    def count_ge(c):
        return _col_reduce((key_ref[0:klen, :] >= c).astype(F32), jnp.sum)

    thr = _kth_largest_key(count_ge, (1, tq), topk)
    key = key_ref[0:klen, :]
    gt = key > thr
    eq = key == thr
    need = topk - _col_reduce(gt.astype(F32), jnp.sum)
    n_eq = _col_reduce(eq.astype(F32), jnp.sum)
    cut_ref[...] = jnp.full(cut_ref.shape, klen, I32)

    @pl.when(jnp.max(n_eq - need) > 0)
    def _():
        def count_eq_below(c):
            hit = (key_ref[0:klen, :] == thr) & (kpos < c)
            return _col_reduce(hit.astype(F32), jnp.sum)

        cut = _tie_cut(count_eq_below, (1, tq), need, int(klen - 1).bit_length())
        cut_ref[...] = jnp.broadcast_to(cut, cut_ref.shape)

    cut = cut_ref[0:1, :]
    bias = jnp.where((gt | (eq & (kpos <= cut))) & causal, 0.0, NEG_INF)

    q = q_ref[...] * (HEAD_DIM ** -0.5)
    k = k_ref[0:klen, :].astype(BF16)
    group = N_HEADS // N_KV_HEADS
    outs = []
    for h in range(N_HEADS):
        n = h // group
        if h % group == 0:
            qs = jnp.concatenate([q[:, g * HEAD_DIM:(g + 1) * HEAD_DIM] for g in range(h, h + group)], axis=0)
            logits_n = _dot_nt(k[:, n * HEAD_DIM:(n + 1) * HEAD_DIM], qs.astype(BF16))
        logits = logits_n[:, (h % group) * tq:(h % group + 1) * tq] + bias
        m = _col_reduce(logits, jnp.max)
        p = jnp.exp(logits - m)
        l = _col_reduce(p, jnp.sum)
        vt = vt_ref[n * HEAD_DIM:(n + 1) * HEAD_DIM, 0:klen].astype(BF16)
        outs.append(_dot(vt, p.astype(BF16)) / l)
    o_ref[...] = jnp.concatenate(outs, axis=0).T


def _pattn_body(tq, t, topk, n_cls, q_ref, qi_ref, wit_ref, k_ref, vt_ref, kiw_ref, o_ref, key_ref, cut_ref):
    j = pl.program_id(1)
    blocks_per_cls = (t // tq) // n_cls
    for c in range(n_cls):
        @pl.when((j >= c * blocks_per_cls) & (j < (c + 1) * blocks_per_cls))
        def _(c=c):
            _pattn_class(tq, (c + 1) * blocks_per_cls * tq, topk, j, q_ref, qi_ref, wit_ref, k_ref, vt_ref, kiw_ref,
                         o_ref, key_ref, cut_ref)


def _pattn(q, qi, wit, k, vt, kiw, *, bsz, t):
    hq = q.shape[1]
    tq = Q_BLOCK
    nqb = t // tq
    n_cls = 4 if nqb % 4 == 0 else 1
    topk = min(TOPK_MAX, t // 4)
    qspec = lambda w: pl.BlockSpec((tq, w), lambda bi, j: (bi * nqb + j, 0))
    kspec = lambda w: pl.BlockSpec((t, w), lambda bi, j: (bi, 0))
    return pl.pallas_call(
        functools.partial(_pattn_body, tq, t, topk, n_cls),
        grid=(bsz, nqb),
        in_specs=[qspec(hq), qspec(qi.shape[1]),
                  pl.BlockSpec((wit.shape[0], tq), lambda bi, j: (0, bi * nqb + j)),
                  kspec(k.shape[1]),
                  pl.BlockSpec((vt.shape[0], t), lambda bi, j: (0, bi)),
                  kspec(kiw.shape[1])],
        out_specs=qspec(hq),
        out_shape=jax.ShapeDtypeStruct((bsz * t, hq), F32),
        scratch_shapes=[pltpu.VMEM((t, tq), I32), pltpu.VMEM((SUBLANES, tq), I32)],
        compiler_params=_cparams(("arbitrary", "arbitrary")),
    )(q, qi, wit, k, vt, kiw)


def _sattn_body(n_pages, g, ds, topk, pt_ref, qi_ref, w_ref, q_ref, kin_ref, kn_ref, vn_ref, *rest):
    kic, kc, vc = rest[0:g], rest[g:2 * g], rest[2 * g:3 * g]
    o_ref, score_ref, mask_ref, m_ref, l_ref, acc_ref, kicat_ref, kcat_ref, vcat_ref = rest[3 * g:]
    ph = pl.program_id(1)
    p = pl.program_id(2)
    n_steps = n_pages // g
    group = N_HEADS // N_KV_HEADS
    sub = lax.broadcasted_iota(I32, (SUBLANES, LANES), 0)
    lane = lax.broadcasted_iota(I32, (SUBLANES, LANES), 1)
    new_valid = (lane <= sub) & (lane < ds)

    def scores(ki_bf, npg):
        s = _dot(qi_ref[...].astype(BF16), ki_bf)
        w = w_ref[...]
        w = jnp.concatenate([w] * npg, axis=1) if npg > 1 else w
        acc = jnp.zeros((SUBLANES, npg * LANES), F32)
        for h in range(IDX_HEADS):
            r = slice(h * SUBLANES, (h + 1) * SUBLANES)
            acc = acc + w[r, :] * jnp.maximum(s[r, :], 0.0)
        return acc

    @pl.when(ph == 0)
    def _():
        for gi in range(g):
            kicat_ref[:, gi * LANES:(gi + 1) * LANES] = kic[gi][...].astype(BF16)
        sc = scores(kicat_ref[...], g)
        for gi in range(g):
            score_ref[p * g + gi] = sc[:, gi * LANES:(gi + 1) * LANES]

        @pl.when(p == 0)
        def _():
            score_ref[n_pages] = jnp.where(new_valid, scores(kin_ref[...].astype(BF16), 1), NEG_INF)

    def attend(mask8, kb, vb):
        maskg = jnp.concatenate([mask8] * group, axis=0) > 0.5
        for n in range(N_KV_HEADS):
            qn = (q_ref[n] * (HEAD_DIM ** -0.5)).astype(BF16)
            logits = _dot(qn, kb[n * HEAD_DIM:(n + 1) * HEAD_DIM, :])
            logits = jnp.where(maskg, logits, NEG_INF)
            m_old = m_ref[n][:, 0:1]
            m_new = jnp.maximum(m_old, jnp.max(logits, axis=1, keepdims=True))
            pr = jnp.where(maskg, jnp.exp(logits - m_new), 0.0)
            alpha = jnp.exp(m_old - m_new)
            l_new = alpha * l_ref[n][:, 0:1] + jnp.sum(pr, axis=1, keepdims=True)
            acc_ref[n] = alpha * acc_ref[n] + _dot_nt(pr.astype(BF16), vb[n * HEAD_DIM:(n + 1) * HEAD_DIM, :])
            m_ref[n] = jnp.broadcast_to(m_new, m_ref.shape[1:])
            l_ref[n] = jnp.broadcast_to(l_new, l_ref.shape[1:])

    @pl.when(ph == 1)
    def _():
        @pl.when(p == 0)
        def _():
            key = _sortable_key(score_ref[...])
            slab = lax.broadcasted_iota(I32, key.shape, 0)
            idx = slab * LANES + lax.broadcasted_iota(I32, key.shape, 2)

            def count(hit):
                x = hit.astype(F32)
                parts = [jnp.sum(x[s:s + 16], axis=0) for s in range(0, x.shape[0], 16)]
                c = functools.reduce(lambda u, v: u + v, parts)
                return jnp.sum(c, axis=1, keepdims=True)[None]

            thr = _kth_largest_key(lambda c: count(key >= c), (1, SUBLANES, 1), topk)
            gt = key > thr
            eq = key == thr
            need = topk - count(gt)
            nbits = int((n_pages + 1) * LANES - 1).bit_length()
            cut = _tie_cut(lambda c: count(eq & (idx < c)), (1, SUBLANES, 1), need, nbits)
            sel = gt | (eq & (idx <= cut))
            valid = (slab < n_pages) | new_valid[None]
            mask_ref[...] = (sel & valid).astype(F32)
            m_ref[...] = jnp.full(m_ref.shape, NEG_INF, F32)
            l_ref[...] = jnp.zeros(l_ref.shape, F32)
            acc_ref[...] = jnp.zeros(acc_ref.shape, F32)
            attend(mask_ref[n_pages], kn_ref[...].astype(BF16), vn_ref[...].astype(BF16))

        for gi in range(g):
            kcat_ref[:, gi * LANES:(gi + 1) * LANES] = kc[gi][...].astype(BF16)
            vcat_ref[:, gi * LANES:(gi + 1) * LANES] = vc[gi][...].astype(BF16)
        mask = jnp.concatenate([mask_ref[p * g + gi] for gi in range(g)], axis=1)
        attend(mask, kcat_ref[...], vcat_ref[...])

        @pl.when(p == n_steps - 1)
        def _():
            o_ref[...] = acc_ref[...] / l_ref[...][:, :, 0:HEAD_DIM]


def _sattn(page_table, qi_arr, w_arr, q_arr, kin, kn, vn, cache_kidx, cache_k, cache_v, *, ds):
    db, n_pages = page_table.shape
    page = cache_k.shape[2]
    assert page == LANES
    g = max(c for c in (1, 2, 4, 8, 16, 32, 64) if n_pages % c == 0)
    n_steps = n_pages // g
    topk = min(TOPK_MAX, (n_pages * page + ds) // 4)
    group = N_HEADS // N_KV_HEADS
    hk = N_KV_HEADS * HEAD_DIM
    rows_i = IDX_HEADS * SUBLANES
    rows_q = group * SUBLANES
    pt = page_table.reshape(-1)
    per_b = lambda *shape: pl.BlockSpec((None,) + shape, lambda b, ph, p, pt: (b,) + tuple(0 for _ in shape))

    def kic_map(gi, b, ph, p, pt):
        return (pt[b * n_pages + (p * (1 - ph) + (n_steps - 1) * ph) * g + gi], 0, 0)

    def kvc_map(gi, b, ph, p, pt):
        return (pt[b * n_pages + p * ph * g + gi], 0, 0)

    kic_specs = [pl.BlockSpec((None, IDX_DIM, page), functools.partial(kic_map, gi)) for gi in range(g)]
    kvc_specs = [pl.BlockSpec((None, hk, page), functools.partial(kvc_map, gi)) for gi in range(g)]
    grid_spec = pltpu.PrefetchScalarGridSpec(
        num_scalar_prefetch=1,
        grid=(db, 2, n_steps),
        in_specs=[per_b(rows_i, IDX_DIM), per_b(rows_i, LANES), per_b(N_KV_HEADS, rows_q, HEAD_DIM),
                  per_b(IDX_DIM, page), per_b(hk, page), per_b(hk, page)] + kic_specs + kvc_specs + kvc_specs,
        out_specs=per_b(N_KV_HEADS, rows_q, HEAD_DIM),
        scratch_shapes=[pltpu.VMEM((n_pages + 1, SUBLANES, LANES), F32),
                        pltpu.VMEM((n_pages + 1, SUBLANES, LANES), F32),
                        pltpu.VMEM((N_KV_HEADS, rows_q, LANES), F32),
                        pltpu.VMEM((N_KV_HEADS, rows_q, LANES), F32),
                        pltpu.VMEM((N_KV_HEADS, rows_q, HEAD_DIM), F32),
                        pltpu.VMEM((IDX_DIM, g * page), BF16),
                        pltpu.VMEM((hk, g * page), BF16),
                        pltpu.VMEM((hk, g * page), BF16)],
    )
    return pl.pallas_call(
        functools.partial(_sattn_body, n_pages, g, ds, topk),
        grid_spec=grid_spec,
        out_shape=jax.ShapeDtypeStruct((db, N_KV_HEADS, rows_q, HEAD_DIM), F32),
        compiler_params=_cparams(("arbitrary", "arbitrary", "arbitrary")),
    )(pt, qi_arr, w_arr, q_arr, kin, kn, vn, *([cache_kidx] * g), *([cache_k] * g), *([cache_v] * g))


def _finish_body(x_ref, yag_ref, sgb_ref, attn_ref, gate_ref, shift_ref, scale_ref, n2w_ref, wao_ref, wout_ref, wq_ref,
                 x1_ref, h2_ref, pq_ref):
    yb = _dot(attn_ref[...].astype(BF16), wao_ref[...])
    mix = yag_ref[...] + sgb_ref[...] * yb
    x1 = x_ref[...] + gate_ref[...] * _dot(mix.astype(BF16), wout_ref[...])
    x1_ref[...] = x1
    ms = jnp.mean(x1 * x1, axis=-1, keepdims=True)
    h2 = x1 * lax.rsqrt(ms + NORM_EPS) * n2w_ref[...]
    h2 = (h2 * (1.0 + scale_ref[...]) + shift_ref[...]).astype(BF16)
    h2_ref[...] = h2
    pq_ref[...] = _dot(h2, wq_ref[...]).astype(BF16)


def _finish(x2, yag, sgb, attn, gate, shift, scale, n2w, wao, wout, wq, *, tm, seq_rows, per_row_mod):
    rows, d = x2.shape
    nt = rows // tm
    tiles_per_seq = max(seq_rows // tm, 1)
    if per_row_mod:
        mod_spec = pl.BlockSpec((tm, d), lambda i: (i, 0))
    else:
        mod_spec = pl.BlockSpec((None, 1, d), lambda i: (i // tiles_per_seq, 0, 0))
    const = lambda shape: pl.BlockSpec(shape, lambda i: tuple(0 for _ in shape))
    row = lambda w: pl.BlockSpec((tm, w), lambda i: (i, 0))
    nq = wq.shape[1]
    return pl.pallas_call(
        _finish_body,
        grid=(nt,),
        in_specs=[row(d), row(d), row(d), row(attn.shape[1]), mod_spec, mod_spec, mod_spec, const((1, d)),
                  const(wao.shape), const(wout.shape), const(wq.shape)],
        out_specs=[row(d), row(d), row(nq)],
        out_shape=[jax.ShapeDtypeStruct((rows, d), F32), jax.ShapeDtypeStruct((rows, d), BF16),
                   jax.ShapeDtypeStruct((rows, nq), BF16)],
        compiler_params=_cparams(("arbitrary",)),
    )(x2, yag, sgb, attn, gate, shift, scale, n2w, wao, wout, wq)


def _top_sorted(x, n):
    vals = []
    for _ in range(n):
        m = jnp.max(x, axis=0, keepdims=True)
        vals.append(m)
        x = jnp.where(x >= m, -jnp.inf, x)
    return jnp.concatenate(vals, axis=0)


def _gelu_tanh(x):
    c = np.float32(np.sqrt(2.0 / np.pi))
    return x * (0.5 * (1.0 + jnp.tanh(c * (x + 0.044715 * (x * x * x)))))


def _peer_select(lg, pq_ref, sk_ref, ct_ref, ea_ref, eb_ref):
    half = PEER_QDIM // 2
    rows = pl.ds(pl.multiple_of(lg * LANES, LANES), LANES)
    for h in range(PEER_HEADS):
        sa = _dot_nt(sk_ref[2 * h], pq_ref[rows, (2 * h) * half:(2 * h + 1) * half])
        sb = _dot_nt(sk_ref[2 * h + 1], pq_ref[rows, (2 * h + 1) * half:(2 * h + 2) * half])
        ta = _top_sorted(sa, PEER_TOPK + 1)
        tb = _top_sorted(sb, PEER_TOPK + 1)
        tail = jnp.concatenate([ta[0:1] + tb[16:17], ta[16:17] + tb[0:1],
                                jnp.full((SUBLANES - 2, LANES), -jnp.inf, F32)], axis=0)
        cand = jnp.concatenate(
            [ta[0:1] + tb[0:16]] + [ta[r:r + 1] + tb[0:8] for r in range(1, 8)] + [ta[8:16] + tb[0:1], tail], axis=0)
        top = _top_sorted(cand, PEER_TOPK + 1)
        thr = 0.5 * (top[PEER_TOPK - 1:PEER_TOPK] + top[PEER_TOPK:PEER_TOPK + 1])
        z = jnp.sum(jnp.exp(top[0:PEER_TOPK] - top[0:1]), axis=0, keepdims=True)
        ct_ref[lg, h] = jnp.exp((thr - tb[0:1]) - sa) / z
        ea_ref[lg, h] = jnp.exp(sa - ta[0:1])
        eb_ref[lg, h] = (jnp.exp(sb - tb[0:1]) / z).reshape(eb_ref.shape[2:])


def _peer_body(ce, tm, nc, prow, h2_ref, pq_ref, x1_ref, gate_ref, sk_ref, *rest):
    n_pairs = ce // prow
    u_refs, vt_refs = rest[0:n_pairs], rest[n_pairs:2 * n_pairs]
    out_ref, ct_ref, ea_ref, eb_ref, yt_ref, p_ref = rest[2 * n_pairs:]
    c = pl.program_id(1)
    kk = PEER_KEYS
    n_lg = tm // LANES
    n_sub = ce // kk
    pair = prow // kk
    jrows = kk // 2

    @pl.when(c == 0)
    def _():
        yt_ref[...] = jnp.zeros_like(yt_ref)

        def body(lg, carry):
            _peer_select(lg, pq_ref, sk_ref, ct_ref, ea_ref, eb_ref)
            return carry

        lax.fori_loop(0, n_lg, body, 0)

    h2 = h2_ref[...]

    def expert_act(ip):
        return _dot_nt(u_refs[ip][...], h2)

    def accumulate(ip):
        yt_ref[...] += _dot(vt_refs[ip][...], p_ref[ip * prow:(ip + 1) * prow, :])

    a_next = expert_act(0)
    for ip in range(n_pairs):
        subs = [pair * ip + t for t in range(pair)]
        a_cur = a_next
        a = [a_cur[t * kk:(t + 1) * kk, :] for t in range(pair)]
        if ip + 1 < n_pairs:
            a_next = expert_act(ip + 1)
        if ip > 0:
            accumulate(ip - 1)
        for lg in range(n_lg):
            cols = slice(lg * LANES, (lg + 1) * LANES)
            for jh in range(2):
                rows = slice(jh * jrows, (jh + 1) * jrows)
                vrows = slice(jh * jrows // SUBLANES, (jh + 1) * jrows // SUBLANES)
                w = [None] * pair
                for h in range(PEER_HEADS):
                    ebh = eb_ref[lg, h, vrows]
                    for t, s in enumerate(subs):
                        i = c * n_sub + s
                        ctv = jnp.broadcast_to(ct_ref[lg, h, pl.ds(i, 1), :], (SUBLANES, LANES))[None]
                        eav = jnp.broadcast_to(ea_ref[lg, h, pl.ds(i, 1), :], (SUBLANES, LANES))[None]
                        term = jnp.where(ebh >= ctv, ebh, 0.0) * eav
                        w[t] = term if w[t] is None else w[t] + term
                for t, s in enumerate(subs):
                    prod = w[t].reshape(jrows, LANES) * _gelu_tanh(a[t][rows, cols])
                    p_ref[s * kk + jh * jrows:s * kk + (jh + 1) * jrows, cols] = prod.astype(BF16)
    accumulate(n_pairs - 1)

    @pl.when(c == nc - 1)
    def _():
        out_ref[...] = x1_ref[...] + gate_ref[...] * yt_ref[...].T


def _peer(h2, pq, x1, gate, sk, u_bf, vt_blk, *, tm, ce, seq_rows, per_row_mod):
    rows, d = x1.shape
    ne = u_bf.shape[0]
    prow = vt_blk.shape[2]
    n_pairs = ce // prow
    nt = rows // tm
    nc = ne // ce
    assert tm % LANES == 0 and rows % tm == 0 and ne % ce == 0 and ce % prow == 0 and prow % PEER_KEYS == 0
    tiles_per_seq = max(seq_rows // tm, 1)
    if per_row_mod:
        mod_spec = pl.BlockSpec((tm, d), lambda i, c: (i, 0))
    else:
        mod_spec = pl.BlockSpec((None, 1, d), lambda i, c: (i // tiles_per_seq, 0, 0))
    row = lambda w: pl.BlockSpec((tm, w), lambda i, c: (i, 0))
    sel_shape = (tm // LANES, PEER_HEADS, PEER_KEYS, LANES)
    vreg_shape = (tm // LANES, PEER_HEADS, PEER_KEYS // SUBLANES, SUBLANES, LANES)
    return pl.pallas_call(
        functools.partial(_peer_body, ce, tm, nc, prow),
        grid=(nt, nc),
        in_specs=[row(d), row(pq.shape[1]), row(d), mod_spec,
                  pl.BlockSpec(sk.shape, lambda i, c: (0, 0, 0))]
                 + [pl.BlockSpec((prow, d), lambda i, c, k=k: (c * n_pairs + k, 0)) for k in range(n_pairs)]
                 + [pl.BlockSpec((None, d, prow), lambda i, c, k=k: (c * n_pairs + k, 0, 0)) for k in range(n_pairs)],
        out_specs=row(d),
        out_shape=jax.ShapeDtypeStruct((rows, d), F32),
        scratch_shapes=[pltpu.VMEM(sel_shape, F32), pltpu.VMEM(sel_shape, F32), pltpu.VMEM(vreg_shape, F32),
                        pltpu.VMEM((d, tm), F32), pltpu.VMEM((ce, tm), BF16)],
        compiler_params=_cparams(("arbitrary", "arbitrary")),
    )(h2, pq, x1, gate, sk, *([u_bf] * n_pairs), *([vt_blk] * n_pairs))


def _rope_tables(pos):
    half = HEAD_DIM // 2
    freqs = ROPE_THETA ** (-jnp.arange(half, dtype=F32) / half)
    ang = pos.astype(F32)[:, None] * freqs[None, :]
    cos, sin = jnp.cos(ang), jnp.sin(ang)
    cos2 = jnp.concatenate([cos, cos, cos, cos], axis=1)
    sin2 = jnp.concatenate([-sin, sin, -sin, sin], axis=1)
    return cos2, sin2


def _pick_tile(rows, pref):
    t = min(pref, rows)
    while rows % t:
        t //= 2
    return t


def kernel(x_prompt, x_sample, cache_k, cache_v, cache_kidx, state_conv, page_table, c_prompt, c_sample, w_ada, b_ada,
           norm1_w, w_in, conv_w, q_norm_w, k_norm_w, w_conv_out, w_attn_out, w_out, norm2_w, peer_w_q, peer_sub_keys,
           peer_u, peer_v):
    bsz, seq, d = x_prompt.shape
    db, ds, _ = x_sample.shape
    depth = w_ada.shape[0]
    dc = conv_w.shape[2]
    hq = N_HEADS * HEAD_DIM
    hk = N_KV_HEADS * HEAD_DIM
    hi_w = IDX_HEADS * IDX_DIM
    n_pages = page_table.shape[1]
    page = cache_k.shape[2]
    past = n_pages * page
    group = N_HEADS // N_KV_HEADS

    pos_p = jnp.arange(seq, dtype=I32)
    pos_s = past + jnp.arange(ds, dtype=I32)
    cos_p, sin_p = _rope_tables(pos_p)
    cos_s, sin_s = _rope_tables(jnp.tile(pos_s, db))
    gsum = jnp.kron(jnp.eye(N_HEADS, dtype=F32), jnp.ones((HEAD_DIM, HEAD_DIM), F32)).astype(BF16)

    xp = x_prompt.reshape(bsz * seq, d)
    xs = x_sample.reshape(db * ds, d)
    outs = {n: [] for n in ("kp", "vp", "kip", "cp", "ks", "vs", "kis", "cs")}

    for layer in range(depth):
        mod = _adaln(jnp.concatenate([c_prompt, c_sample], axis=0), w_ada[layer], b_ada[layer])
        mod_p = [m.reshape(bsz, 1, d) for m in jnp.split(mod[:bsz], N_MOD, axis=-1)]
        mod_s = [jnp.repeat(m, ds, axis=0) for m in jnp.split(mod[bsz:], N_MOD, axis=-1)]

        w = w_in[layer]
        o_ki = 3 * dc + hq + 2 * hk + hi_w
        o_g = o_ki + IDX_DIM + IDX_HEADS
        slab = jnp.concatenate([w[:, o_ki:o_g], jnp.zeros((d, LANES - IDX_DIM - IDX_HEADS), F32)], axis=1)
        w_pad = jnp.concatenate([w[:, :o_ki], slab, w[:, o_g:]], axis=1).astype(BF16)
        n1w = norm1_w[layer].reshape(1, d)
        qnw = jnp.tile(q_norm_w[layer], N_HEADS).reshape(1, hq)
        knw = jnp.tile(k_norm_w[layer], N_HEADS).reshape(1, hq)
        wco = w_conv_out[layer].astype(BF16)
        wao = w_attn_out[layer].astype(BF16)
        wout = w_out[layer].astype(BF16)
        wq = peer_w_q[layer].astype(BF16)
        n2w = norm2_w[layer].reshape(1, d)
        sk = jnp.transpose(peer_sub_keys[layer], (1, 0, 2, 3)).reshape(2 * PEER_HEADS, PEER_KEYS, PEER_QDIM // 2)
        sk = sk.astype(BF16)
        u_bf = peer_u[layer].astype(BF16)
        ce = min(1024, u_bf.shape[0])
        prow = 2 * PEER_KEYS
        vt_bf = peer_v[layer].astype(BF16).reshape(-1, prow, d).transpose(0, 2, 1)

        tm_p = _pick_tile(seq, 256)
        zero_ext = jnp.zeros((1, dc), F32)
        u, q, k, v, qi, kiw, yag, sgb, vt, wit = _inproj(
            xp, mod_p[0], mod_p[1], n1w, w_pad, cos_p, sin_p, qnw, knw, gsum, conv_w[layer], wco, zero_ext, zero_ext,
            tm=tm_p, seq_rows=seq, per_row_mod=False)
        ki = kiw[:, :IDX_DIM]
        attn = _pattn(q, qi, wit, k, vt, kiw, bsz=bsz, t=seq)
        tm_f = _pick_tile(seq, 512)
        x1, h2, pq = _finish(xp, yag, sgb, attn, mod_p[2], mod_p[3], mod_p[4], n2w, wao, wout,
                             wq, tm=tm_f, seq_rows=seq, per_row_mod=False)
        xp = _peer(h2, pq, x1, mod_p[5], sk, u_bf, vt_bf, tm=tm_f, ce=ce, seq_rows=seq, per_row_mod=False)
        outs["kp"].append(k.reshape(bsz, seq, N_KV_HEADS, HEAD_DIM))
        outs["vp"].append(v.reshape(bsz, seq, N_KV_HEADS, HEAD_DIM))
        outs["kip"].append(ki.reshape(bsz, seq, IDX_DIM))
        outs["cp"].append(u.reshape(bsz, seq, dc)[:, seq - 2:, :])

        rows_s = db * ds
        st = state_conv[layer]
        posr = jnp.tile(jnp.arange(ds), db)
        ext1 = jnp.repeat(st[:, 1, :], ds, axis=0)
        ext2 = jnp.where((posr == 0)[:, None], jnp.repeat(st[:, 0, :], ds, axis=0), ext1)
        u, q, k, v, qi, kiw, yag, sgb, _, _ = _inproj(
            xs, mod_s[0], mod_s[1], n1w, w_pad, cos_s, sin_s, qnw, knw, gsum, conv_w[layer], wco, ext1, ext2,
            tm=rows_s, seq_rows=ds, per_row_mod=True)
        ki = kiw[:, :IDX_DIM]
        wi = kiw[:, IDX_DIM:IDX_DIM + IDX_HEADS]
        pad_q = lambda a: jnp.pad(a, ((0, 0), (0, 0), (0, SUBLANES - ds), (0, 0)))
        qi_arr = pad_q(qi.reshape(db, ds, IDX_HEADS, IDX_DIM).transpose(0, 2, 1, 3)).reshape(db, IDX_HEADS * SUBLANES, IDX_DIM)
        w_arr = pad_q(wi.reshape(db, ds, IDX_HEADS, 1).transpose(0, 2, 1, 3)).reshape(db, IDX_HEADS * SUBLANES, 1)
        w_arr = jnp.broadcast_to(w_arr, (db, IDX_HEADS * SUBLANES, LANES))
        q_arr = pad_q(q.reshape(db, ds, N_HEADS, HEAD_DIM).transpose(0, 2, 1, 3)).reshape(db, N_KV_HEADS, group * SUBLANES, HEAD_DIM)
        pad_page = lambda a: jnp.pad(a.reshape(db, ds, -1), ((0, 0), (0, page - ds), (0, 0))).transpose(0, 2, 1)
        cki = jnp.transpose(cache_kidx[layer], (0, 2, 1))
        ck = jnp.transpose(cache_k[layer], (0, 2, 3, 1)).reshape(-1, hk, page)
        cv = jnp.transpose(cache_v[layer], (0, 2, 3, 1)).reshape(-1, hk, page)
        o = _sattn(page_table, qi_arr, w_arr, q_arr, pad_page(ki), pad_page(k), pad_page(v), cki, ck, cv, ds=ds)
        attn = o.reshape(db, N_HEADS, SUBLANES, HEAD_DIM)[:, :, :ds, :].transpose(0, 2, 1, 3).reshape(rows_s, hq)
        x1, h2, pq = _finish(xs, yag, sgb, attn, mod_s[2], mod_s[3], mod_s[4], n2w, wao, wout, wq,
                             tm=rows_s, seq_rows=ds, per_row_mod=True)
        xs = _peer(h2, pq, x1, mod_s[5], sk, u_bf, vt_bf, tm=rows_s, ce=ce, seq_rows=ds, per_row_mod=True)
        outs["ks"].append(k.reshape(db, ds, N_KV_HEADS, HEAD_DIM))
        outs["vs"].append(v.reshape(db, ds, N_KV_HEADS, HEAD_DIM))
        outs["kis"].append(ki.reshape(db, ds, IDX_DIM))
        full = jnp.concatenate([st, u.reshape(db, ds, dc)], axis=1)
        outs["cs"].append(full[:, full.shape[1] - 2:, :])

    stack = lambda n: jnp.stack(outs[n])
    return (xp.reshape(bsz, seq, d), xs.reshape(db, ds, d), stack("kp"), stack("vp"), stack("kip"), stack("cp"),
            stack("ks"), stack("vs"), stack("kis"), stack("cs"))
```

```python
import functools

import numpy as np
import jax
import jax.numpy as jnp
from jax import lax
from jax.experimental import pallas as pl
from jax.experimental.pallas import tpu as pltpu

F32 = jnp.float32
BF16 = jnp.bfloat16
I32 = jnp.int32

N_HEADS = 8
N_KV_HEADS = 2
HEAD_DIM = 64
IDX_HEADS = 8
IDX_DIM = 64
IDX_SCALE = (IDX_DIM ** -0.5) * (IDX_HEADS ** -0.5)
TOPK_MAX = 256
Q_BLOCK = 128
ROPE_THETA = 10000.0
PEER_HEADS = 8
PEER_KEYS = 128
PEER_QDIM = 256
PEER_TOPK = 16
N_MOD = 6
NORM_EPS = 1e-6
NEG_INF = -1e30
INT_MIN = -2 ** 31

LANES = 128
SUBLANES = 8
VMEM_LIMIT = 56 * 1024 * 1024


def _cparams(sem):
    return pltpu.CompilerParams(dimension_semantics=sem, vmem_limit_bytes=VMEM_LIMIT)


def _dot(a, b):
    return jnp.dot(a, b, preferred_element_type=F32)


def _dot_nt(a, b):
    return lax.dot_general(a, b, (((1,), (1,)), ((), ())), preferred_element_type=F32)


def _adaln_body(c_ref, w_ref, b_ref, o_ref):
    o_ref[...] = jnp.dot(c_ref[...], w_ref[...], preferred_element_type=F32,
                         precision=lax.Precision.HIGHEST) + b_ref[...]


def _adaln(c, w_ada, b_ada):
    m, d = c.shape
    n = w_ada.shape[1]
    tn = d
    return pl.pallas_call(
        _adaln_body,
        grid=(n // tn,),
        in_specs=[pl.BlockSpec((m, d), lambda j: (0, 0)),
                  pl.BlockSpec((d, tn), lambda j: (0, j)),
                  pl.BlockSpec((1, tn), lambda j: (0, j))],
        out_specs=pl.BlockSpec((m, tn), lambda j: (0, j)),
        out_shape=jax.ShapeDtypeStruct((m, n), F32),
        compiler_params=_cparams(("arbitrary",)),
    )(c, w_ada, b_ada.reshape(1, n))


def _rope_lanes(z, cos, sin_signed):
    w = z.shape[1]
    reps = w // LANES
    cosw = jnp.concatenate([cos] * reps, axis=1) if reps > 1 else cos
    sinw = jnp.concatenate([sin_signed] * reps, axis=1) if reps > 1 else sin_signed
    half = HEAD_DIM // 2
    lane = lax.broadcasted_iota(I32, (1, w), 1)
    first = (lane % HEAD_DIM) < half
    swapped = jnp.where(first, pltpu.roll(z, w - half, 1), pltpu.roll(z, half, 1))
    return z * cosw + swapped * sinw


def _head_rms(z, nw, gsum):
    z2 = z * z
    hi = z2.astype(BF16)
    lo = (z2 - hi.astype(F32)).astype(BF16)
    ssum = _dot(hi, gsum) + _dot(lo, gsum)
    return z * lax.rsqrt(ssum * (1.0 / HEAD_DIM) + NORM_EPS) * nw


def _inproj_body(tm, seq_rows, dc, x_ref, shift_ref, scale_ref, n1w_ref, w_ref, cos_ref, sin_ref, qnw_ref, knw_ref,
                 gsum_ref, convw_ref, wco_ref, ext1_ref, ext2_ref,
                 u_ref, q_ref, k_ref, v_ref, qi_ref, kiw_ref, yag_ref, sgb_ref, vt_ref, wit_ref, carry_ref, buf_ref):
    i = pl.program_id(0)
    hq = N_HEADS * HEAD_DIM
    hk = N_KV_HEADS * HEAD_DIM
    hi_w = IDX_HEADS * IDX_DIM
    d = x_ref.shape[1]
    offs = np.cumsum([0, dc, dc, dc, hq, hk, hk, hi_w, LANES, d, d])

    def proj(s):
        return _dot(hb, w_ref[:, int(offs[s]):int(offs[s + 1])])

    x = x_ref[...]
    ms = jnp.mean(x * x, axis=-1, keepdims=True)
    h = x * lax.rsqrt(ms + NORM_EPS) * n1w_ref[...]
    h = h * (1.0 + scale_ref[...]) + shift_ref[...]
    hb = h.astype(BF16)

    u = proj(2) * proj(0)
    u_ref[...] = u

    @pl.when(i == 0)
    def _():
        carry_ref[...] = jnp.zeros_like(carry_ref)

    buf_ref[0:SUBLANES, :] = carry_ref[...]
    buf_ref[SUBLANES:SUBLANES + tm, :] = u
    carry_ref[...] = u[tm - SUBLANES:tm, :]
    pos = (i * tm + lax.broadcasted_iota(I32, (tm, 1), 0)) % seq_rows
    u1 = jnp.where(pos >= 1, buf_ref[SUBLANES - 1:SUBLANES - 1 + tm, :], ext1_ref[...])
    u2 = jnp.where(pos >= 2, buf_ref[SUBLANES - 2:SUBLANES - 2 + tm, :], ext2_ref[...])
    yc = convw_ref[2:3, :] * u + convw_ref[0:1, :] * u2 + convw_ref[1:2, :] * u1
    a_pre = proj(1) * yc
    ya = _dot(a_pre.astype(BF16), wco_ref[...])
    yag_ref[...] = jax.nn.sigmoid(proj(8)) * ya
    sgb_ref[...] = jax.nn.sigmoid(proj(9))

    cos = cos_ref[...]
    sin = sin_ref[...]
    gsum = gsum_ref[...]
    q_ref[...] = _rope_lanes(_head_rms(proj(3), qnw_ref[...], gsum), cos, sin)
    k_ref[...] = _rope_lanes(_head_rms(proj(4), knw_ref[:, 0:hk], gsum[0:hk, 0:hk]), cos, sin)
    v = proj(5)
    v_ref[...] = v
    vt_ref[...] = v.T
    qi_ref[...] = _rope_lanes(proj(6), cos, sin)
    slab = proj(7)
    lane = lax.broadcasted_iota(I32, (1, LANES), 1)
    kiw = jnp.where(lane < IDX_DIM, _rope_lanes(slab, cos, sin), slab * IDX_SCALE)
    kiw_ref[...] = kiw
    wit_ref[...] = kiw.T[IDX_DIM:IDX_DIM + IDX_HEADS, :]


def _inproj(x2, shift, scale, n1w, w_pad, cos, sin, qnw, knw, gsum, convw, wco, ext1, ext2, *, tm, seq_rows,
            per_row_mod):
    rows, d = x2.shape
    dc = convw.shape[1]
    hq = N_HEADS * HEAD_DIM
    hk = N_KV_HEADS * HEAD_DIM
    hi_w = IDX_HEADS * IDX_DIM
    nt = rows // tm
    pos_tiles = cos.shape[0] // tm
    tiles_per_seq = max(seq_rows // tm, 1)
    if per_row_mod:
        mod_spec = pl.BlockSpec((tm, d), lambda i: (i, 0))
        ext_spec = pl.BlockSpec((tm, dc), lambda i: (i, 0))
    else:
        mod_spec = pl.BlockSpec((None, 1, d), lambda i: (i // tiles_per_seq, 0, 0))
        ext_spec = pl.BlockSpec((1, dc), lambda i: (0, 0))
    const = lambda shape: pl.BlockSpec(shape, lambda i: tuple(0 for _ in shape))
    row = lambda w: pl.BlockSpec((tm, w), lambda i: (i, 0))
    out_widths = [dc, hq, hk, hk, hi_w, LANES, d, d]
    return pl.pallas_call(
        functools.partial(_inproj_body, tm, seq_rows, dc),
        grid=(nt,),
        in_specs=[row(d), mod_spec, mod_spec, const((1, d)), const(w_pad.shape),
                  pl.BlockSpec((tm, LANES), lambda i: (i % pos_tiles, 0)),
                  pl.BlockSpec((tm, LANES), lambda i: (i % pos_tiles, 0)),
                  const((1, hq)), const((1, hq)), const((hq, hq)), const(convw.shape), const(wco.shape),
                  ext_spec, ext_spec],
        out_specs=[row(w) for w in out_widths] + [pl.BlockSpec((hk, tm), lambda i: (0, i)),
                                                  pl.BlockSpec((IDX_HEADS, tm), lambda i: (0, i))],
        out_shape=[jax.ShapeDtypeStruct((rows, w), F32) for w in out_widths]
                  + [jax.ShapeDtypeStruct((hk, rows), F32), jax.ShapeDtypeStruct((IDX_HEADS, rows), F32)],
        scratch_shapes=[pltpu.VMEM((SUBLANES, dc), F32), pltpu.VMEM((tm + SUBLANES, dc), F32)],
        compiler_params=_cparams(("arbitrary",)),
    )(x2, shift, scale, n1w, w_pad, cos, sin, qnw, knw, gsum, convw, wco, ext1, ext2)


def _sortable_key(score):
    bits = lax.bitcast_convert_type(score, I32)
    key = jnp.where(bits < 0, bits ^ jnp.int32(0x7FFFFFFF), bits)
    return jnp.where(score == 0.0, jnp.int32(0), key)


def _kth_largest_key(count_ge, shape, k):
    ans = jnp.where(count_ge(jnp.zeros(shape, I32)) >= k, jnp.int32(0), jnp.int32(INT_MIN))

    def step(it, ans):
        cand = ans + jnp.left_shift(jnp.int32(1), jnp.int32(30) - it)
        return jnp.where(count_ge(cand) >= k, cand, ans)

    return lax.fori_loop(0, 31, step, ans)


def _tie_cut(count_eq_below, shape, need, nbits):
    def step(it, c):
        cand = c + jnp.left_shift(jnp.int32(1), jnp.int32(nbits - 1) - it)
        return jnp.where(count_eq_below(cand) < need, cand, c)

    return lax.fori_loop(0, nbits, step, jnp.zeros(shape, I32))


def _col_reduce(x, op, rows=64):
    n = x.shape[0]
    if n > rows and n % rows == 0:
        x = op(x.reshape(n // rows, rows, x.shape[1]), axis=0)
    return op(x, axis=0, keepdims=True)


def _pattn_class(tq, klen, topk, j, q_ref, qi_ref, wit_ref, k_ref, vt_ref, kiw_ref, o_ref, key_ref, cut_ref):
    qi = qi_ref[...]
    wit = wit_ref[...]
    ki = kiw_ref[0:klen, 0:IDX_DIM].astype(BF16)
    score = jnp.zeros((klen, tq), F32)
    hpm = 4
    for h0 in range(0, IDX_HEADS, hpm):
        qs = jnp.concatenate([qi[:, h * IDX_DIM:(h + 1) * IDX_DIM] for h in range(h0, h0 + hpm)], axis=0)
        s = _dot_nt(ki, qs.astype(BF16))
        for g in range(hpm):
            score = score + wit[h0 + g:h0 + g + 1, :] * jnp.maximum(s[:, g * tq:(g + 1) * tq], 0.0)
    kpos = lax.broadcasted_iota(I32, (klen, 1), 0)
    qpos = j * tq + lax.broadcasted_iota(I32, (1, tq), 1)
    causal = kpos <= qpos
    score = jnp.where(causal, score, NEG_INF)
    key_ref[0:klen, :] = _sortable_key(score)

    def count_ge(c):
        return _col_reduce((key_ref[0:klen, :] >= c).astype(F32), jnp.sum)

    thr = _kth_largest_key(count_ge, (1, tq), topk)
    key = key_ref[0:klen, :]
    gt = key > thr
    eq = key == thr
    need = topk - _col_reduce(gt.astype(F32), jnp.sum)
    n_eq = _col_reduce(eq.astype(F32), jnp.sum)
    cut_ref[...] = jnp.full(cut_ref.shape, klen, I32)

    @pl.when(jnp.max(n_eq - need) > 0)
    def _():
        def count_eq_below(c):
            hit = (key_ref[0:klen, :] == thr) & (kpos < c)
            return _col_reduce(hit.astype(F32), jnp.sum)

        cut = _tie_cut(count_eq_below, (1, tq), need, int(klen - 1).bit_length())
        cut_ref[...] = jnp.broadcast_to(cut, cut_ref.shape)

    cut = cut_ref[0:1, :]
    bias = jnp.where((gt | (eq & (kpos <= cut))) & causal, 0.0, NEG_INF)

    q = q_ref[...] * (HEAD_DIM ** -0.5)
    k = k_ref[0:klen, :].astype(BF16)
    group = N_HEADS // N_KV_HEADS
    outs = []
    for h in range(N_HEADS):
        n = h // group
        if h % group == 0:
            qs = jnp.concatenate([q[:, g * HEAD_DIM:(g + 1) * HEAD_DIM] for g in range(h, h + group)], axis=0)
            logits_n = _dot_nt(k[:, n * HEAD_DIM:(n + 1) * HEAD_DIM], qs.astype(BF16))
        logits = logits_n[:, (h % group) * tq:(h % group + 1) * tq] + bias
        m = _col_reduce(logits, jnp.max)
        p = jnp.exp(logits - m)
        l = _col_reduce(p, jnp.sum)
        vt = vt_ref[n * HEAD_DIM:(n + 1) * HEAD_DIM, 0:klen].astype(BF16)
        outs.append(_dot(vt, p.astype(BF16)) / l)
    o_ref[...] = jnp.concatenate(outs, axis=0).T


def _pattn_body(tq, t, topk, n_cls, q_ref, qi_ref, wit_ref, k_ref, vt_ref, kiw_ref, o_ref, key_ref, cut_ref):
    j = pl.program_id(1)
    blocks_per_cls = (t // tq) // n_cls
    for c in range(n_cls):
        @pl.when((j >= c * blocks_per_cls) & (j < (c + 1) * blocks_per_cls))
        def _(c=c):
            _pattn_class(tq, (c + 1) * blocks_per_cls * tq, topk, j, q_ref, qi_ref, wit_ref, k_ref, vt_ref, kiw_ref,
                         o_ref, key_ref, cut_ref)


def _pattn(q, qi, wit, k, vt, kiw, *, bsz, t):
    hq = q.shape[1]
    tq = Q_BLOCK
    nqb = t // tq
    n_cls = 4 if nqb % 4 == 0 else 1
    topk = min(TOPK_MAX, t // 4)
    qspec = lambda w: pl.BlockSpec((tq, w), lambda bi, j: (bi * nqb + j, 0))
    kspec = lambda w: pl.BlockSpec((t, w), lambda bi, j: (bi, 0))
    return pl.pallas_call(
        functools.partial(_pattn_body, tq, t, topk, n_cls),
        grid=(bsz, nqb),
        in_specs=[qspec(hq), qspec(qi.shape[1]),
                  pl.BlockSpec((wit.shape[0], tq), lambda bi, j: (0, bi * nqb + j)),
                  kspec(k.shape[1]),
                  pl.BlockSpec((vt.shape[0], t), lambda bi, j: (0, bi)),
                  kspec(kiw.shape[1])],
        out_specs=qspec(hq),
        out_shape=jax.ShapeDtypeStruct((bsz * t, hq), F32),
        scratch_shapes=[pltpu.VMEM((t, tq), I32), pltpu.VMEM((SUBLANES, tq), I32)],
        compiler_params=_cparams(("arbitrary", "arbitrary")),
    )(q, qi, wit, k, vt, kiw)


def _sattn_body(n_pages, g, ds, topk, pt_ref, qi_ref, w_ref, q_ref, kin_ref, kn_ref, vn_ref, *rest):
    kic, kc, vc = rest[0:g], rest[g:2 * g], rest[2 * g:3 * g]
    o_ref, score_ref, mask_ref, m_ref, l_ref, acc_ref, kicat_ref, kcat_ref, vcat_ref = rest[3 * g:]
    ph = pl.program_id(1)
    p = pl.program_id(2)
    n_steps = n_pages // g
    group = N_HEADS // N_KV_HEADS
    sub = lax.broadcasted_iota(I32, (SUBLANES, LANES), 0)
    lane = lax.broadcasted_iota(I32, (SUBLANES, LANES), 1)
    new_valid = (lane <= sub) & (lane < ds)

    def scores(ki_bf, npg):
        s = _dot(qi_ref[...].astype(BF16), ki_bf)
        w = w_ref[...]
        w = jnp.concatenate([w] * npg, axis=1) if npg > 1 else w
        acc = jnp.zeros((SUBLANES, npg * LANES), F32)
        for h in range(IDX_HEADS):
            r = slice(h * SUBLANES, (h + 1) * SUBLANES)
            acc = acc + w[r, :] * jnp.maximum(s[r, :], 0.0)
        return acc

    @pl.when(ph == 0)
    def _():
        for gi in range(g):
            kicat_ref[:, gi * LANES:(gi + 1) * LANES] = kic[gi][...].astype(BF16)
        sc = scores(kicat_ref[...], g)
        for gi in range(g):
            score_ref[p * g + gi] = sc[:, gi * LANES:(gi + 1) * LANES]

        @pl.when(p == 0)
        def _():
            score_ref[n_pages] = jnp.where(new_valid, scores(kin_ref[...].astype(BF16), 1), NEG_INF)

    def attend(mask8, kb, vb):
        maskg = jnp.concatenate([mask8] * group, axis=0) > 0.5
        for n in range(N_KV_HEADS):
            qn = (q_ref[n] * (HEAD_DIM ** -0.5)).astype(BF16)
            logits = _dot(qn, kb[n * HEAD_DIM:(n + 1) * HEAD_DIM, :])
            logits = jnp.where(maskg, logits, NEG_INF)
            m_old = m_ref[n][:, 0:1]
            m_new = jnp.maximum(m_old, jnp.max(logits, axis=1, keepdims=True))
            pr = jnp.where(maskg, jnp.exp(logits - m_new), 0.0)
            alpha = jnp.exp(m_old - m_new)
            l_new = alpha * l_ref[n][:, 0:1] + jnp.sum(pr, axis=1, keepdims=True)
            acc_ref[n] = alpha * acc_ref[n] + _dot_nt(pr.astype(BF16), vb[n * HEAD_DIM:(n + 1) * HEAD_DIM, :])
            m_ref[n] = jnp.broadcast_to(m_new, m_ref.shape[1:])
            l_ref[n] = jnp.broadcast_to(l_new, l_ref.shape[1:])

    @pl.when(ph == 1)
    def _():
        @pl.when(p == 0)
        def _():
            key = _sortable_key(score_ref[...])
            slab = lax.broadcasted_iota(I32, key.shape, 0)
            idx = slab * LANES + lax.broadcasted_iota(I32, key.shape, 2)

            def count(hit):
                x = hit.astype(F32)
                parts = [jnp.sum(x[s:s + 16], axis=0) for s in range(0, x.shape[0], 16)]
                c = functools.reduce(lambda u, v: u + v, parts)
                return jnp.sum(c, axis=1, keepdims=True)[None]

            thr = _kth_largest_key(lambda c: count(key >= c), (1, SUBLANES, 1), topk)
            gt = key > thr
            eq = key == thr
            need = topk - count(gt)
            nbits = int((n_pages + 1) * LANES - 1).bit_length()
            cut = _tie_cut(lambda c: count(eq & (idx < c)), (1, SUBLANES, 1), need, nbits)
            sel = gt | (eq & (idx <= cut))
            valid = (slab < n_pages) | new_valid[None]
            mask_ref[...] = (sel & valid).astype(F32)
            m_ref[...] = jnp.full(m_ref.shape, NEG_INF, F32)
            l_ref[...] = jnp.zeros(l_ref.shape, F32)
            acc_ref[...] = jnp.zeros(acc_ref.shape, F32)
            attend(mask_ref[n_pages], kn_ref[...].astype(BF16), vn_ref[...].astype(BF16))

        for gi in range(g):
            kcat_ref[:, gi * LANES:(gi + 1) * LANES] = kc[gi][...].astype(BF16)
            vcat_ref[:, gi * LANES:(gi + 1) * LANES] = vc[gi][...].astype(BF16)
        mask = jnp.concatenate([mask_ref[p * g + gi] for gi in range(g)], axis=1)
        attend(mask, kcat_ref[...], vcat_ref[...])

        @pl.when(p == n_steps - 1)
        def _():
            o_ref[...] = acc_ref[...] / l_ref[...][:, :, 0:HEAD_DIM]


def _sattn(page_table, qi_arr, w_arr, q_arr, kin, kn, vn, cache_kidx, cache_k, cache_v, *, ds):
    db, n_pages = page_table.shape
    page = cache_k.shape[2]
    assert page == LANES
    g = max(c for c in (1, 2, 4, 8, 16, 32, 64) if n_pages % c == 0)
    n_steps = n_pages // g
    topk = min(TOPK_MAX, (n_pages * page + ds) // 4)
    group = N_HEADS // N_KV_HEADS
    hk = N_KV_HEADS * HEAD_DIM
    rows_i = IDX_HEADS * SUBLANES
    rows_q = group * SUBLANES
    pt = page_table.reshape(-1)
    per_b = lambda *shape: pl.BlockSpec((None,) + shape, lambda b, ph, p, pt: (b,) + tuple(0 for _ in shape))

    def kic_map(gi, b, ph, p, pt):
        return (pt[b * n_pages + (p * (1 - ph) + (n_steps - 1) * ph) * g + gi], 0, 0)

    def kvc_map(gi, b, ph, p, pt):
        return (pt[b * n_pages + p * ph * g + gi], 0, 0)

    kic_specs = [pl.BlockSpec((None, IDX_DIM, page), functools.partial(kic_map, gi)) for gi in range(g)]
    kvc_specs = [pl.BlockSpec((None, hk, page), functools.partial(kvc_map, gi)) for gi in range(g)]
    grid_spec = pltpu.PrefetchScalarGridSpec(
        num_scalar_prefetch=1,
        grid=(db, 2, n_steps),
        in_specs=[per_b(rows_i, IDX_DIM), per_b(rows_i, LANES), per_b(N_KV_HEADS, rows_q, HEAD_DIM),
                  per_b(IDX_DIM, page), per_b(hk, page), per_b(hk, page)] + kic_specs + kvc_specs + kvc_specs,
        out_specs=per_b(N_KV_HEADS, rows_q, HEAD_DIM),
        scratch_shapes=[pltpu.VMEM((n_pages + 1, SUBLANES, LANES), F32),
                        pltpu.VMEM((n_pages + 1, SUBLANES, LANES), F32),
                        pltpu.VMEM((N_KV_HEADS, rows_q, LANES), F32),
                        pltpu.VMEM((N_KV_HEADS, rows_q, LANES), F32),
                        pltpu.VMEM((N_KV_HEADS, rows_q, HEAD_DIM), F32),
                        pltpu.VMEM((IDX_DIM, g * page), BF16),
                        pltpu.VMEM((hk, g * page), BF16),
                        pltpu.VMEM((hk, g * page), BF16)],
    )
    return pl.pallas_call(
        functools.partial(_sattn_body, n_pages, g, ds, topk),
        grid_spec=grid_spec,
        out_shape=jax.ShapeDtypeStruct((db, N_KV_HEADS, rows_q, HEAD_DIM), F32),
        compiler_params=_cparams(("arbitrary", "arbitrary", "arbitrary")),
    )(pt, qi_arr, w_arr, q_arr, kin, kn, vn, *([cache_kidx] * g), *([cache_k] * g), *([cache_v] * g))


def _finish_body(x_ref, yag_ref, sgb_ref, attn_ref, gate_ref, shift_ref, scale_ref, n2w_ref, wao_ref, wout_ref, wq_ref,
                 x1_ref, h2t_ref, pq_ref):
    yb = _dot(attn_ref[...].astype(BF16), wao_ref[...])
    mix = yag_ref[...] + sgb_ref[...] * yb
    x1 = x_ref[...] + gate_ref[...] * _dot(mix.astype(BF16), wout_ref[...])
    x1_ref[...] = x1
    ms = jnp.mean(x1 * x1, axis=-1, keepdims=True)
    h2 = x1 * lax.rsqrt(ms + NORM_EPS) * n2w_ref[...]
    h2 = h2 * (1.0 + scale_ref[...]) + shift_ref[...]
    h2t_ref[...] = h2.T.astype(BF16)
    pq_ref[...] = _dot(h2.astype(BF16), wq_ref[...]).astype(BF16)


def _finish(x2, yag, sgb, attn, gate, shift, scale, n2w, wao, wout, wq, *, tm, seq_rows, per_row_mod):
    rows, d = x2.shape
    nt = rows // tm
    tiles_per_seq = max(seq_rows // tm, 1)
    if per_row_mod:
        mod_spec = pl.BlockSpec((tm, d), lambda i: (i, 0))
    else:
        mod_spec = pl.BlockSpec((None, 1, d), lambda i: (i // tiles_per_seq, 0, 0))
    const = lambda shape: pl.BlockSpec(shape, lambda i: tuple(0 for _ in shape))
    row = lambda w: pl.BlockSpec((tm, w), lambda i: (i, 0))
    nq = wq.shape[1]
    return pl.pallas_call(
        _finish_body,
        grid=(nt,),
        in_specs=[row(d), row(d), row(d), row(attn.shape[1]), mod_spec, mod_spec, mod_spec, const((1, d)),
                  const(wao.shape), const(wout.shape), const(wq.shape)],
        out_specs=[row(d), pl.BlockSpec((d, tm), lambda i: (0, i)), row(nq)],
        out_shape=[jax.ShapeDtypeStruct((rows, d), F32), jax.ShapeDtypeStruct((d, rows), BF16),
                   jax.ShapeDtypeStruct((rows, nq), BF16)],
        compiler_params=_cparams(("arbitrary",)),
    )(x2, yag, sgb, attn, gate, shift, scale, n2w, wao, wout, wq)


def _top_sorted(x, n):
    vals = []
    for _ in range(n):
        m = jnp.max(x, axis=0, keepdims=True)
        vals.append(m)
        x = jnp.where(x >= m, -jnp.inf, x)
    return jnp.concatenate(vals, axis=0)


def _gelu_tanh(x):
    c = np.float32(np.sqrt(2.0 / np.pi))
    return x * (0.5 * (1.0 + jnp.tanh(c * (x + 0.044715 * (x * x * x)))))


def _peer_select(lg, pq_ref, sk_ref, ct_ref, ea_ref, eb_ref):
    half = PEER_QDIM // 2
    rows = pl.ds(pl.multiple_of(lg * LANES, LANES), LANES)
    for h in range(PEER_HEADS):
        sa = _dot_nt(sk_ref[2 * h], pq_ref[rows, (2 * h) * half:(2 * h + 1) * half])
        sb = _dot_nt(sk_ref[2 * h + 1], pq_ref[rows, (2 * h + 1) * half:(2 * h + 2) * half])
        ta = _top_sorted(sa, PEER_TOPK + 1)
        tb = _top_sorted(sb, PEER_TOPK + 1)
        tail = jnp.concatenate([ta[0:1] + tb[16:17], ta[16:17] + tb[0:1],
                                jnp.full((SUBLANES - 2, LANES), -jnp.inf, F32)], axis=0)
        cand = jnp.concatenate(
            [ta[0:1] + tb[0:16]] + [ta[r:r + 1] + tb[0:8] for r in range(1, 8)] + [ta[8:16] + tb[0:1], tail], axis=0)
        top = _top_sorted(cand, PEER_TOPK + 1)
        thr = 0.5 * (top[PEER_TOPK - 1:PEER_TOPK] + top[PEER_TOPK:PEER_TOPK + 1])
        z = jnp.sum(jnp.exp(top[0:PEER_TOPK] - top[0:1]), axis=0, keepdims=True)
        ct_ref[lg, h] = jnp.exp((thr - tb[0:1]) - sa) / z
        ea_ref[lg, h] = jnp.exp(sa - ta[0:1])
        eb_ref[lg, h] = (jnp.exp(sb - tb[0:1]) / z).reshape(eb_ref.shape[2:])


def _peer_body(ce, tm, nc, h2t_ref, pq_ref, x1_ref, gate_ref, sk_ref, u_ref, vt_ref, out_ref,
               ct_ref, ea_ref, eb_ref, yt_ref, p_ref, act_ref):
    c = pl.program_id(1)
    kk = PEER_KEYS
    n_lg = tm // LANES
    n_sub = ce // kk
    pair = 2 if n_sub % 2 == 0 else 1
    jrows = kk // 2
    n_half = 2 if n_lg % 2 == 0 else 1
    hw = tm // n_half

    @pl.when(c == 0)
    def _():
        yt_ref[...] = jnp.zeros_like(yt_ref)

        def body(lg, carry):
            _peer_select(lg, pq_ref, sk_ref, ct_ref, ea_ref, eb_ref)
            return carry

        lax.fori_loop(0, n_lg, body, 0)

    def expert_act(n):
        act_ref[:, n * hw:(n + 1) * hw] = _dot(u_ref[...], h2t_ref[:, n * hw:(n + 1) * hw])

    expert_act(0)
    for n in range(n_half):
        for lg in range(n * n_lg // n_half, (n + 1) * n_lg // n_half):
            if n + 1 < n_half and lg == (n + 1) * n_lg // n_half - 1:
                expert_act(n + 1)
            cols = slice(lg * LANES, (lg + 1) * LANES)
            for ip in range(n_sub // pair):
                subs = [pair * ip + t for t in range(pair)]
                for jh in range(2):
                    vrows = slice(jh * jrows // SUBLANES, (jh + 1) * jrows // SUBLANES)
                    w = [None] * pair
                    for h in range(PEER_HEADS):
                        ebh = eb_ref[lg, h, vrows]
                        for t, s in enumerate(subs):
                            i = c * n_sub + s
                            ctv = jnp.broadcast_to(ct_ref[lg, h, pl.ds(i, 1), :], (SUBLANES, LANES))[None]
                            eav = jnp.broadcast_to(ea_ref[lg, h, pl.ds(i, 1), :], (SUBLANES, LANES))[None]
                            term = jnp.where(ebh >= ctv, ebh, 0.0) * eav
                            w[t] = term if w[t] is None else w[t] + term
                    for t, s in enumerate(subs):
                        rows = slice(s * kk + jh * jrows, s * kk + (jh + 1) * jrows)
                        prod = w[t].reshape(jrows, LANES) * _gelu_tanh(act_ref[rows, cols])
                        p_ref[rows, cols] = prod.astype(BF16)
        yt_ref[:, n * hw:(n + 1) * hw] += _dot(vt_ref[...], p_ref[:, n * hw:(n + 1) * hw])

    @pl.when(c == nc - 1)
    def _():
        out_ref[...] = x1_ref[...] + gate_ref[...] * yt_ref[...].T


def _peer(h2t, pq, x1, gate, sk, u_bf, vt_blk, *, tm, seq_rows, per_row_mod):
    rows, d = x1.shape
    ne = u_bf.shape[0]
    ce = vt_blk.shape[2]
    nt = rows // tm
    nc = ne // ce
    assert tm % LANES == 0 and rows % tm == 0 and ne % ce == 0 and ce % PEER_KEYS == 0
    tiles_per_seq = max(seq_rows // tm, 1)
    if per_row_mod:
        mod_spec = pl.BlockSpec((tm, d), lambda i, c: (i, 0))
    else:
        mod_spec = pl.BlockSpec((None, 1, d), lambda i, c: (i // tiles_per_seq, 0, 0))
    row = lambda w: pl.BlockSpec((tm, w), lambda i, c: (i, 0))
    sel_shape = (tm // LANES, PEER_HEADS, PEER_KEYS, LANES)
    vreg_shape = (tm // LANES, PEER_HEADS, PEER_KEYS // SUBLANES, SUBLANES, LANES)
    return pl.pallas_call(
        functools.partial(_peer_body, ce, tm, nc),
        grid=(nt, nc),
        in_specs=[pl.BlockSpec((d, tm), lambda i, c: (0, i)), row(pq.shape[1]), row(d), mod_spec,
                  pl.BlockSpec(sk.shape, lambda i, c: (0, 0, 0)),
                  pl.BlockSpec((ce, d), lambda i, c: (c, 0)),
                  pl.BlockSpec((None, d, ce), lambda i, c: (c, 0, 0))],
        out_specs=row(d),
        out_shape=jax.ShapeDtypeStruct((rows, d), F32),
        scratch_shapes=[pltpu.VMEM(sel_shape, F32), pltpu.VMEM(sel_shape, F32), pltpu.VMEM(vreg_shape, F32),
                        pltpu.VMEM((d, tm), F32), pltpu.VMEM((ce, tm), BF16), pltpu.VMEM((ce, tm), F32)],
        compiler_params=_cparams(("arbitrary", "arbitrary")),
    )(h2t, pq, x1, gate, sk, u_bf, vt_blk)


def _rope_tables(pos):
    half = HEAD_DIM // 2
    freqs = ROPE_THETA ** (-jnp.arange(half, dtype=F32) / half)
    ang = pos.astype(F32)[:, None] * freqs[None, :]
    cos, sin = jnp.cos(ang), jnp.sin(ang)
    cos2 = jnp.concatenate([cos, cos, cos, cos], axis=1)
    sin2 = jnp.concatenate([-sin, sin, -sin, sin], axis=1)
    return cos2, sin2


def _pick_tile(rows, pref):
    t = min(pref, rows)
    while rows % t:
        t //= 2
    return t


def kernel(x_prompt, x_sample, cache_k, cache_v, cache_kidx, state_conv, page_table, c_prompt, c_sample, w_ada, b_ada,
           norm1_w, w_in, conv_w, q_norm_w, k_norm_w, w_conv_out, w_attn_out, w_out, norm2_w, peer_w_q, peer_sub_keys,
           peer_u, peer_v):
    bsz, seq, d = x_prompt.shape
    db, ds, _ = x_sample.shape
    depth = w_ada.shape[0]
    dc = conv_w.shape[2]
    hq = N_HEADS * HEAD_DIM
    hk = N_KV_HEADS * HEAD_DIM
    hi_w = IDX_HEADS * IDX_DIM
    n_pages = page_table.shape[1]
    page = cache_k.shape[2]
    past = n_pages * page
    group = N_HEADS // N_KV_HEADS

    pos_p = jnp.arange(seq, dtype=I32)
    pos_s = past + jnp.arange(ds, dtype=I32)
    cos_p, sin_p = _rope_tables(pos_p)
    cos_s, sin_s = _rope_tables(jnp.tile(pos_s, db))
    gsum = jnp.kron(jnp.eye(N_HEADS, dtype=F32), jnp.ones((HEAD_DIM, HEAD_DIM), F32)).astype(BF16)

    xp = x_prompt.reshape(bsz * seq, d)
    xs = x_sample.reshape(db * ds, d)
    outs = {n: [] for n in ("kp", "vp", "kip", "cp", "ks", "vs", "kis", "cs")}

    for layer in range(depth):
        mod = _adaln(jnp.concatenate([c_prompt, c_sample], axis=0), w_ada[layer], b_ada[layer])
        mod_p = [m.reshape(bsz, 1, d) for m in jnp.split(mod[:bsz], N_MOD, axis=-1)]
        mod_s = [jnp.repeat(m, ds, axis=0) for m in jnp.split(mod[bsz:], N_MOD, axis=-1)]

        w = w_in[layer]
        o_ki = 3 * dc + hq + 2 * hk + hi_w
        o_g = o_ki + IDX_DIM + IDX_HEADS
        slab = jnp.concatenate([w[:, o_ki:o_g], jnp.zeros((d, LANES - IDX_DIM - IDX_HEADS), F32)], axis=1)
        w_pad = jnp.concatenate([w[:, :o_ki], slab, w[:, o_g:]], axis=1).astype(BF16)
        n1w = norm1_w[layer].reshape(1, d)
        qnw = jnp.tile(q_norm_w[layer], N_HEADS).reshape(1, hq)
        knw = jnp.tile(k_norm_w[layer], N_HEADS).reshape(1, hq)
        wco = w_conv_out[layer].astype(BF16)
        wao = w_attn_out[layer].astype(BF16)
        wout = w_out[layer].astype(BF16)
        wq = peer_w_q[layer].astype(BF16)
        n2w = norm2_w[layer].reshape(1, d)
        sk = jnp.transpose(peer_sub_keys[layer], (1, 0, 2, 3)).reshape(2 * PEER_HEADS, PEER_KEYS, PEER_QDIM // 2)
        sk = sk.astype(BF16)
        u_bf = peer_u[layer].astype(BF16)
        ce = min(1024, u_bf.shape[0])
        vt_bf = peer_v[layer].astype(BF16).reshape(-1, ce, d).transpose(0, 2, 1)

        tm_p = _pick_tile(seq, 256)
        zero_ext = jnp.zeros((1, dc), F32)
        u, q, k, v, qi, kiw, yag, sgb, vt, wit = _inproj(
            xp, mod_p[0], mod_p[1], n1w, w_pad, cos_p, sin_p, qnw, knw, gsum, conv_w[layer], wco, zero_ext, zero_ext,
            tm=tm_p, seq_rows=seq, per_row_mod=False)
        ki = kiw[:, :IDX_DIM]
        attn = _pattn(q, qi, wit, k, vt, kiw, bsz=bsz, t=seq)
        tm_f = _pick_tile(seq, 512)
        x1, h2t, pq = _finish(xp, yag, sgb, attn, mod_p[2], mod_p[3], mod_p[4], n2w, wao, wout,
                              wq, tm=tm_f, seq_rows=seq, per_row_mod=False)
        xp = _peer(h2t, pq, x1, mod_p[5], sk, u_bf, vt_bf, tm=tm_f, seq_rows=seq, per_row_mod=False)
        outs["kp"].append(k.reshape(bsz, seq, N_KV_HEADS, HEAD_DIM))
        outs["vp"].append(v.reshape(bsz, seq, N_KV_HEADS, HEAD_DIM))
        outs["kip"].append(ki.reshape(bsz, seq, IDX_DIM))
        outs["cp"].append(u.reshape(bsz, seq, dc)[:, seq - 2:, :])

        rows_s = db * ds
        st = state_conv[layer]
        posr = jnp.tile(jnp.arange(ds), db)
        ext1 = jnp.repeat(st[:, 1, :], ds, axis=0)
        ext2 = jnp.where((posr == 0)[:, None], jnp.repeat(st[:, 0, :], ds, axis=0), ext1)
        u, q, k, v, qi, kiw, yag, sgb, _, _ = _inproj(
            xs, mod_s[0], mod_s[1], n1w, w_pad, cos_s, sin_s, qnw, knw, gsum, conv_w[layer], wco, ext1, ext2,
            tm=rows_s, seq_rows=ds, per_row_mod=True)
        ki = kiw[:, :IDX_DIM]
        wi = kiw[:, IDX_DIM:IDX_DIM + IDX_HEADS]
        pad_q = lambda a: jnp.pad(a, ((0, 0), (0, 0), (0, SUBLANES - ds), (0, 0)))
        qi_arr = pad_q(qi.reshape(db, ds, IDX_HEADS, IDX_DIM).transpose(0, 2, 1, 3)).reshape(db, IDX_HEADS * SUBLANES, IDX_DIM)
        w_arr = pad_q(wi.reshape(db, ds, IDX_HEADS, 1).transpose(0, 2, 1, 3)).reshape(db, IDX_HEADS * SUBLANES, 1)
        w_arr = jnp.broadcast_to(w_arr, (db, IDX_HEADS * SUBLANES, LANES))
        q_arr = pad_q(q.reshape(db, ds, N_HEADS, HEAD_DIM).transpose(0, 2, 1, 3)).reshape(db, N_KV_HEADS, group * SUBLANES, HEAD_DIM)
        pad_page = lambda a: jnp.pad(a.reshape(db, ds, -1), ((0, 0), (0, page - ds), (0, 0))).transpose(0, 2, 1)
        cki = jnp.transpose(cache_kidx[layer], (0, 2, 1))
        ck = jnp.transpose(cache_k[layer], (0, 2, 3, 1)).reshape(-1, hk, page)
        cv = jnp.transpose(cache_v[layer], (0, 2, 3, 1)).reshape(-1, hk, page)
        o = _sattn(page_table, qi_arr, w_arr, q_arr, pad_page(ki), pad_page(k), pad_page(v), cki, ck, cv, ds=ds)
        attn = o.reshape(db, N_HEADS, SUBLANES, HEAD_DIM)[:, :, :ds, :].transpose(0, 2, 1, 3).reshape(rows_s, hq)
        x1, h2t, pq = _finish(xs, yag, sgb, attn, mod_s[2], mod_s[3], mod_s[4], n2w, wao, wout, wq,
                              tm=rows_s, seq_rows=ds, per_row_mod=True)
        xs = _peer(h2t, pq, x1, mod_s[5], sk, u_bf, vt_bf, tm=rows_s, seq_rows=ds, per_row_mod=True)
        outs["ks"].append(k.reshape(db, ds, N_KV_HEADS, HEAD_DIM))
        outs["vs"].append(v.reshape(db, ds, N_KV_HEADS, HEAD_DIM))
        outs["kis"].append(ki.reshape(db, ds, IDX_DIM))
        full = jnp.concatenate([st, u.reshape(db, ds, dc)], axis=1)
        outs["cs"].append(full[:, full.shape[1] - 2:, :])

    stack = lambda n: jnp.stack(outs[n])
    return (xp.reshape(bsz, seq, d), xs.reshape(db, ds, d), stack("kp"), stack("vp"), stack("kip"), stack("cp"),
            stack("ks"), stack("vs"), stack("kis"), stack("cs"))
```

```python
import functools

import numpy as np
import jax
import jax.numpy as jnp
from jax import lax
from jax.experimental import pallas as pl
from jax.experimental.pallas import tpu as pltpu

F32 = jnp.float32
BF16 = jnp.bfloat16
I32 = jnp.int32

N_HEADS = 8
N_KV_HEADS = 2
HEAD_DIM = 64
IDX_HEADS = 8
IDX_DIM = 64
IDX_SCALE = (IDX_DIM ** -0.5) * (IDX_HEADS ** -0.5)
TOPK_MAX = 256
Q_BLOCK = 128
ROPE_THETA = 10000.0
PEER_HEADS = 8
PEER_KEYS = 128
PEER_QDIM = 256
PEER_TOPK = 16
N_MOD = 6
NORM_EPS = 1e-6
NEG_INF = -1e30
INT_MIN = -2 ** 31

LANES = 128
SUBLANES = 8
VMEM_LIMIT = 56 * 1024 * 1024


def _cparams(sem):
    return pltpu.CompilerParams(dimension_semantics=sem, vmem_limit_bytes=VMEM_LIMIT)


def _dot(a, b):
    return jnp.dot(a, b, preferred_element_type=F32)


def _dot_nt(a, b):
    return lax.dot_general(a, b, (((1,), (1,)), ((), ())), preferred_element_type=F32)


def _adaln_body(c_ref, w_ref, b_ref, o_ref):
    o_ref[...] = jnp.dot(c_ref[...], w_ref[...], preferred_element_type=F32,
                         precision=lax.Precision.HIGHEST) + b_ref[...]


def _adaln(c, w_ada, b_ada):
    m, d = c.shape
    n = w_ada.shape[1]
    tn = d
    return pl.pallas_call(
        _adaln_body,
        grid=(n // tn,),
        in_specs=[pl.BlockSpec((m, d), lambda j: (0, 0)),
                  pl.BlockSpec((d, tn), lambda j: (0, j)),
                  pl.BlockSpec((1, tn), lambda j: (0, j))],
        out_specs=pl.BlockSpec((m, tn), lambda j: (0, j)),
        out_shape=jax.ShapeDtypeStruct((m, n), F32),
        compiler_params=_cparams(("arbitrary",)),
    )(c, w_ada, b_ada.reshape(1, n))


def _rope_lanes(z, cos, sin_signed):
    w = z.shape[1]
    reps = w // LANES
    cosw = jnp.concatenate([cos] * reps, axis=1) if reps > 1 else cos
    sinw = jnp.concatenate([sin_signed] * reps, axis=1) if reps > 1 else sin_signed
    half = HEAD_DIM // 2
    lane = lax.broadcasted_iota(I32, (1, w), 1)
    first = (lane % HEAD_DIM) < half
    swapped = jnp.where(first, pltpu.roll(z, w - half, 1), pltpu.roll(z, half, 1))
    return z * cosw + swapped * sinw


def _head_rms(z, nw, gsum):
    z2 = z * z
    hi = z2.astype(BF16)
    lo = (z2 - hi.astype(F32)).astype(BF16)
    ssum = _dot(hi, gsum) + _dot(lo, gsum)
    return z * lax.rsqrt(ssum * (1.0 / HEAD_DIM) + NORM_EPS) * nw


def _inproj_body(tm, seq_rows, dc, x_ref, shift_ref, scale_ref, n1w_ref, w_ref, cos_ref, sin_ref, qnw_ref, knw_ref,
                 gsum_ref, convw_ref, wco_ref, ext1_ref, ext2_ref,
                 u_ref, q_ref, k_ref, v_ref, qi_ref, kiw_ref, yag_ref, sgb_ref, kt_ref, vt_ref, wit_ref, carry_ref,
                 buf_ref):
    i = pl.program_id(0)
    hq = N_HEADS * HEAD_DIM
    hk = N_KV_HEADS * HEAD_DIM
    hi_w = IDX_HEADS * IDX_DIM
    d = x_ref.shape[1]
    offs = np.cumsum([0, dc, dc, dc, hq, hk, hk, hi_w, LANES, d, d])

    def proj(s):
        return _dot(hb, w_ref[:, int(offs[s]):int(offs[s + 1])])

    x = x_ref[...]
    ms = jnp.mean(x * x, axis=-1, keepdims=True)
    h = x * lax.rsqrt(ms + NORM_EPS) * n1w_ref[...]
    h = h * (1.0 + scale_ref[...]) + shift_ref[...]
    hb = h.astype(BF16)

    u = proj(2) * proj(0)
    u_ref[...] = u

    @pl.when(i == 0)
    def _():
        carry_ref[...] = jnp.zeros_like(carry_ref)

    buf_ref[0:SUBLANES, :] = carry_ref[...]
    buf_ref[SUBLANES:SUBLANES + tm, :] = u
    carry_ref[...] = u[tm - SUBLANES:tm, :]
    pos = (i * tm + lax.broadcasted_iota(I32, (tm, 1), 0)) % seq_rows
    u1 = jnp.where(pos >= 1, buf_ref[SUBLANES - 1:SUBLANES - 1 + tm, :], ext1_ref[...])
    u2 = jnp.where(pos >= 2, buf_ref[SUBLANES - 2:SUBLANES - 2 + tm, :], ext2_ref[...])
    yc = convw_ref[2:3, :] * u + convw_ref[0:1, :] * u2 + convw_ref[1:2, :] * u1
    a_pre = proj(1) * yc
    ya = _dot(a_pre.astype(BF16), wco_ref[...])
    yag_ref[...] = jax.nn.sigmoid(proj(8)) * ya
    sgb_ref[...] = jax.nn.sigmoid(proj(9))

    cos = cos_ref[...]
    sin = sin_ref[...]
    gsum = gsum_ref[...]
    q_ref[...] = _rope_lanes(_head_rms(proj(3), qnw_ref[...], gsum), cos, sin)
    k = _rope_lanes(_head_rms(proj(4), knw_ref[:, 0:hk], gsum[0:hk, 0:hk]), cos, sin)
    k_ref[...] = k
    kt_ref[...] = k.T
    v = proj(5)
    v_ref[...] = v
    vt_ref[...] = v.T
    qi_ref[...] = _rope_lanes(proj(6), cos, sin)
    slab = proj(7)
    lane = lax.broadcasted_iota(I32, (1, LANES), 1)
    kiw = jnp.where(lane < IDX_DIM, _rope_lanes(slab, cos, sin), slab * IDX_SCALE)
    kiw_ref[...] = kiw
    wit_ref[...] = kiw.T[IDX_DIM:IDX_DIM + IDX_HEADS, :]


def _inproj(x2, shift, scale, n1w, w_pad, cos, sin, qnw, knw, gsum, convw, wco, ext1, ext2, *, tm, seq_rows,
            per_row_mod):
    rows, d = x2.shape
    dc = convw.shape[1]
    hq = N_HEADS * HEAD_DIM
    hk = N_KV_HEADS * HEAD_DIM
    hi_w = IDX_HEADS * IDX_DIM
    nt = rows // tm
    pos_tiles = cos.shape[0] // tm
    tiles_per_seq = max(seq_rows // tm, 1)
    if per_row_mod:
        mod_spec = pl.BlockSpec((tm, d), lambda i: (i, 0))
        ext_spec = pl.BlockSpec((tm, dc), lambda i: (i, 0))
    else:
        mod_spec = pl.BlockSpec((None, 1, d), lambda i: (i // tiles_per_seq, 0, 0))
        ext_spec = pl.BlockSpec((1, dc), lambda i: (0, 0))
    const = lambda shape: pl.BlockSpec(shape, lambda i: tuple(0 for _ in shape))
    row = lambda w: pl.BlockSpec((tm, w), lambda i: (i, 0))
    out_widths = [dc, hq, hk, hk, hi_w, LANES, d, d]
    n_seq = max(rows // (tiles_per_seq * tm), 1)
    kvt_spec = pl.BlockSpec((None, hk, tm), lambda i: (i // tiles_per_seq, 0, i % tiles_per_seq))
    kvt_shape = jax.ShapeDtypeStruct((n_seq, hk, tiles_per_seq * tm), F32)
    return pl.pallas_call(
        functools.partial(_inproj_body, tm, seq_rows, dc),
        grid=(nt,),
        in_specs=[row(d), mod_spec, mod_spec, const((1, d)), const(w_pad.shape),
                  pl.BlockSpec((tm, LANES), lambda i: (i % pos_tiles, 0)),
                  pl.BlockSpec((tm, LANES), lambda i: (i % pos_tiles, 0)),
                  const((1, hq)), const((1, hq)), const((hq, hq)), const(convw.shape), const(wco.shape),
                  ext_spec, ext_spec],
        out_specs=[row(w) for w in out_widths] + [kvt_spec, kvt_spec, pl.BlockSpec((IDX_HEADS, tm), lambda i: (0, i))],
        out_shape=[jax.ShapeDtypeStruct((rows, w), F32) for w in out_widths]
                  + [kvt_shape, kvt_shape, jax.ShapeDtypeStruct((IDX_HEADS, rows), F32)],
        scratch_shapes=[pltpu.VMEM((SUBLANES, dc), F32), pltpu.VMEM((tm + SUBLANES, dc), F32)],
        compiler_params=_cparams(("arbitrary",)),
    )(x2, shift, scale, n1w, w_pad, cos, sin, qnw, knw, gsum, convw, wco, ext1, ext2)


def _sortable_key(score):
    bits = lax.bitcast_convert_type(score, I32)
    key = jnp.where(bits < 0, bits ^ jnp.int32(0x7FFFFFFF), bits)
    return jnp.where(score == 0.0, jnp.int32(0), key)


def _kth_largest_key(count_ge, shape, k):
    ans = jnp.where(count_ge(jnp.zeros(shape, I32)) >= k, jnp.int32(0), jnp.int32(INT_MIN))

    def step(it, ans):
        cand = ans + jnp.left_shift(jnp.int32(1), jnp.int32(30) - it)
        return jnp.where(count_ge(cand) >= k, cand, ans)

    return lax.fori_loop(0, 31, step, ans)


def _tie_cut(count_eq_below, shape, need, nbits):
    def step(it, c):
        cand = c + jnp.left_shift(jnp.int32(1), jnp.int32(nbits - 1) - it)
        return jnp.where(count_eq_below(cand) < need, cand, c)

    return lax.fori_loop(0, nbits, step, jnp.zeros(shape, I32))


def _col_reduce(x, op, rows=64):
    n = x.shape[0]
    if n > rows and n % rows == 0:
        x = op(x.reshape(n // rows, rows, x.shape[1]), axis=0)
    return op(x, axis=0, keepdims=True)


def _pattn_class(tq, klen, topk, j, q_ref, qi_ref, wit_ref, k_ref, vt_ref, kiw_ref, o_ref, key_ref, cut_ref):
    qi = qi_ref[...]
    wit = wit_ref[...]
    ki = kiw_ref[0:klen, 0:IDX_DIM].astype(BF16)
    score = jnp.zeros((klen, tq), F32)
    hpm = 4
    for h0 in range(0, IDX_HEADS, hpm):
        qs = jnp.concatenate([qi[:, h * IDX_DIM:(h + 1) * IDX_DIM] for h in range(h0, h0 + hpm)], axis=0)
        s = _dot_nt(ki, qs.astype(BF16))
        for g in range(hpm):
            score = score + wit[h0 + g:h0 + g + 1, :] * jnp.maximum(s[:, g * tq:(g + 1) * tq], 0.0)
    kpos = lax.broadcasted_iota(I32, (klen, 1), 0)
    qpos = j * tq + lax.broadcasted_iota(I32, (1, tq), 1)
    causal = kpos <= qpos
    score = jnp.where(causal, score, NEG_INF)
    key_ref[0:klen, :] = _sortable_key(score)

    def count_ge(c):
        return _col_reduce((key_ref[0:klen, :] >= c).astype(F32), jnp.sum)

    thr = _kth_largest_key(count_ge, (1, tq), topk)
    key = key_ref[0:klen, :]
    gt = key > thr
    eq = key == thr
    need = topk - _col_reduce(gt.astype(F32), jnp.sum)
    n_eq = _col_reduce(eq.astype(F32), jnp.sum)
    cut_ref[...] = jnp.full(cut_ref.shape, klen, I32)

    @pl.when(jnp.max(n_eq - need) > 0)
    def _():
        def count_eq_below(c):
            hit = (key_ref[0:klen, :] == thr) & (kpos < c)
            return _col_reduce(hit.astype(F32), jnp.sum)

        cut = _tie_cut(count_eq_below, (1, tq), need, int(klen - 1).bit_length())
        cut_ref[...] = jnp.broadcast_to(cut, cut_ref.shape)

    cut = cut_ref[0:1, :]
    bias = jnp.where((gt | (eq & (kpos <= cut))) & causal, 0.0, NEG_INF)

    q = q_ref[...] * (HEAD_DIM ** -0.5)
    k = k_ref[0:klen, :].astype(BF16)
    group = N_HEADS // N_KV_HEADS
    outs = []
    for h in range(N_HEADS):
        n = h // group
        if h % group == 0:
            qs = jnp.concatenate([q[:, g * HEAD_DIM:(g + 1) * HEAD_DIM] for g in range(h, h + group)], axis=0)
            logits_n = _dot_nt(k[:, n * HEAD_DIM:(n + 1) * HEAD_DIM], qs.astype(BF16))
        logits = logits_n[:, (h % group) * tq:(h % group + 1) * tq] + bias
        m = _col_reduce(logits, jnp.max)
        p = jnp.exp(logits - m)
        l = _col_reduce(p, jnp.sum)
        vt = vt_ref[n * HEAD_DIM:(n + 1) * HEAD_DIM, 0:klen].astype(BF16)
        outs.append(_dot(vt, p.astype(BF16)) / l)
    o_ref[...] = jnp.concatenate(outs, axis=0).T


def _pattn_body(tq, t, topk, n_cls, q_ref, qi_ref, wit_ref, k_ref, vt_ref, kiw_ref, o_ref, key_ref, cut_ref):
    j = pl.program_id(1)
    blocks_per_cls = (t // tq) // n_cls
    for c in range(n_cls):
        @pl.when((j >= c * blocks_per_cls) & (j < (c + 1) * blocks_per_cls))
        def _(c=c):
            _pattn_class(tq, (c + 1) * blocks_per_cls * tq, topk, j, q_ref, qi_ref, wit_ref, k_ref, vt_ref, kiw_ref,
                         o_ref, key_ref, cut_ref)


def _pattn(q, qi, wit, k, vt, kiw, *, bsz, t):
    hq = q.shape[1]
    tq = Q_BLOCK
    nqb = t // tq
    n_cls = max(c for c in (1, 2, 4, 8) if nqb % c == 0)
    topk = min(TOPK_MAX, t // 4)
    qspec = lambda w: pl.BlockSpec((tq, w), lambda bi, j: (bi * nqb + j, 0))
    kspec = lambda w: pl.BlockSpec((t, w), lambda bi, j: (bi, 0))
    return pl.pallas_call(
        functools.partial(_pattn_body, tq, t, topk, n_cls),
        grid=(bsz, nqb),
        in_specs=[qspec(hq), qspec(qi.shape[1]),
                  pl.BlockSpec((wit.shape[0], tq), lambda bi, j: (0, bi * nqb + j)),
                  kspec(k.shape[1]),
                  pl.BlockSpec((None, vt.shape[1], t), lambda bi, j: (bi, 0, 0)),
                  kspec(kiw.shape[1])],
        out_specs=qspec(hq),
        out_shape=jax.ShapeDtypeStruct((bsz * t, hq), F32),
        scratch_shapes=[pltpu.VMEM((t, tq), I32), pltpu.VMEM((SUBLANES, tq), I32)],
        compiler_params=_cparams(("arbitrary", "arbitrary")),
    )(q, qi, wit, k, vt, kiw)


def _sattn_body(n_pages, g, ds, topk, pt_ref, qi_ref, w_ref, q_ref, kin_ref, kn_ref, vn_ref, *rest):
    kic, kc, vc = rest[0:g], rest[g:2 * g], rest[2 * g:3 * g]
    o_ref, score_ref, mask_ref, m_ref, l_ref, acc_ref, kicat_ref, kcat_ref, vcat_ref = rest[3 * g:]
    ph = pl.program_id(1)
    p = pl.program_id(2)
    n_steps = n_pages // g
    group = N_HEADS // N_KV_HEADS
    sub = lax.broadcasted_iota(I32, (SUBLANES, LANES), 0)
    lane = lax.broadcasted_iota(I32, (SUBLANES, LANES), 1)
    new_valid = (lane <= sub) & (lane < ds)

    def scores(ki_bf, npg):
        s = _dot(qi_ref[...].astype(BF16), ki_bf)
        w = w_ref[...]
        w = jnp.concatenate([w] * npg, axis=1) if npg > 1 else w
        acc = jnp.zeros((SUBLANES, npg * LANES), F32)
        for h in range(IDX_HEADS):
            r = slice(h * SUBLANES, (h + 1) * SUBLANES)
            acc = acc + w[r, :] * jnp.maximum(s[r, :], 0.0)
        return acc

    @pl.when(ph == 0)
    def _():
        for gi in range(g):
            kicat_ref[:, gi * LANES:(gi + 1) * LANES] = kic[gi][...].astype(BF16)
        sc = scores(kicat_ref[...], g)
        for gi in range(g):
            score_ref[p * g + gi] = sc[:, gi * LANES:(gi + 1) * LANES]

        @pl.when(p == 0)
        def _():
            score_ref[n_pages] = jnp.where(new_valid, scores(kin_ref[...].astype(BF16), 1), NEG_INF)

    def attend(mask8, kb, vb):
        maskg = jnp.concatenate([mask8] * group, axis=0) > 0.5
        for n in range(N_KV_HEADS):
            qn = (q_ref[n] * (HEAD_DIM ** -0.5)).astype(BF16)
            logits = _dot(qn, kb[n * HEAD_DIM:(n + 1) * HEAD_DIM, :])
            logits = jnp.where(maskg, logits, NEG_INF)
            m_old = m_ref[n][:, 0:1]
            m_new = jnp.maximum(m_old, jnp.max(logits, axis=1, keepdims=True))
            pr = jnp.where(maskg, jnp.exp(logits - m_new), 0.0)
            alpha = jnp.exp(m_old - m_new)
            l_new = alpha * l_ref[n][:, 0:1] + jnp.sum(pr, axis=1, keepdims=True)
            acc_ref[n] = alpha * acc_ref[n] + _dot_nt(pr.astype(BF16), vb[n * HEAD_DIM:(n + 1) * HEAD_DIM, :])
            m_ref[n] = jnp.broadcast_to(m_new, m_ref.shape[1:])
            l_ref[n] = jnp.broadcast_to(l_new, l_ref.shape[1:])

    @pl.when(ph == 1)
    def _():
        @pl.when(p == 0)
        def _():
            key = _sortable_key(score_ref[...])
            slab = lax.broadcasted_iota(I32, key.shape, 0)
            idx = slab * LANES + lax.broadcasted_iota(I32, key.shape, 2)

            def count(hit):
                x = hit.astype(F32)
                parts = [jnp.sum(x[s:s + 16], axis=0) for s in range(0, x.shape[0], 16)]
                c = functools.reduce(lambda u, v: u + v, parts)
                return jnp.sum(c, axis=1, keepdims=True)[None]

            thr = _kth_largest_key(lambda c: count(key >= c), (1, SUBLANES, 1), topk)
            gt = key > thr
            eq = key == thr
            need = topk - count(gt)
            valid = (slab < n_pages) | new_valid[None]
            mask_ref[...] = ((gt | eq) & valid).astype(F32)

            real_row = lax.broadcasted_iota(I32, (1, SUBLANES, 1), 1) < ds

            @pl.when(jnp.max(jnp.where(real_row, count(eq) - need, 0.0)) > 0)
            def _():
                nbits = int((n_pages + 1) * LANES - 1).bit_length()
                cut = _tie_cut(lambda c: count(eq & (idx < c)), (1, SUBLANES, 1), need, nbits)
                mask_ref[...] = ((gt | (eq & (idx <= cut))) & valid).astype(F32)

            m_ref[...] = jnp.full(m_ref.shape, NEG_INF, F32)
            l_ref[...] = jnp.zeros(l_ref.shape, F32)
            acc_ref[...] = jnp.zeros(acc_ref.shape, F32)
            attend(mask_ref[n_pages], kn_ref[...].astype(BF16), vn_ref[...].astype(BF16))

        for gi in range(g):
            kcat_ref[:, gi * LANES:(gi + 1) * LANES] = kc[gi][...].astype(BF16)
            vcat_ref[:, gi * LANES:(gi + 1) * LANES] = vc[gi][...].astype(BF16)
        mask = jnp.concatenate([mask_ref[p * g + gi] for gi in range(g)], axis=1)
        attend(mask, kcat_ref[...], vcat_ref[...])

        @pl.when(p == n_steps - 1)
        def _():
            o_ref[...] = acc_ref[...] / l_ref[...][:, :, 0:HEAD_DIM]


def _sattn(page_table, qi_arr, w_arr, q_arr, kin, kn, vn, cache_kidx, cache_k, cache_v, *, ds):
    db, n_pages = page_table.shape
    page = cache_k.shape[2]
    assert page == LANES
    g = max(c for c in (1, 2, 4, 8, 16, 32, 64) if n_pages % c == 0)
    n_steps = n_pages // g
    topk = min(TOPK_MAX, (n_pages * page + ds) // 4)
    group = N_HEADS // N_KV_HEADS
    hk = N_KV_HEADS * HEAD_DIM
    rows_i = IDX_HEADS * SUBLANES
    rows_q = group * SUBLANES
    pt = page_table.reshape(-1)
    per_b = lambda *shape: pl.BlockSpec((None,) + shape, lambda b, ph, p, pt: (b,) + tuple(0 for _ in shape))

    def kic_map(gi, b, ph, p, pt):
        return (pt[b * n_pages + (p * (1 - ph) + (n_steps - 1) * ph) * g + gi], 0, 0)

    def kvc_map(gi, b, ph, p, pt):
        return (pt[b * n_pages + p * ph * g + gi], 0, 0)

    kic_specs = [pl.BlockSpec((None, IDX_DIM, page), functools.partial(kic_map, gi)) for gi in range(g)]
    kvc_specs = [pl.BlockSpec((None, hk, page), functools.partial(kvc_map, gi)) for gi in range(g)]
    grid_spec = pltpu.PrefetchScalarGridSpec(
        num_scalar_prefetch=1,
        grid=(db, 2, n_steps),
        in_specs=[per_b(rows_i, IDX_DIM), per_b(rows_i, LANES), per_b(N_KV_HEADS, rows_q, HEAD_DIM),
                  per_b(IDX_DIM, page), per_b(hk, page), per_b(hk, page)] + kic_specs + kvc_specs + kvc_specs,
        out_specs=per_b(N_KV_HEADS, rows_q, HEAD_DIM),
        scratch_shapes=[pltpu.VMEM((n_pages + 1, SUBLANES, LANES), F32),
                        pltpu.VMEM((n_pages + 1, SUBLANES, LANES), F32),
                        pltpu.VMEM((N_KV_HEADS, rows_q, LANES), F32),
                        pltpu.VMEM((N_KV_HEADS, rows_q, LANES), F32),
                        pltpu.VMEM((N_KV_HEADS, rows_q, HEAD_DIM), F32),
                        pltpu.VMEM((IDX_DIM, g * page), BF16),
                        pltpu.VMEM((hk, g * page), BF16),
                        pltpu.VMEM((hk, g * page), BF16)],
    )
    return pl.pallas_call(
        functools.partial(_sattn_body, n_pages, g, ds, topk),
        grid_spec=grid_spec,
        out_shape=jax.ShapeDtypeStruct((db, N_KV_HEADS, rows_q, HEAD_DIM), F32),
        compiler_params=_cparams(("arbitrary", "arbitrary", "arbitrary")),
    )(pt, qi_arr, w_arr, q_arr, kin, kn, vn, *([cache_kidx] * g), *([cache_k] * g), *([cache_v] * g))


def _finish_body(x_ref, yag_ref, sgb_ref, attn_ref, gate_ref, shift_ref, scale_ref, n2w_ref, wao_ref, wout_ref, wq_ref,
                 x1_ref, h2t_ref, pq_ref):
    yb = _dot(attn_ref[...].astype(BF16), wao_ref[...])
    mix = yag_ref[...] + sgb_ref[...] * yb
    x1 = x_ref[...] + gate_ref[...] * _dot(mix.astype(BF16), wout_ref[...])
    x1_ref[...] = x1
    ms = jnp.mean(x1 * x1, axis=-1, keepdims=True)
    h2 = x1 * lax.rsqrt(ms + NORM_EPS) * n2w_ref[...]
    h2 = h2 * (1.0 + scale_ref[...]) + shift_ref[...]
    h2t_ref[...] = h2.T.astype(BF16)
    pq_ref[...] = _dot(h2.astype(BF16), wq_ref[...]).astype(BF16)


def _finish(x2, yag, sgb, attn, gate, shift, scale, n2w, wao, wout, wq, *, tm, seq_rows, per_row_mod):
    rows, d = x2.shape
    nt = rows // tm
    tiles_per_seq = max(seq_rows // tm, 1)
    if per_row_mod:
        mod_spec = pl.BlockSpec((tm, d), lambda i: (i, 0))
    else:
        mod_spec = pl.BlockSpec((None, 1, d), lambda i: (i // tiles_per_seq, 0, 0))
    const = lambda shape: pl.BlockSpec(shape, lambda i: tuple(0 for _ in shape))
    row = lambda w: pl.BlockSpec((tm, w), lambda i: (i, 0))
    nq = wq.shape[1]
    return pl.pallas_call(
        _finish_body,
        grid=(nt,),
        in_specs=[row(d), row(d), row(d), row(attn.shape[1]), mod_spec, mod_spec, mod_spec, const((1, d)),
                  const(wao.shape), const(wout.shape), const(wq.shape)],
        out_specs=[row(d), pl.BlockSpec((d, tm), lambda i: (0, i)), row(nq)],
        out_shape=[jax.ShapeDtypeStruct((rows, d), F32), jax.ShapeDtypeStruct((d, rows), BF16),
                   jax.ShapeDtypeStruct((rows, nq), BF16)],
        compiler_params=_cparams(("arbitrary",)),
    )(x2, yag, sgb, attn, gate, shift, scale, n2w, wao, wout, wq)


def _top_sorted(x, n):
    vals = []
    for _ in range(n):
        m = jnp.max(x, axis=0, keepdims=True)
        vals.append(m)
        x = jnp.where(x >= m, -jnp.inf, x)
    return jnp.concatenate(vals, axis=0)


def _gelu_tanh(x):
    c = np.float32(np.sqrt(2.0 / np.pi))
    return x * (0.5 * (1.0 + jnp.tanh(c * (x + 0.044715 * (x * x * x)))))


def _peer_select(lg, pq_ref, sk_ref, ct_ref, ea_ref, eb_ref):
    half = PEER_QDIM // 2
    rows = pl.ds(pl.multiple_of(lg * LANES, LANES), LANES)
    for h in range(PEER_HEADS):
        sa = _dot_nt(sk_ref[2 * h], pq_ref[rows, (2 * h) * half:(2 * h + 1) * half])
        sb = _dot_nt(sk_ref[2 * h + 1], pq_ref[rows, (2 * h + 1) * half:(2 * h + 2) * half])
        ta = _top_sorted(sa, PEER_TOPK + 1)
        tb = _top_sorted(sb, PEER_TOPK + 1)
        tail = jnp.concatenate([ta[0:1] + tb[16:17], ta[16:17] + tb[0:1],
                                jnp.full((SUBLANES - 2, LANES), -jnp.inf, F32)], axis=0)
        cand = jnp.concatenate(
            [ta[0:1] + tb[0:16]] + [ta[r:r + 1] + tb[0:8] for r in range(1, 8)] + [ta[8:16] + tb[0:1], tail], axis=0)
        top = _top_sorted(cand, PEER_TOPK + 1)
        thr = 0.5 * (top[PEER_TOPK - 1:PEER_TOPK] + top[PEER_TOPK:PEER_TOPK + 1])
        z = jnp.sum(jnp.exp(top[0:PEER_TOPK] - top[0:1]), axis=0, keepdims=True)
        ct_ref[lg, h] = jnp.exp((thr - tb[0:1]) - sa) / z
        ea_ref[lg, h] = jnp.exp(sa - ta[0:1])
        eb_ref[lg, h] = (jnp.exp(sb - tb[0:1]) / z).reshape(eb_ref.shape[2:])


def _peer_body(ce, tm, nc, h2t_ref, pq_ref, x1_ref, gate_ref, sk_ref, u_ref, vt_ref, out_ref,
               ct_ref, ea_ref, eb_ref, yt_ref, p_ref, act_ref):
    c = pl.program_id(1)
    kk = PEER_KEYS
    n_lg = tm // LANES
    n_sub = ce // kk
    pair = 2 if n_sub % 2 == 0 else 1
    jrows = kk // 2
    n_half = 2 if n_lg % 2 == 0 else 1
    hw = tm // n_half

    @pl.when(c == 0)
    def _():
        yt_ref[...] = jnp.zeros_like(yt_ref)

        def body(lg, carry):
            _peer_select(lg, pq_ref, sk_ref, ct_ref, ea_ref, eb_ref)
            return carry

        lax.fori_loop(0, n_lg, body, 0)

    def expert_act(n):
        act_ref[:, n * hw:(n + 1) * hw] = _dot(u_ref[...], h2t_ref[:, n * hw:(n + 1) * hw])

    expert_act(0)
    for n in range(n_half):
        for lg in range(n * n_lg // n_half, (n + 1) * n_lg // n_half):
            if n + 1 < n_half and lg == (n + 1) * n_lg // n_half - 1:
                expert_act(n + 1)
            cols = slice(lg * LANES, (lg + 1) * LANES)
            for ip in range(n_sub // pair):
                subs = [pair * ip + t for t in range(pair)]
                for jh in range(2):
                    vrows = slice(jh * jrows // SUBLANES, (jh + 1) * jrows // SUBLANES)
                    w = [None] * pair
                    for h in range(PEER_HEADS):
                        ebh = eb_ref[lg, h, vrows]
                        for t, s in enumerate(subs):
                            i = c * n_sub + s
                            ctv = jnp.broadcast_to(ct_ref[lg, h, pl.ds(i, 1), :], (SUBLANES, LANES))[None]
                            eav = jnp.broadcast_to(ea_ref[lg, h, pl.ds(i, 1), :], (SUBLANES, LANES))[None]
                            term = jnp.where(ebh >= ctv, ebh, 0.0) * eav
                            w[t] = term if w[t] is None else w[t] + term
                    for t, s in enumerate(subs):
                        rows = slice(s * kk + jh * jrows, s * kk + (jh + 1) * jrows)
                        prod = w[t].reshape(jrows, LANES) * _gelu_tanh(act_ref[rows, cols])
                        p_ref[rows, cols] = prod.astype(BF16)
        yt_ref[:, n * hw:(n + 1) * hw] += _dot(vt_ref[...], p_ref[:, n * hw:(n + 1) * hw])

    @pl.when(c == nc - 1)
    def _():
        out_ref[...] = x1_ref[...] + gate_ref[...] * yt_ref[...].T


def _peer(h2t, pq, x1, gate, sk, u_bf, vt_blk, *, tm, seq_rows, per_row_mod):
    rows, d = x1.shape
    ne = u_bf.shape[0]
    ce = vt_blk.shape[2]
    nt = rows // tm
    nc = ne // ce
    assert tm % LANES == 0 and rows % tm == 0 and ne % ce == 0 and ce % PEER_KEYS == 0
    tiles_per_seq = max(seq_rows // tm, 1)
    if per_row_mod:
        mod_spec = pl.BlockSpec((tm, d), lambda i, c: (i, 0))
    else:
        mod_spec = pl.BlockSpec((None, 1, d), lambda i, c: (i // tiles_per_seq, 0, 0))
    row = lambda w: pl.BlockSpec((tm, w), lambda i, c: (i, 0))
    sel_shape = (tm // LANES, PEER_HEADS, PEER_KEYS, LANES)
    vreg_shape = (tm // LANES, PEER_HEADS, PEER_KEYS // SUBLANES, SUBLANES, LANES)
    return pl.pallas_call(
        functools.partial(_peer_body, ce, tm, nc),
        grid=(nt, nc),
        in_specs=[pl.BlockSpec((d, tm), lambda i, c: (0, i)), row(pq.shape[1]), row(d), mod_spec,
                  pl.BlockSpec(sk.shape, lambda i, c: (0, 0, 0)),
                  pl.BlockSpec((ce, d), lambda i, c: (c, 0)),
                  pl.BlockSpec((None, d, ce), lambda i, c: (c, 0, 0))],
        out_specs=row(d),
        out_shape=jax.ShapeDtypeStruct((rows, d), F32),
        scratch_shapes=[pltpu.VMEM(sel_shape, F32), pltpu.VMEM(sel_shape, F32), pltpu.VMEM(vreg_shape, F32),
                        pltpu.VMEM((d, tm), F32), pltpu.VMEM((ce, tm), BF16), pltpu.VMEM((ce, tm), F32)],
        compiler_params=_cparams(("arbitrary", "arbitrary")),
    )(h2t, pq, x1, gate, sk, u_bf, vt_blk)


def _rope_tables(pos):
    half = HEAD_DIM // 2
    freqs = ROPE_THETA ** (-jnp.arange(half, dtype=F32) / half)
    ang = pos.astype(F32)[:, None] * freqs[None, :]
    cos, sin = jnp.cos(ang), jnp.sin(ang)
    cos2 = jnp.concatenate([cos, cos, cos, cos], axis=1)
    sin2 = jnp.concatenate([-sin, sin, -sin, sin], axis=1)
    return cos2, sin2


def _pick_tile(rows, pref):
    t = min(pref, rows)
    while rows % t:
        t //= 2
    return t


def kernel(x_prompt, x_sample, cache_k, cache_v, cache_kidx, state_conv, page_table, c_prompt, c_sample, w_ada, b_ada,
           norm1_w, w_in, conv_w, q_norm_w, k_norm_w, w_conv_out, w_attn_out, w_out, norm2_w, peer_w_q, peer_sub_keys,
           peer_u, peer_v):
    bsz, seq, d = x_prompt.shape
    db, ds, _ = x_sample.shape
    depth = w_ada.shape[0]
    dc = conv_w.shape[2]
    hq = N_HEADS * HEAD_DIM
    hk = N_KV_HEADS * HEAD_DIM
    hi_w = IDX_HEADS * IDX_DIM
    n_pages = page_table.shape[1]
    page = cache_k.shape[2]
    past = n_pages * page
    group = N_HEADS // N_KV_HEADS

    pos_p = jnp.arange(seq, dtype=I32)
    pos_s = past + jnp.arange(ds, dtype=I32)
    cos_p, sin_p = _rope_tables(pos_p)
    cos_s, sin_s = _rope_tables(jnp.tile(pos_s, db))
    gsum = jnp.kron(jnp.eye(N_HEADS, dtype=F32), jnp.ones((HEAD_DIM, HEAD_DIM), F32)).astype(BF16)

    xp = x_prompt.reshape(bsz * seq, d)
    xs = x_sample.reshape(db * ds, d)
    outs = {n: [] for n in ("kp", "vp", "kip", "cp", "ks", "vs", "kis", "cs")}

    for layer in range(depth):
        mod = _adaln(jnp.concatenate([c_prompt, c_sample], axis=0), w_ada[layer], b_ada[layer])
        mod_p = [m.reshape(bsz, 1, d) for m in jnp.split(mod[:bsz], N_MOD, axis=-1)]
        mod_s = [jnp.repeat(m, ds, axis=0) for m in jnp.split(mod[bsz:], N_MOD, axis=-1)]

        w = w_in[layer]
        o_ki = 3 * dc + hq + 2 * hk + hi_w
        o_g = o_ki + IDX_DIM + IDX_HEADS
        slab = jnp.concatenate([w[:, o_ki:o_g], jnp.zeros((d, LANES - IDX_DIM - IDX_HEADS), F32)], axis=1)
        w_pad = jnp.concatenate([w[:, :o_ki], slab, w[:, o_g:]], axis=1).astype(BF16)
        n1w = norm1_w[layer].reshape(1, d)
        qnw = jnp.tile(q_norm_w[layer], N_HEADS).reshape(1, hq)
        knw = jnp.tile(k_norm_w[layer], N_HEADS).reshape(1, hq)
        wco = w_conv_out[layer].astype(BF16)
        wao = w_attn_out[layer].astype(BF16)
        wout = w_out[layer].astype(BF16)
        wq = peer_w_q[layer].astype(BF16)
        n2w = norm2_w[layer].reshape(1, d)
        sk = jnp.transpose(peer_sub_keys[layer], (1, 0, 2, 3)).reshape(2 * PEER_HEADS, PEER_KEYS, PEER_QDIM // 2)
        sk = sk.astype(BF16)
        u_bf = peer_u[layer].astype(BF16)
        ce = min(1024, u_bf.shape[0])
        vt_bf = peer_v[layer].astype(BF16).reshape(-1, ce, d).transpose(0, 2, 1)

        tm_p = _pick_tile(seq, 256)
        zero_ext = jnp.zeros((1, dc), F32)
        u, q, k, v, qi, kiw, yag, sgb, kt, vt, wit = _inproj(
            xp, mod_p[0], mod_p[1], n1w, w_pad, cos_p, sin_p, qnw, knw, gsum, conv_w[layer], wco, zero_ext, zero_ext,
            tm=tm_p, seq_rows=seq, per_row_mod=False)
        ki = kiw[:, :IDX_DIM]
        attn = _pattn(q, qi, wit, k, vt, kiw, bsz=bsz, t=seq)
        tm_f = _pick_tile(seq, 512)
        x1, h2t, pq = _finish(xp, yag, sgb, attn, mod_p[2], mod_p[3], mod_p[4], n2w, wao, wout,
                              wq, tm=tm_f, seq_rows=seq, per_row_mod=False)
        xp = _peer(h2t, pq, x1, mod_p[5], sk, u_bf, vt_bf, tm=tm_f, seq_rows=seq, per_row_mod=False)
        outs["kp"].append(kt.reshape(bsz, N_KV_HEADS, HEAD_DIM, seq).transpose(0, 3, 1, 2))
        outs["vp"].append(vt.reshape(bsz, N_KV_HEADS, HEAD_DIM, seq).transpose(0, 3, 1, 2))
        outs["kip"].append(ki.reshape(bsz, seq, IDX_DIM))
        outs["cp"].append(u.reshape(bsz, seq, dc)[:, seq - 2:, :])

        rows_s = db * ds
        st = state_conv[layer]
        posr = jnp.tile(jnp.arange(ds), db)
        ext1 = jnp.repeat(st[:, 1, :], ds, axis=0)
        ext2 = jnp.where((posr == 0)[:, None], jnp.repeat(st[:, 0, :], ds, axis=0), ext1)
        u, q, k, v, qi, kiw, yag, sgb, _, _, _ = _inproj(
            xs, mod_s[0], mod_s[1], n1w, w_pad, cos_s, sin_s, qnw, knw, gsum, conv_w[layer], wco, ext1, ext2,
            tm=rows_s, seq_rows=ds, per_row_mod=True)
        ki = kiw[:, :IDX_DIM]
        wi = kiw[:, IDX_DIM:IDX_DIM + IDX_HEADS]
        pad_q = lambda a: jnp.pad(a, ((0, 0), (0, 0), (0, SUBLANES - ds), (0, 0)))
        qi_arr = pad_q(qi.reshape(db, ds, IDX_HEADS, IDX_DIM).transpose(0, 2, 1, 3)).reshape(db, IDX_HEADS * SUBLANES, IDX_DIM)
        w_arr = pad_q(wi.reshape(db, ds, IDX_HEADS, 1).transpose(0, 2, 1, 3)).reshape(db, IDX_HEADS * SUBLANES, 1)
        w_arr = jnp.broadcast_to(w_arr, (db, IDX_HEADS * SUBLANES, LANES))
        q_arr = pad_q(q.reshape(db, ds, N_HEADS, HEAD_DIM).transpose(0, 2, 1, 3)).reshape(db, N_KV_HEADS, group * SUBLANES, HEAD_DIM)
        pad_page = lambda a: jnp.pad(a.reshape(db, ds, -1), ((0, 0), (0, page - ds), (0, 0))).transpose(0, 2, 1)
        cki = jnp.transpose(cache_kidx[layer], (0, 2, 1))
        ck = jnp.transpose(cache_k[layer], (0, 2, 3, 1)).reshape(-1, hk, page)
        cv = jnp.transpose(cache_v[layer], (0, 2, 3, 1)).reshape(-1, hk, page)
        o = _sattn(page_table, qi_arr, w_arr, q_arr, pad_page(ki), pad_page(k), pad_page(v), cki, ck, cv, ds=ds)
        attn = o.reshape(db, N_HEADS, SUBLANES, HEAD_DIM)[:, :, :ds, :].transpose(0, 2, 1, 3).reshape(rows_s, hq)
        x1, h2t, pq = _finish(xs, yag, sgb, attn, mod_s[2], mod_s[3], mod_s[4], n2w, wao, wout, wq,
                              tm=rows_s, seq_rows=ds, per_row_mod=True)
        xs = _peer(h2t, pq, x1, mod_s[5], sk, u_bf, vt_bf, tm=rows_s, seq_rows=ds, per_row_mod=True)
        outs["ks"].append(k.reshape(db, ds, N_KV_HEADS, HEAD_DIM))
        outs["vs"].append(v.reshape(db, ds, N_KV_HEADS, HEAD_DIM))
        outs["kis"].append(ki.reshape(db, ds, IDX_DIM))
        full = jnp.concatenate([st, u.reshape(db, ds, dc)], axis=1)
        outs["cs"].append(full[:, full.shape[1] - 2:, :])

    stack = lambda n: jnp.stack(outs[n])
    return (xp.reshape(bsz, seq, d), xs.reshape(db, ds, d), stack("kp"), stack("vp"), stack("kip"), stack("cp"),
            stack("ks"), stack("vs"), stack("kis"), stack("cs"))
```

```python
import functools

import numpy as np
import jax
import jax.numpy as jnp
from jax import lax
from jax.experimental import pallas as pl
from jax.experimental.pallas import tpu as pltpu

F32 = jnp.float32
BF16 = jnp.bfloat16
I32 = jnp.int32

N_HEADS = 8
N_KV_HEADS = 2
HEAD_DIM = 64
IDX_HEADS = 8
IDX_DIM = 64
IDX_SCALE = (IDX_DIM ** -0.5) * (IDX_HEADS ** -0.5)
TOPK_MAX = 256
Q_BLOCK = 128
ROPE_THETA = 10000.0
PEER_HEADS = 8
PEER_KEYS = 128
PEER_QDIM = 256
PEER_TOPK = 16
N_MOD = 6
NORM_EPS = 1e-6
NEG_INF = -1e30
INT_MIN = -2 ** 31

LANES = 128
SUBLANES = 8
VMEM_LIMIT = 56 * 1024 * 1024


def _cparams(sem):
    return pltpu.CompilerParams(dimension_semantics=sem, vmem_limit_bytes=VMEM_LIMIT)


def _dot(a, b):
    return jnp.dot(a, b, preferred_element_type=F32)


def _dot_nt(a, b):
    return lax.dot_general(a, b, (((1,), (1,)), ((), ())), preferred_element_type=F32)


def _adaln_body(c_ref, w_ref, b_ref, o_ref):
    o_ref[...] = jnp.dot(c_ref[...], w_ref[...], preferred_element_type=F32,
                         precision=lax.Precision.HIGHEST) + b_ref[...]


def _adaln(c, w_ada, b_ada):
    m, d = c.shape
    n = w_ada.shape[1]
    tn = d
    return pl.pallas_call(
        _adaln_body,
        grid=(n // tn,),
        in_specs=[pl.BlockSpec((m, d), lambda j: (0, 0)),
                  pl.BlockSpec((d, tn), lambda j: (0, j)),
                  pl.BlockSpec((1, tn), lambda j: (0, j))],
        out_specs=pl.BlockSpec((m, tn), lambda j: (0, j)),
        out_shape=jax.ShapeDtypeStruct((m, n), F32),
        compiler_params=_cparams(("arbitrary",)),
    )(c, w_ada, b_ada.reshape(1, n))


def _rope_lanes(z, cos, sin_signed):
    w = z.shape[1]
    reps = w // LANES
    cosw = jnp.concatenate([cos] * reps, axis=1) if reps > 1 else cos
    sinw = jnp.concatenate([sin_signed] * reps, axis=1) if reps > 1 else sin_signed
    half = HEAD_DIM // 2
    lane = lax.broadcasted_iota(I32, (1, w), 1)
    first = (lane % HEAD_DIM) < half
    swapped = jnp.where(first, pltpu.roll(z, w - half, 1), pltpu.roll(z, half, 1))
    return z * cosw + swapped * sinw


def _head_rms(z, nw, gsum):
    z2 = z * z
    hi = z2.astype(BF16)
    lo = (z2 - hi.astype(F32)).astype(BF16)
    ssum = _dot(hi, gsum) + _dot(lo, gsum)
    return z * lax.rsqrt(ssum * (1.0 / HEAD_DIM) + NORM_EPS) * nw


def _inproj_body(tm, seq_rows, dc, x_ref, shift_ref, scale_ref, n1w_ref, w_ref, cos_ref, sin_ref, qnw_ref, knw_ref,
                 gsum_ref, convw_ref, wco_ref, ext1_ref, ext2_ref,
                 u_ref, q_ref, k_ref, v_ref, qi_ref, kiw_ref, yag_ref, sgb_ref, kt_ref, vt_ref, wit_ref, carry_ref,
                 buf_ref):
    i = pl.program_id(0)
    hq = N_HEADS * HEAD_DIM
    hk = N_KV_HEADS * HEAD_DIM
    hi_w = IDX_HEADS * IDX_DIM
    d = x_ref.shape[1]
    offs = np.cumsum([0, dc, dc, dc, hq, hk, hk, hi_w, LANES, d, d])

    def proj(s):
        return _dot(hb, w_ref[:, int(offs[s]):int(offs[s + 1])])

    x = x_ref[...]
    ms = jnp.mean(x * x, axis=-1, keepdims=True)
    h = x * lax.rsqrt(ms + NORM_EPS) * n1w_ref[...]
    h = h * (1.0 + scale_ref[...]) + shift_ref[...]
    hb = h.astype(BF16)

    u = proj(2) * proj(0)
    u_ref[...] = u

    @pl.when(i == 0)
    def _():
        carry_ref[...] = jnp.zeros_like(carry_ref)

    buf_ref[0:SUBLANES, :] = carry_ref[...]
    buf_ref[SUBLANES:SUBLANES + tm, :] = u
    carry_ref[...] = u[tm - SUBLANES:tm, :]
    pos = (i * tm + lax.broadcasted_iota(I32, (tm, 1), 0)) % seq_rows
    u1 = jnp.where(pos >= 1, buf_ref[SUBLANES - 1:SUBLANES - 1 + tm, :], ext1_ref[...])
    u2 = jnp.where(pos >= 2, buf_ref[SUBLANES - 2:SUBLANES - 2 + tm, :], ext2_ref[...])
    yc = convw_ref[2:3, :] * u + convw_ref[0:1, :] * u2 + convw_ref[1:2, :] * u1
    a_pre = proj(1) * yc
    ya = _dot(a_pre.astype(BF16), wco_ref[...])
    yag_ref[...] = jax.nn.sigmoid(proj(8)) * ya
    sgb_ref[...] = jax.nn.sigmoid(proj(9))

    cos = cos_ref[...]
    sin = sin_ref[...]
    gsum = gsum_ref[...]
    q_ref[...] = _rope_lanes(_head_rms(proj(3), qnw_ref[...], gsum), cos, sin)
    k = _rope_lanes(_head_rms(proj(4), knw_ref[:, 0:hk], gsum[0:hk, 0:hk]), cos, sin)
    k_ref[...] = k
    kt_ref[...] = k.T
    v = proj(5)
    v_ref[...] = v
    vt_ref[...] = v.T
    qi_ref[...] = _rope_lanes(proj(6), cos, sin)
    slab = proj(7)
    lane = lax.broadcasted_iota(I32, (1, LANES), 1)
    kiw = jnp.where(lane < IDX_DIM, _rope_lanes(slab, cos, sin), slab * IDX_SCALE)
    kiw_ref[...] = kiw
    wit_ref[...] = kiw.T[IDX_DIM:IDX_DIM + IDX_HEADS, :]


def _inproj(x2, shift, scale, n1w, w_pad, cos, sin, qnw, knw, gsum, convw, wco, ext1, ext2, *, tm, seq_rows,
            per_row_mod):
    rows, d = x2.shape
    dc = convw.shape[1]
    hq = N_HEADS * HEAD_DIM
    hk = N_KV_HEADS * HEAD_DIM
    hi_w = IDX_HEADS * IDX_DIM
    nt = rows // tm
    pos_tiles = cos.shape[0] // tm
    tiles_per_seq = max(seq_rows // tm, 1)
    if per_row_mod:
        mod_spec = pl.BlockSpec((tm, d), lambda i: (i, 0))
        ext_spec = pl.BlockSpec((tm, dc), lambda i: (i, 0))
    else:
        mod_spec = pl.BlockSpec((None, 1, d), lambda i: (i // tiles_per_seq, 0, 0))
        ext_spec = pl.BlockSpec((1, dc), lambda i: (0, 0))
    const = lambda shape: pl.BlockSpec(shape, lambda i: tuple(0 for _ in shape))
    row = lambda w: pl.BlockSpec((tm, w), lambda i: (i, 0))
    out_widths = [dc, hq, hk, hk, hi_w, LANES, d, d]
    n_seq = max(rows // (tiles_per_seq * tm), 1)
    kvt_spec = pl.BlockSpec((None, hk, tm), lambda i: (i // tiles_per_seq, 0, i % tiles_per_seq))
    kvt_shape = jax.ShapeDtypeStruct((n_seq, hk, tiles_per_seq * tm), F32)
    return pl.pallas_call(
        functools.partial(_inproj_body, tm, seq_rows, dc),
        grid=(nt,),
        in_specs=[row(d), mod_spec, mod_spec, const((1, d)), const(w_pad.shape),
                  pl.BlockSpec((tm, LANES), lambda i: (i % pos_tiles, 0)),
                  pl.BlockSpec((tm, LANES), lambda i: (i % pos_tiles, 0)),
                  const((1, hq)), const((1, hq)), const((hq, hq)), const(convw.shape), const(wco.shape),
                  ext_spec, ext_spec],
        out_specs=[row(w) for w in out_widths] + [kvt_spec, kvt_spec, pl.BlockSpec((IDX_HEADS, tm), lambda i: (0, i))],
        out_shape=[jax.ShapeDtypeStruct((rows, w), F32) for w in out_widths]
                  + [kvt_shape, kvt_shape, jax.ShapeDtypeStruct((IDX_HEADS, rows), F32)],
        scratch_shapes=[pltpu.VMEM((SUBLANES, dc), F32), pltpu.VMEM((tm + SUBLANES, dc), F32)],
        compiler_params=_cparams(("arbitrary",)),
    )(x2, shift, scale, n1w, w_pad, cos, sin, qnw, knw, gsum, convw, wco, ext1, ext2)


def _sortable_key(score):
    bits = lax.bitcast_convert_type(score, I32)
    key = jnp.where(bits < 0, bits ^ jnp.int32(0x7FFFFFFF), bits)
    return jnp.where(score == 0.0, jnp.int32(0), key)


def _kth_largest_key(count_ge, shape, k):
    ans = jnp.where(count_ge(jnp.zeros(shape, I32)) >= k, jnp.int32(0), jnp.int32(INT_MIN))

    def step(it, ans):
        cand = ans + jnp.left_shift(jnp.int32(1), jnp.int32(30) - it)
        return jnp.where(count_ge(cand) >= k, cand, ans)

    return lax.fori_loop(0, 31, step, ans)


def _tie_cut(count_eq_below, shape, need, nbits):
    def step(it, c):
        cand = c + jnp.left_shift(jnp.int32(1), jnp.int32(nbits - 1) - it)
        return jnp.where(count_eq_below(cand) < need, cand, c)

    return lax.fori_loop(0, nbits, step, jnp.zeros(shape, I32))


def _col_reduce(x, op, rows=64):
    n = x.shape[0]
    if n > rows and n % rows == 0:
        x = op(x.reshape(n // rows, rows, x.shape[1]), axis=0)
    return op(x, axis=0, keepdims=True)


def _pattn_class(tq, klen, topk, j, q_ref, qi_ref, wit_ref, k_ref, vt_ref, kiw_ref, o_ref, key_ref, cut_ref):
    qi = qi_ref[...]
    wit = wit_ref[...]
    ki = kiw_ref[0:klen, 0:IDX_DIM].astype(BF16)
    score = jnp.zeros((klen, tq), F32)
    hpm = 4
    for h0 in range(0, IDX_HEADS, hpm):
        qs = jnp.concatenate([qi[:, h * IDX_DIM:(h + 1) * IDX_DIM] for h in range(h0, h0 + hpm)], axis=0)
        s = _dot_nt(ki, qs.astype(BF16))
        for g in range(hpm):
            score = score + wit[h0 + g:h0 + g + 1, :] * jnp.maximum(s[:, g * tq:(g + 1) * tq], 0.0)
    kpos = lax.broadcasted_iota(I32, (klen, 1), 0)
    qpos = j * tq + lax.broadcasted_iota(I32, (1, tq), 1)
    causal = kpos <= qpos
    score = jnp.where(causal, score, NEG_INF)
    key_ref[0:klen, :] = _sortable_key(score)

    def count_ge(c):
        return _col_reduce((key_ref[0:klen, :] >= c).astype(F32), jnp.sum)

    thr = _kth_largest_key(count_ge, (1, tq), topk)
    key = key_ref[0:klen, :]
    gt = key > thr
    eq = key == thr
    need = topk - _col_reduce(gt.astype(F32), jnp.sum)
    n_eq = _col_reduce(eq.astype(F32), jnp.sum)
    cut_ref[...] = jnp.full(cut_ref.shape, klen, I32)

    @pl.when(jnp.max(n_eq - need) > 0)
    def _():
        def count_eq_below(c):
            hit = (key_ref[0:klen, :] == thr) & (kpos < c)
            return _col_reduce(hit.astype(F32), jnp.sum)

        cut = _tie_cut(count_eq_below, (1, tq), need, int(klen - 1).bit_length())
        cut_ref[...] = jnp.broadcast_to(cut, cut_ref.shape)

    cut = cut_ref[0:1, :]
    bias = jnp.where((gt | (eq & (kpos <= cut))) & causal, 0.0, NEG_INF)

    q = q_ref[...] * (HEAD_DIM ** -0.5)
    k = k_ref[0:klen, :].astype(BF16)
    group = N_HEADS // N_KV_HEADS
    outs = []
    for h in range(N_HEADS):
        n = h // group
        if h % group == 0:
            qs = jnp.concatenate([q[:, g * HEAD_DIM:(g + 1) * HEAD_DIM] for g in range(h, h + group)], axis=0)
            logits_n = _dot_nt(k[:, n * HEAD_DIM:(n + 1) * HEAD_DIM], qs.astype(BF16))
        logits = logits_n[:, (h % group) * tq:(h % group + 1) * tq] + bias
        m = _col_reduce(logits, jnp.max)
        p = jnp.exp(logits - m)
        l = _col_reduce(p, jnp.sum)
        vt = vt_ref[n * HEAD_DIM:(n + 1) * HEAD_DIM, 0:klen].astype(BF16)
        outs.append(_dot(vt, p.astype(BF16)) / l)
    o_ref[...] = jnp.concatenate(outs, axis=0).T


def _pattn_body(tq, t, topk, n_cls, q_ref, qi_ref, wit_ref, k_ref, vt_ref, kiw_ref, o_ref, key_ref, cut_ref):
    j = pl.program_id(1)
    blocks_per_cls = (t // tq) // n_cls
    for c in range(n_cls):
        @pl.when((j >= c * blocks_per_cls) & (j < (c + 1) * blocks_per_cls))
        def _(c=c):
            _pattn_class(tq, (c + 1) * blocks_per_cls * tq, topk, j, q_ref, qi_ref, wit_ref, k_ref, vt_ref, kiw_ref,
                         o_ref, key_ref, cut_ref)


def _pattn(q, qi, wit, k, vt, kiw, *, bsz, t):
    hq = q.shape[1]
    tq = Q_BLOCK
    nqb = t // tq
    n_cls = max(c for c in (1, 2, 4, 8) if nqb % c == 0)
    topk = min(TOPK_MAX, t // 4)
    qspec = lambda w: pl.BlockSpec((tq, w), lambda bi, j: (bi * nqb + j, 0))
    kspec = lambda w: pl.BlockSpec((t, w), lambda bi, j: (bi, 0))
    return pl.pallas_call(
        functools.partial(_pattn_body, tq, t, topk, n_cls),
        grid=(bsz, nqb),
        in_specs=[qspec(hq), qspec(qi.shape[1]),
                  pl.BlockSpec((wit.shape[0], tq), lambda bi, j: (0, bi * nqb + j)),
                  kspec(k.shape[1]),
                  pl.BlockSpec((None, vt.shape[1], t), lambda bi, j: (bi, 0, 0)),
                  kspec(kiw.shape[1])],
        out_specs=qspec(hq),
        out_shape=jax.ShapeDtypeStruct((bsz * t, hq), F32),
        scratch_shapes=[pltpu.VMEM((t, tq), I32), pltpu.VMEM((SUBLANES, tq), I32)],
        compiler_params=_cparams(("arbitrary", "arbitrary")),
    )(q, qi, wit, k, vt, kiw)


def _sattn_body(n_pages, g, ds, topk, pt_ref, qi_ref, w_ref, q_ref, kin_ref, kn_ref, vn_ref, *rest):
    kic, kc, vc = rest[0:g], rest[g:2 * g], rest[2 * g:3 * g]
    o_ref, score_ref, mask_ref, m_ref, l_ref, acc_ref, kicat_ref, kcat_ref, vcat_ref = rest[3 * g:]
    ph = pl.program_id(1)
    p = pl.program_id(2)
    n_steps = n_pages // g
    group = N_HEADS // N_KV_HEADS
    sub = lax.broadcasted_iota(I32, (SUBLANES, LANES), 0)
    lane = lax.broadcasted_iota(I32, (SUBLANES, LANES), 1)
    new_valid = (lane <= sub) & (lane < ds)

    def scores(ki_bf, npg):
        s = _dot(qi_ref[...].astype(BF16), ki_bf)
        w = w_ref[...]
        w = jnp.concatenate([w] * npg, axis=1) if npg > 1 else w
        acc = jnp.zeros((SUBLANES, npg * LANES), F32)
        for h in range(IDX_HEADS):
            r = slice(h * SUBLANES, (h + 1) * SUBLANES)
            acc = acc + w[r, :] * jnp.maximum(s[r, :], 0.0)
        return acc

    @pl.when(ph == 0)
    def _():
        for gi in range(g):
            kicat_ref[:, gi * LANES:(gi + 1) * LANES] = kic[gi][...].astype(BF16)
        sc = scores(kicat_ref[...], g)
        for gi in range(g):
            score_ref[p * g + gi] = sc[:, gi * LANES:(gi + 1) * LANES]

        @pl.when(p == 0)
        def _():
            score_ref[n_pages] = jnp.where(new_valid, scores(kin_ref[...].astype(BF16), 1), NEG_INF)

        @pl.when(p == n_steps - 1)
        def _():
            key = _sortable_key(score_ref[...])
            slab = lax.broadcasted_iota(I32, key.shape, 0)
            idx = slab * LANES + lax.broadcasted_iota(I32, key.shape, 2)

            def count(hit):
                x = hit.astype(F32)
                parts = [jnp.sum(x[s:s + 16], axis=0) for s in range(0, x.shape[0], 16)]
                c = functools.reduce(lambda u, v: u + v, parts)
                return jnp.sum(c, axis=1, keepdims=True)[None]

            thr = _kth_largest_key(lambda c: count(key >= c), (1, SUBLANES, 1), topk)
            gt = key > thr
            eq = key == thr
            need = topk - count(gt)
            valid = (slab < n_pages) | new_valid[None]
            mask_ref[...] = ((gt | eq) & valid).astype(F32)

            real_row = lax.broadcasted_iota(I32, (1, SUBLANES, 1), 1) < ds

            @pl.when(jnp.max(jnp.where(real_row, count(eq) - need, 0.0)) > 0)
            def _():
                nbits = int((n_pages + 1) * LANES - 1).bit_length()
                cut = _tie_cut(lambda c: count(eq & (idx < c)), (1, SUBLANES, 1), need, nbits)
                mask_ref[...] = ((gt | (eq & (idx <= cut))) & valid).astype(F32)

    def attend(mask8, kb, vb):
        maskg = jnp.concatenate([mask8] * group, axis=0) > 0.5
        for n in range(N_KV_HEADS):
            qn = (q_ref[n] * (HEAD_DIM ** -0.5)).astype(BF16)
            logits = _dot(qn, kb[n * HEAD_DIM:(n + 1) * HEAD_DIM, :])
            logits = jnp.where(maskg, logits, NEG_INF)
            m_old = m_ref[n][:, 0:1]
            m_new = jnp.maximum(m_old, jnp.max(logits, axis=1, keepdims=True))
            pr = jnp.where(maskg, jnp.exp(logits - m_new), 0.0)
            alpha = jnp.exp(m_old - m_new)
            l_new = alpha * l_ref[n][:, 0:1] + jnp.sum(pr, axis=1, keepdims=True)
            acc_ref[n] = alpha * acc_ref[n] + _dot_nt(pr.astype(BF16), vb[n * HEAD_DIM:(n + 1) * HEAD_DIM, :])
            m_ref[n] = jnp.broadcast_to(m_new, m_ref.shape[1:])
            l_ref[n] = jnp.broadcast_to(l_new, l_ref.shape[1:])

    @pl.when(ph == 1)
    def _():
        @pl.when(p == 0)
        def _():
            m_ref[...] = jnp.full(m_ref.shape, NEG_INF, F32)
            l_ref[...] = jnp.zeros(l_ref.shape, F32)
            acc_ref[...] = jnp.zeros(acc_ref.shape, F32)
            attend(mask_ref[n_pages], kn_ref[...].astype(BF16), vn_ref[...].astype(BF16))

        for gi in range(g):
            kcat_ref[:, gi * LANES:(gi + 1) * LANES] = kc[gi][...].astype(BF16)
            vcat_ref[:, gi * LANES:(gi + 1) * LANES] = vc[gi][...].astype(BF16)
        mask = jnp.concatenate([mask_ref[p * g + gi] for gi in range(g)], axis=1)
        attend(mask, kcat_ref[...], vcat_ref[...])

        @pl.when(p == n_steps - 1)
        def _():
            o_ref[...] = acc_ref[...] / l_ref[...][:, :, 0:HEAD_DIM]


def _sattn(page_table, qi_arr, w_arr, q_arr, kin, kn, vn, cache_kidx, cache_k, cache_v, *, ds):
    db, n_pages = page_table.shape
    page = cache_k.shape[2]
    assert page == LANES
    g = max(c for c in (1, 2, 4, 8, 16, 32, 64) if n_pages % c == 0)
    n_steps = n_pages // g
    topk = min(TOPK_MAX, (n_pages * page + ds) // 4)
    group = N_HEADS // N_KV_HEADS
    hk = N_KV_HEADS * HEAD_DIM
    rows_i = IDX_HEADS * SUBLANES
    rows_q = group * SUBLANES
    pt = page_table.reshape(-1)
    per_b = lambda *shape: pl.BlockSpec((None,) + shape, lambda b, ph, p, pt: (b,) + tuple(0 for _ in shape))

    def kic_map(gi, b, ph, p, pt):
        return (pt[b * n_pages + (p * (1 - ph) + (n_steps - 1) * ph) * g + gi], 0, 0)

    def kvc_map(gi, b, ph, p, pt):
        return (pt[b * n_pages + p * ph * g + gi], 0, 0)

    kic_specs = [pl.BlockSpec((None, IDX_DIM, page), functools.partial(kic_map, gi)) for gi in range(g)]
    kvc_specs = [pl.BlockSpec((None, hk, page), functools.partial(kvc_map, gi)) for gi in range(g)]
    grid_spec = pltpu.PrefetchScalarGridSpec(
        num_scalar_prefetch=1,
        grid=(db, 2, n_steps),
        in_specs=[per_b(rows_i, IDX_DIM), per_b(rows_i, LANES), per_b(N_KV_HEADS, rows_q, HEAD_DIM),
                  per_b(IDX_DIM, page), per_b(hk, page), per_b(hk, page)] + kic_specs + kvc_specs + kvc_specs,
        out_specs=per_b(N_KV_HEADS, rows_q, HEAD_DIM),
        scratch_shapes=[pltpu.VMEM((n_pages + 1, SUBLANES, LANES), F32),
                        pltpu.VMEM((n_pages + 1, SUBLANES, LANES), F32),
                        pltpu.VMEM((N_KV_HEADS, rows_q, LANES), F32),
                        pltpu.VMEM((N_KV_HEADS, rows_q, LANES), F32),
                        pltpu.VMEM((N_KV_HEADS, rows_q, HEAD_DIM), F32),
                        pltpu.VMEM((IDX_DIM, g * page), BF16),
                        pltpu.VMEM((hk, g * page), BF16),
                        pltpu.VMEM((hk, g * page), BF16)],
    )
    return pl.pallas_call(
        functools.partial(_sattn_body, n_pages, g, ds, topk),
        grid_spec=grid_spec,
        out_shape=jax.ShapeDtypeStruct((db, N_KV_HEADS, rows_q, HEAD_DIM), F32),
        compiler_params=_cparams(("arbitrary", "arbitrary", "arbitrary")),
    )(pt, qi_arr, w_arr, q_arr, kin, kn, vn, *([cache_kidx] * g), *([cache_k] * g), *([cache_v] * g))


def _finish_body(x_ref, yag_ref, sgb_ref, attn_ref, gate_ref, shift_ref, scale_ref, n2w_ref, wao_ref, wout_ref, wq_ref,
                 x1_ref, h2t_ref, pq_ref):
    yb = _dot(attn_ref[...].astype(BF16), wao_ref[...])
    mix = yag_ref[...] + sgb_ref[...] * yb
    x1 = x_ref[...] + gate_ref[...] * _dot(mix.astype(BF16), wout_ref[...])
    x1_ref[...] = x1
    ms = jnp.mean(x1 * x1, axis=-1, keepdims=True)
    h2 = x1 * lax.rsqrt(ms + NORM_EPS) * n2w_ref[...]
    h2 = h2 * (1.0 + scale_ref[...]) + shift_ref[...]
    h2t_ref[...] = h2.T.astype(BF16)
    pq_ref[...] = _dot(h2.astype(BF16), wq_ref[...]).astype(BF16)


def _finish(x2, yag, sgb, attn, gate, shift, scale, n2w, wao, wout, wq, *, tm, seq_rows, per_row_mod):
    rows, d = x2.shape
    nt = rows // tm
    tiles_per_seq = max(seq_rows // tm, 1)
    if per_row_mod:
        mod_spec = pl.BlockSpec((tm, d), lambda i: (i, 0))
    else:
        mod_spec = pl.BlockSpec((None, 1, d), lambda i: (i // tiles_per_seq, 0, 0))
    const = lambda shape: pl.BlockSpec(shape, lambda i: tuple(0 for _ in shape))
    row = lambda w: pl.BlockSpec((tm, w), lambda i: (i, 0))
    nq = wq.shape[1]
    return pl.pallas_call(
        _finish_body,
        grid=(nt,),
        in_specs=[row(d), row(d), row(d), row(attn.shape[1]), mod_spec, mod_spec, mod_spec, const((1, d)),
                  const(wao.shape), const(wout.shape), const(wq.shape)],
        out_specs=[row(d), pl.BlockSpec((d, tm), lambda i: (0, i)), row(nq)],
        out_shape=[jax.ShapeDtypeStruct((rows, d), F32), jax.ShapeDtypeStruct((d, rows), BF16),
                   jax.ShapeDtypeStruct((rows, nq), BF16)],
        compiler_params=_cparams(("arbitrary",)),
    )(x2, yag, sgb, attn, gate, shift, scale, n2w, wao, wout, wq)


def _top_sorted(x, n):
    vals = []
    for _ in range(n):
        m = jnp.max(x, axis=0, keepdims=True)
        vals.append(m)
        x = jnp.where(x >= m, -jnp.inf, x)
    return jnp.concatenate(vals, axis=0)


def _gelu_tanh(x):
    c = np.float32(np.sqrt(2.0 / np.pi))
    return x * (0.5 * (1.0 + jnp.tanh(c * (x + 0.044715 * (x * x * x)))))


def _peer_select(lg, pq_ref, sk_ref, ct_ref, ea_ref, eb_ref):
    half = PEER_QDIM // 2
    rows = pl.ds(pl.multiple_of(lg * LANES, LANES), LANES)
    for h in range(PEER_HEADS):
        sa = _dot_nt(sk_ref[2 * h], pq_ref[rows, (2 * h) * half:(2 * h + 1) * half])
        sb = _dot_nt(sk_ref[2 * h + 1], pq_ref[rows, (2 * h + 1) * half:(2 * h + 2) * half])
        ta = _top_sorted(sa, PEER_TOPK + 1)
        tb = _top_sorted(sb, PEER_TOPK + 1)
        tail = jnp.concatenate([ta[0:1] + tb[16:17], ta[16:17] + tb[0:1],
                                jnp.full((SUBLANES - 2, LANES), -jnp.inf, F32)], axis=0)
        cand = jnp.concatenate(
            [ta[0:1] + tb[0:16]] + [ta[r:r + 1] + tb[0:8] for r in range(1, 8)] + [ta[8:16] + tb[0:1], tail], axis=0)
        top = _top_sorted(cand, PEER_TOPK + 1)
        thr = 0.5 * (top[PEER_TOPK - 1:PEER_TOPK] + top[PEER_TOPK:PEER_TOPK + 1])
        z = jnp.sum(jnp.exp(top[0:PEER_TOPK] - top[0:1]), axis=0, keepdims=True)
        ct_ref[lg, h] = jnp.exp((thr - tb[0:1]) - sa) / z
        ea_ref[lg, h] = jnp.exp(sa - ta[0:1])
        eb_ref[lg, h] = (jnp.exp(sb - tb[0:1]) / z).reshape(eb_ref.shape[2:])


def _peer_body(ce, tm, nc, h2t_ref, pq_ref, x1_ref, gate_ref, sk_ref, u_ref, vt_ref, out_ref,
               ct_ref, ea_ref, eb_ref, yt_ref, p_ref, act_ref):
    c = pl.program_id(1)
    kk = PEER_KEYS
    n_lg = tm // LANES
    n_sub = ce // kk
    pair = 2 if n_sub % 2 == 0 else 1
    jrows = kk // 2
    n_half = 2 if n_lg % 2 == 0 else 1
    hw = tm // n_half

    @pl.when(c == 0)
    def _():
        yt_ref[...] = jnp.zeros_like(yt_ref)

        def body(lg, carry):
            _peer_select(lg, pq_ref, sk_ref, ct_ref, ea_ref, eb_ref)
            return carry

        lax.fori_loop(0, n_lg, body, 0)

    def expert_act(n):
        act_ref[:, n * hw:(n + 1) * hw] = _dot(u_ref[...], h2t_ref[:, n * hw:(n + 1) * hw])

    expert_act(0)
    for n in range(n_half):
        for lg in range(n * n_lg // n_half, (n + 1) * n_lg // n_half):
            if n + 1 < n_half and lg == (n + 1) * n_lg // n_half - 1:
                expert_act(n + 1)
            cols = slice(lg * LANES, (lg + 1) * LANES)
            for ip in range(n_sub // pair):
                subs = [pair * ip + t for t in range(pair)]
                for jh in range(2):
                    vrows = slice(jh * jrows // SUBLANES, (jh + 1) * jrows // SUBLANES)
                    w = [None] * pair
                    for h in range(PEER_HEADS):
                        ebh = eb_ref[lg, h, vrows]
                        for t, s in enumerate(subs):
                            i = c * n_sub + s
                            ctv = jnp.broadcast_to(ct_ref[lg, h, pl.ds(i, 1), :], (SUBLANES, LANES))[None]
                            eav = jnp.broadcast_to(ea_ref[lg, h, pl.ds(i, 1), :], (SUBLANES, LANES))[None]
                            term = jnp.where(ebh >= ctv, ebh, 0.0) * eav
                            w[t] = term if w[t] is None else w[t] + term
                    for t, s in enumerate(subs):
                        rows = slice(s * kk + jh * jrows, s * kk + (jh + 1) * jrows)
                        prod = w[t].reshape(jrows, LANES) * _gelu_tanh(act_ref[rows, cols])
                        p_ref[rows, cols] = prod.astype(BF16)
        yt_ref[:, n * hw:(n + 1) * hw] += _dot(vt_ref[...], p_ref[:, n * hw:(n + 1) * hw])

    @pl.when(c == nc - 1)
    def _():
        out_ref[...] = x1_ref[...] + gate_ref[...] * yt_ref[...].T


def _peer(h2t, pq, x1, gate, sk, u_bf, vt_blk, *, tm, seq_rows, per_row_mod):
    rows, d = x1.shape
    ne = u_bf.shape[0]
    ce = vt_blk.shape[2]
    nt = rows // tm
    nc = ne // ce
    assert tm % LANES == 0 and rows % tm == 0 and ne % ce == 0 and ce % PEER_KEYS == 0
    tiles_per_seq = max(seq_rows // tm, 1)
    if per_row_mod:
        mod_spec = pl.BlockSpec((tm, d), lambda i, c: (i, 0))
    else:
        mod_spec = pl.BlockSpec((None, 1, d), lambda i, c: (i // tiles_per_seq, 0, 0))
    row = lambda w: pl.BlockSpec((tm, w), lambda i, c: (i, 0))
    sel_shape = (tm // LANES, PEER_HEADS, PEER_KEYS, LANES)
    vreg_shape = (tm // LANES, PEER_HEADS, PEER_KEYS // SUBLANES, SUBLANES, LANES)
    return pl.pallas_call(
        functools.partial(_peer_body, ce, tm, nc),
        grid=(nt, nc),
        in_specs=[pl.BlockSpec((d, tm), lambda i, c: (0, i)), row(pq.shape[1]), row(d), mod_spec,
                  pl.BlockSpec(sk.shape, lambda i, c: (0, 0, 0)),
                  pl.BlockSpec((ce, d), lambda i, c: (c, 0)),
                  pl.BlockSpec((None, d, ce), lambda i, c: (c, 0, 0))],
        out_specs=row(d),
        out_shape=jax.ShapeDtypeStruct((rows, d), F32),
        scratch_shapes=[pltpu.VMEM(sel_shape, F32), pltpu.VMEM(sel_shape, F32), pltpu.VMEM(vreg_shape, F32),
                        pltpu.VMEM((d, tm), F32), pltpu.VMEM((ce, tm), BF16), pltpu.VMEM((ce, tm), F32)],
        compiler_params=_cparams(("arbitrary", "arbitrary")),
    )(h2t, pq, x1, gate, sk, u_bf, vt_blk)


def _rope_tables(pos):
    half = HEAD_DIM // 2
    freqs = ROPE_THETA ** (-jnp.arange(half, dtype=F32) / half)
    ang = pos.astype(F32)[:, None] * freqs[None, :]
    cos, sin = jnp.cos(ang), jnp.sin(ang)
    cos2 = jnp.concatenate([cos, cos, cos, cos], axis=1)
    sin2 = jnp.concatenate([-sin, sin, -sin, sin], axis=1)
    return cos2, sin2


def _pick_tile(rows, pref):
    t = min(pref, rows)
    while rows % t:
        t //= 2
    return t


def kernel(x_prompt, x_sample, cache_k, cache_v, cache_kidx, state_conv, page_table, c_prompt, c_sample, w_ada, b_ada,
           norm1_w, w_in, conv_w, q_norm_w, k_norm_w, w_conv_out, w_attn_out, w_out, norm2_w, peer_w_q, peer_sub_keys,
           peer_u, peer_v):
    bsz, seq, d = x_prompt.shape
    db, ds, _ = x_sample.shape
    depth = w_ada.shape[0]
    dc = conv_w.shape[2]
    hq = N_HEADS * HEAD_DIM
    hk = N_KV_HEADS * HEAD_DIM
    hi_w = IDX_HEADS * IDX_DIM
    n_pages = page_table.shape[1]
    page = cache_k.shape[2]
    past = n_pages * page
    group = N_HEADS // N_KV_HEADS

    pos_p = jnp.arange(seq, dtype=I32)
    pos_s = past + jnp.arange(ds, dtype=I32)
    cos_p, sin_p = _rope_tables(pos_p)
    cos_s, sin_s = _rope_tables(jnp.tile(pos_s, db))
    gsum = jnp.kron(jnp.eye(N_HEADS, dtype=F32), jnp.ones((HEAD_DIM, HEAD_DIM), F32)).astype(BF16)

    xp = x_prompt.reshape(bsz * seq, d)
    xs = x_sample.reshape(db * ds, d)
    outs = {n: [] for n in ("kp", "vp", "kip", "cp", "ks", "vs", "kis", "cs")}

    for layer in range(depth):
        mod = _adaln(jnp.concatenate([c_prompt, c_sample], axis=0), w_ada[layer], b_ada[layer])
        mod_p = [m.reshape(bsz, 1, d) for m in jnp.split(mod[:bsz], N_MOD, axis=-1)]
        mod_s = [jnp.repeat(m, ds, axis=0) for m in jnp.split(mod[bsz:], N_MOD, axis=-1)]

        w = w_in[layer]
        o_ki = 3 * dc + hq + 2 * hk + hi_w
        o_g = o_ki + IDX_DIM + IDX_HEADS
        slab = jnp.concatenate([w[:, o_ki:o_g], jnp.zeros((d, LANES - IDX_DIM - IDX_HEADS), F32)], axis=1)
        w_pad = jnp.concatenate([w[:, :o_ki], slab, w[:, o_g:]], axis=1).astype(BF16)
        n1w = norm1_w[layer].reshape(1, d)
        qnw = jnp.tile(q_norm_w[layer], N_HEADS).reshape(1, hq)
        knw = jnp.tile(k_norm_w[layer], N_HEADS).reshape(1, hq)
        wco = w_conv_out[layer].astype(BF16)
        wao = w_attn_out[layer].astype(BF16)
        wout = w_out[layer].astype(BF16)
        wq = peer_w_q[layer].astype(BF16)
        n2w = norm2_w[layer].reshape(1, d)
        sk = jnp.transpose(peer_sub_keys[layer], (1, 0, 2, 3)).reshape(2 * PEER_HEADS, PEER_KEYS, PEER_QDIM // 2)
        sk = sk.astype(BF16)
        u_bf = peer_u[layer].astype(BF16)
        ce = min(2048, u_bf.shape[0])
        vt_bf = peer_v[layer].astype(BF16).reshape(-1, ce, d).transpose(0, 2, 1)

        tm_p = _pick_tile(seq, 256)
        zero_ext = jnp.zeros((1, dc), F32)
        u, q, k, v, qi, kiw, yag, sgb, kt, vt, wit = _inproj(
            xp, mod_p[0], mod_p[1], n1w, w_pad, cos_p, sin_p, qnw, knw, gsum, conv_w[layer], wco, zero_ext, zero_ext,
            tm=tm_p, seq_rows=seq, per_row_mod=False)
        ki = kiw[:, :IDX_DIM]
        attn = _pattn(q, qi, wit, k, vt, kiw, bsz=bsz, t=seq)
        tm_f = _pick_tile(seq, 512)
        x1, h2t, pq = _finish(xp, yag, sgb, attn, mod_p[2], mod_p[3], mod_p[4], n2w, wao, wout,
                              wq, tm=tm_f, seq_rows=seq, per_row_mod=False)
        xp = _peer(h2t, pq, x1, mod_p[5], sk, u_bf, vt_bf, tm=tm_f, seq_rows=seq, per_row_mod=False)
        outs["kp"].append(kt.reshape(bsz, N_KV_HEADS, HEAD_DIM, seq).transpose(0, 3, 1, 2))
        outs["vp"].append(vt.reshape(bsz, N_KV_HEADS, HEAD_DIM, seq).transpose(0, 3, 1, 2))
        outs["kip"].append(ki.reshape(bsz, seq, IDX_DIM))
        outs["cp"].append(u.reshape(bsz, seq, dc)[:, seq - 2:, :])

        rows_s = db * ds
        st = state_conv[layer]
        posr = jnp.tile(jnp.arange(ds), db)
        ext1 = jnp.repeat(st[:, 1, :], ds, axis=0)
        ext2 = jnp.where((posr == 0)[:, None], jnp.repeat(st[:, 0, :], ds, axis=0), ext1)
        u, q, k, v, qi, kiw, yag, sgb, _, _, _ = _inproj(
            xs, mod_s[0], mod_s[1], n1w, w_pad, cos_s, sin_s, qnw, knw, gsum, conv_w[layer], wco, ext1, ext2,
            tm=rows_s, seq_rows=ds, per_row_mod=True)
        ki = kiw[:, :IDX_DIM]
        wi = kiw[:, IDX_DIM:IDX_DIM + IDX_HEADS]
        pad_q = lambda a: jnp.pad(a, ((0, 0), (0, 0), (0, SUBLANES - ds), (0, 0)))
        qi_arr = pad_q(qi.reshape(db, ds, IDX_HEADS, IDX_DIM).transpose(0, 2, 1, 3)).reshape(db, IDX_HEADS * SUBLANES, IDX_DIM)
        w_arr = pad_q(wi.reshape(db, ds, IDX_HEADS, 1).transpose(0, 2, 1, 3)).reshape(db, IDX_HEADS * SUBLANES, 1)
        w_arr = jnp.broadcast_to(w_arr, (db, IDX_HEADS * SUBLANES, LANES))
        q_arr = pad_q(q.reshape(db, ds, N_HEADS, HEAD_DIM).transpose(0, 2, 1, 3)).reshape(db, N_KV_HEADS, group * SUBLANES, HEAD_DIM)
        pad_page = lambda a: jnp.pad(a.reshape(db, ds, -1), ((0, 0), (0, page - ds), (0, 0))).transpose(0, 2, 1)
        cki = jnp.transpose(cache_kidx[layer], (0, 2, 1))
        ck = jnp.transpose(cache_k[layer], (0, 2, 3, 1)).reshape(-1, hk, page)
        cv = jnp.transpose(cache_v[layer], (0, 2, 3, 1)).reshape(-1, hk, page)
        o = _sattn(page_table, qi_arr, w_arr, q_arr, pad_page(ki), pad_page(k), pad_page(v), cki, ck, cv, ds=ds)
        attn = o.reshape(db, N_HEADS, SUBLANES, HEAD_DIM)[:, :, :ds, :].transpose(0, 2, 1, 3).reshape(rows_s, hq)
        x1, h2t, pq = _finish(xs, yag, sgb, attn, mod_s[2], mod_s[3], mod_s[4], n2w, wao, wout, wq,
                              tm=rows_s, seq_rows=ds, per_row_mod=True)
        xs = _peer(h2t, pq, x1, mod_s[5], sk, u_bf, vt_bf, tm=rows_s, seq_rows=ds, per_row_mod=True)
        outs["ks"].append(k.reshape(db, ds, N_KV_HEADS, HEAD_DIM))
        outs["vs"].append(v.reshape(db, ds, N_KV_HEADS, HEAD_DIM))
        outs["kis"].append(ki.reshape(db, ds, IDX_DIM))
        full = jnp.concatenate([st, u.reshape(db, ds, dc)], axis=1)
        outs["cs"].append(full[:, full.shape[1] - 2:, :])

    stack = lambda n: jnp.stack(outs[n])
    return (xp.reshape(bsz, seq, d), xs.reshape(db, ds, d), stack("kp"), stack("vp"), stack("kip"), stack("cp"),
            stack("ks"), stack("vs"), stack("kis"), stack("cs"))
```

```python
import functools

import numpy as np
import jax
import jax.numpy as jnp
from jax import lax
from jax.experimental import pallas as pl
from jax.experimental.pallas import tpu as pltpu

F32 = jnp.float32
BF16 = jnp.bfloat16
I32 = jnp.int32

N_HEADS = 8
N_KV_HEADS = 2
HEAD_DIM = 64
IDX_HEADS = 8
IDX_DIM = 64
IDX_SCALE = (IDX_DIM ** -0.5) * (IDX_HEADS ** -0.5)
TOPK_MAX = 256
Q_BLOCK = 128
ROPE_THETA = 10000.0
PEER_HEADS = 8
PEER_KEYS = 128
PEER_QDIM = 256
PEER_TOPK = 16
N_MOD = 6
NORM_EPS = 1e-6
NEG_INF = -1e30
INT_MIN = -2 ** 31

LANES = 128
SUBLANES = 8
VMEM_LIMIT = 56 * 1024 * 1024


def _cparams(sem):
    return pltpu.CompilerParams(dimension_semantics=sem, vmem_limit_bytes=VMEM_LIMIT)


def _dot(a, b):
    return jnp.dot(a, b, preferred_element_type=F32)


def _dot_nt(a, b):
    return lax.dot_general(a, b, (((1,), (1,)), ((), ())), preferred_element_type=F32)


def _adaln_body(c_ref, w_ref, b_ref, o_ref):
    o_ref[...] = jnp.dot(c_ref[...], w_ref[...], preferred_element_type=F32,
                         precision=lax.Precision.HIGHEST) + b_ref[...]


def _adaln(c, w_ada, b_ada):
    m, d = c.shape
    n = w_ada.shape[1]
    tn = d
    return pl.pallas_call(
        _adaln_body,
        grid=(n // tn,),
        in_specs=[pl.BlockSpec((m, d), lambda j: (0, 0)),
                  pl.BlockSpec((d, tn), lambda j: (0, j)),
                  pl.BlockSpec((1, tn), lambda j: (0, j))],
        out_specs=pl.BlockSpec((m, tn), lambda j: (0, j)),
        out_shape=jax.ShapeDtypeStruct((m, n), F32),
        compiler_params=_cparams(("arbitrary",)),
    )(c, w_ada, b_ada.reshape(1, n))


def _rope_lanes(z, cos, sin_signed):
    w = z.shape[1]
    reps = w // LANES
    cosw = jnp.concatenate([cos] * reps, axis=1) if reps > 1 else cos
    sinw = jnp.concatenate([sin_signed] * reps, axis=1) if reps > 1 else sin_signed
    half = HEAD_DIM // 2
    lane = lax.broadcasted_iota(I32, (1, w), 1)
    first = (lane % HEAD_DIM) < half
    swapped = jnp.where(first, pltpu.roll(z, w - half, 1), pltpu.roll(z, half, 1))
    return z * cosw + swapped * sinw


def _head_rms(z, nw, gsum):
    z2 = z * z
    hi = z2.astype(BF16)
    lo = (z2 - hi.astype(F32)).astype(BF16)
    ssum = _dot(hi, gsum) + _dot(lo, gsum)
    return z * lax.rsqrt(ssum * (1.0 / HEAD_DIM) + NORM_EPS) * nw


def _inproj_body(tm, seq_rows, dc, x_ref, shift_ref, scale_ref, n1w_ref, w_ref, cos_ref, sin_ref, qnw_ref, knw_ref,
                 gsum_ref, convw_ref, wco_ref, ext1_ref, ext2_ref,
                 u_ref, q_ref, k_ref, v_ref, qi_ref, kiw_ref, yag_ref, sgb_ref, kt_ref, vt_ref, wit_ref, carry_ref,
                 buf_ref):
    i = pl.program_id(0)
    hq = N_HEADS * HEAD_DIM
    hk = N_KV_HEADS * HEAD_DIM
    hi_w = IDX_HEADS * IDX_DIM
    d = x_ref.shape[1]
    offs = np.cumsum([0, dc, dc, dc, hq, hk, hk, hi_w, LANES, d, d])

    def proj(s):
        return _dot(hb, w_ref[:, int(offs[s]):int(offs[s + 1])])

    x = x_ref[...]
    ms = jnp.mean(x * x, axis=-1, keepdims=True)
    h = x * lax.rsqrt(ms + NORM_EPS) * n1w_ref[...]
    h = h * (1.0 + scale_ref[...]) + shift_ref[...]
    hb = h.astype(BF16)

    u = proj(2) * proj(0)
    u_ref[...] = u

    @pl.when(i == 0)
    def _():
        carry_ref[...] = jnp.zeros_like(carry_ref)

    buf_ref[0:SUBLANES, :] = carry_ref[...]
    buf_ref[SUBLANES:SUBLANES + tm, :] = u
    carry_ref[...] = u[tm - SUBLANES:tm, :]
    pos = (i * tm + lax.broadcasted_iota(I32, (tm, 1), 0)) % seq_rows
    u1 = jnp.where(pos >= 1, buf_ref[SUBLANES - 1:SUBLANES - 1 + tm, :], ext1_ref[...])
    u2 = jnp.where(pos >= 2, buf_ref[SUBLANES - 2:SUBLANES - 2 + tm, :], ext2_ref[...])
    yc = convw_ref[2:3, :] * u + convw_ref[0:1, :] * u2 + convw_ref[1:2, :] * u1
    a_pre = proj(1) * yc
    ya = _dot(a_pre.astype(BF16), wco_ref[...])
    yag_ref[...] = jax.nn.sigmoid(proj(8)) * ya
    sgb_ref[...] = jax.nn.sigmoid(proj(9))

    cos = cos_ref[...]
    sin = sin_ref[...]
    gsum = gsum_ref[...]
    q_ref[...] = _rope_lanes(_head_rms(proj(3), qnw_ref[...], gsum), cos, sin)
    k = _rope_lanes(_head_rms(proj(4), knw_ref[:, 0:hk], gsum[0:hk, 0:hk]), cos, sin)
    k_ref[...] = k
    kt_ref[...] = k.T
    v = proj(5)
    v_ref[...] = v
    vt_ref[...] = v.T
    qi_ref[...] = _rope_lanes(proj(6), cos, sin)
    slab = proj(7)
    lane = lax.broadcasted_iota(I32, (1, LANES), 1)
    kiw = jnp.where(lane < IDX_DIM, _rope_lanes(slab, cos, sin), slab * IDX_SCALE)
    kiw_ref[...] = kiw
    wit_ref[...] = kiw.T[IDX_DIM:IDX_DIM + IDX_HEADS, :]


def _inproj(x2, shift, scale, n1w, w_pad, cos, sin, qnw, knw, gsum, convw, wco, ext1, ext2, *, tm, seq_rows,
            per_row_mod):
    rows, d = x2.shape
    dc = convw.shape[1]
    hq = N_HEADS * HEAD_DIM
    hk = N_KV_HEADS * HEAD_DIM
    hi_w = IDX_HEADS * IDX_DIM
    nt = rows // tm
    pos_tiles = cos.shape[0] // tm
    tiles_per_seq = max(seq_rows // tm, 1)
    if per_row_mod:
        mod_spec = pl.BlockSpec((tm, d), lambda i: (i, 0))
        ext_spec = pl.BlockSpec((tm, dc), lambda i: (i, 0))
    else:
        mod_spec = pl.BlockSpec((None, 1, d), lambda i: (i // tiles_per_seq, 0, 0))
        ext_spec = pl.BlockSpec((1, dc), lambda i: (0, 0))
    const = lambda shape: pl.BlockSpec(shape, lambda i: tuple(0 for _ in shape))
    row = lambda w: pl.BlockSpec((tm, w), lambda i: (i, 0))
    out_widths = [dc, hq, hk, hk, hi_w, LANES, d, d]
    n_seq = max(rows // (tiles_per_seq * tm), 1)
    kvt_spec = pl.BlockSpec((None, hk, tm), lambda i: (i // tiles_per_seq, 0, i % tiles_per_seq))
    kvt_shape = jax.ShapeDtypeStruct((n_seq, hk, tiles_per_seq * tm), F32)
    return pl.pallas_call(
        functools.partial(_inproj_body, tm, seq_rows, dc),
        grid=(nt,),
        in_specs=[row(d), mod_spec, mod_spec, const((1, d)), const(w_pad.shape),
                  pl.BlockSpec((tm, LANES), lambda i: (i % pos_tiles, 0)),
                  pl.BlockSpec((tm, LANES), lambda i: (i % pos_tiles, 0)),
                  const((1, hq)), const((1, hq)), const((hq, hq)), const(convw.shape), const(wco.shape),
                  ext_spec, ext_spec],
        out_specs=[row(w) for w in out_widths] + [kvt_spec, kvt_spec, pl.BlockSpec((IDX_HEADS, tm), lambda i: (0, i))],
        out_shape=[jax.ShapeDtypeStruct((rows, w), F32) for w in out_widths]
                  + [kvt_shape, kvt_shape, jax.ShapeDtypeStruct((IDX_HEADS, rows), F32)],
        scratch_shapes=[pltpu.VMEM((SUBLANES, dc), F32), pltpu.VMEM((tm + SUBLANES, dc), F32)],
        compiler_params=_cparams(("arbitrary",)),
    )(x2, shift, scale, n1w, w_pad, cos, sin, qnw, knw, gsum, convw, wco, ext1, ext2)


def _sortable_key(score):
    bits = lax.bitcast_convert_type(score, I32)
    key = jnp.where(bits < 0, bits ^ jnp.int32(0x7FFFFFFF), bits)
    return jnp.where(score == 0.0, jnp.int32(0), key)


def _kth_largest_key(count_ge, shape, k):
    ans = jnp.where(count_ge(jnp.zeros(shape, I32)) >= k, jnp.int32(0), jnp.int32(INT_MIN))

    def step(it, ans):
        cand = ans + jnp.left_shift(jnp.int32(1), jnp.int32(30) - it)
        return jnp.where(count_ge(cand) >= k, cand, ans)

    return lax.fori_loop(0, 31, step, ans)


def _tie_cut(count_eq_below, shape, need, nbits):
    def step(it, c):
        cand = c + jnp.left_shift(jnp.int32(1), jnp.int32(nbits - 1) - it)
        return jnp.where(count_eq_below(cand) < need, cand, c)

    return lax.fori_loop(0, nbits, step, jnp.zeros(shape, I32))


def _col_reduce(x, op, rows=64):
    n = x.shape[0]
    if n > rows and n % rows == 0:
        x = op(x.reshape(n // rows, rows, x.shape[1]), axis=0)
    return op(x, axis=0, keepdims=True)


def _pattn_class(tq, klen, topk, j, q_ref, qi_ref, wit_ref, k_ref, vt_ref, kiw_ref, o_ref, key_ref, cut_ref):
    qi = qi_ref[...]
    wit = wit_ref[...]
    ki = kiw_ref[0:klen, 0:IDX_DIM].astype(BF16)
    score = jnp.zeros((klen, tq), F32)
    hpm = 4
    for h0 in range(0, IDX_HEADS, hpm):
        qs = jnp.concatenate([qi[:, h * IDX_DIM:(h + 1) * IDX_DIM] for h in range(h0, h0 + hpm)], axis=0)
        s = _dot_nt(ki, qs.astype(BF16))
        for g in range(hpm):
            score = score + wit[h0 + g:h0 + g + 1, :] * jnp.maximum(s[:, g * tq:(g + 1) * tq], 0.0)
    kpos = lax.broadcasted_iota(I32, (klen, 1), 0)
    qpos = j * tq + lax.broadcasted_iota(I32, (1, tq), 1)
    causal = kpos <= qpos
    score = jnp.where(causal, score, NEG_INF)
    key_ref[0:klen, :] = _sortable_key(score)

    def count_ge(c):
        return _col_reduce((key_ref[0:klen, :] >= c).astype(F32), jnp.sum)

    thr = _kth_largest_key(count_ge, (1, tq), topk)
    key = key_ref[0:klen, :]
    gt = key > thr
    eq = key == thr
    need = topk - _col_reduce(gt.astype(F32), jnp.sum)
    n_eq = _col_reduce(eq.astype(F32), jnp.sum)
    cut_ref[...] = jnp.full(cut_ref.shape, klen, I32)

    @pl.when(jnp.max(n_eq - need) > 0)
    def _():
        def count_eq_below(c):
            hit = (key_ref[0:klen, :] == thr) & (kpos < c)
            return _col_reduce(hit.astype(F32), jnp.sum)

        cut = _tie_cut(count_eq_below, (1, tq), need, int(klen - 1).bit_length())
        cut_ref[...] = jnp.broadcast_to(cut, cut_ref.shape)

    cut = cut_ref[0:1, :]
    bias = jnp.where((gt | (eq & (kpos <= cut))) & causal, 0.0, NEG_INF)

    q = q_ref[...] * (HEAD_DIM ** -0.5)
    k = k_ref[0:klen, :].astype(BF16)
    group = N_HEADS // N_KV_HEADS
    outs = []
    for h in range(N_HEADS):
        n = h // group
        if h % group == 0:
            qs = jnp.concatenate([q[:, g * HEAD_DIM:(g + 1) * HEAD_DIM] for g in range(h, h + group)], axis=0)
            logits_n = _dot_nt(k[:, n * HEAD_DIM:(n + 1) * HEAD_DIM], qs.astype(BF16))
        logits = logits_n[:, (h % group) * tq:(h % group + 1) * tq] + bias
        m = _col_reduce(logits, jnp.max)
        p = jnp.exp(logits - m)
        l = _col_reduce(p, jnp.sum)
        vt = vt_ref[n * HEAD_DIM:(n + 1) * HEAD_DIM, 0:klen].astype(BF16)
        outs.append(_dot(vt, p.astype(BF16)) / l)
    o_ref[...] = jnp.concatenate(outs, axis=0).T


def _pattn_body(tq, t, topk, n_cls, q_ref, qi_ref, wit_ref, k_ref, vt_ref, kiw_ref, o_ref, key_ref, cut_ref):
    j = pl.program_id(1)
    blocks_per_cls = (t // tq) // n_cls
    for c in range(n_cls):
        @pl.when((j >= c * blocks_per_cls) & (j < (c + 1) * blocks_per_cls))
        def _(c=c):
            _pattn_class(tq, (c + 1) * blocks_per_cls * tq, topk, j, q_ref, qi_ref, wit_ref, k_ref, vt_ref, kiw_ref,
                         o_ref, key_ref, cut_ref)


def _pattn(q, qi, wit, k, vt, kiw, *, bsz, t):
    hq = q.shape[1]
    tq = Q_BLOCK
    nqb = t // tq
    n_cls = max(c for c in (1, 2, 4, 8) if nqb % c == 0)
    topk = min(TOPK_MAX, t // 4)
    qspec = lambda w: pl.BlockSpec((tq, w), lambda bi, j: (bi * nqb + j, 0))
    kspec = lambda w: pl.BlockSpec((t, w), lambda bi, j: (bi, 0))
    return pl.pallas_call(
        functools.partial(_pattn_body, tq, t, topk, n_cls),
        grid=(bsz, nqb),
        in_specs=[qspec(hq), qspec(qi.shape[1]),
                  pl.BlockSpec((wit.shape[0], tq), lambda bi, j: (0, bi * nqb + j)),
                  kspec(k.shape[1]),
                  pl.BlockSpec((None, vt.shape[1], t), lambda bi, j: (bi, 0, 0)),
                  kspec(kiw.shape[1])],
        out_specs=qspec(hq),
        out_shape=jax.ShapeDtypeStruct((bsz * t, hq), F32),
        scratch_shapes=[pltpu.VMEM((t, tq), I32), pltpu.VMEM((SUBLANES, tq), I32)],
        compiler_params=_cparams(("arbitrary", "arbitrary")),
    )(q, qi, wit, k, vt, kiw)


def _sattn_body(n_pages, g, ds, topk, pt_ref, qi_ref, w_ref, q_ref, kin_ref, kn_ref, vn_ref, *rest):
    kic, kc, vc = rest[0:g], rest[g:2 * g], rest[2 * g:3 * g]
    o_ref, score_ref, mask_ref, m_ref, l_ref, acc_ref, kicat_ref, kcat_ref, vcat_ref = rest[3 * g:]
    ph = pl.program_id(1)
    p = pl.program_id(2)
    n_steps = n_pages // g
    group = N_HEADS // N_KV_HEADS
    sub = lax.broadcasted_iota(I32, (SUBLANES, LANES), 0)
    lane = lax.broadcasted_iota(I32, (SUBLANES, LANES), 1)
    new_valid = (lane <= sub) & (lane < ds)

    def scores(ki_bf, npg):
        s = _dot(qi_ref[...].astype(BF16), ki_bf)
        w = w_ref[...]
        w = jnp.concatenate([w] * npg, axis=1) if npg > 1 else w
        acc = jnp.zeros((SUBLANES, npg * LANES), F32)
        for h in range(IDX_HEADS):
            r = slice(h * SUBLANES, (h + 1) * SUBLANES)
            acc = acc + w[r, :] * jnp.maximum(s[r, :], 0.0)
        return acc

    merged = n_steps == 1

    def in_phase(k):
        return (lambda f: f()) if merged else pl.when(ph == k)

    @in_phase(0)
    def _():
        for gi in range(g):
            kicat_ref[:, gi * LANES:(gi + 1) * LANES] = kic[gi][...].astype(BF16)
        sc = scores(kicat_ref[...], g)
        for gi in range(g):
            score_ref[p * g + gi] = sc[:, gi * LANES:(gi + 1) * LANES]

        @pl.when(p == 0)
        def _():
            score_ref[n_pages] = jnp.where(new_valid, scores(kin_ref[...].astype(BF16), 1), NEG_INF)

        @pl.when(p == n_steps - 1)
        def _():
            key = _sortable_key(score_ref[...])
            slab = lax.broadcasted_iota(I32, key.shape, 0)
            idx = slab * LANES + lax.broadcasted_iota(I32, key.shape, 2)

            def count(hit):
                x = hit.astype(F32)
                parts = [jnp.sum(x[s:s + 16], axis=0) for s in range(0, x.shape[0], 16)]
                c = functools.reduce(lambda u, v: u + v, parts)
                return jnp.sum(c, axis=1, keepdims=True)[None]

            thr = _kth_largest_key(lambda c: count(key >= c), (1, SUBLANES, 1), topk)
            gt = key > thr
            eq = key == thr
            need = topk - count(gt)
            valid = (slab < n_pages) | new_valid[None]
            mask_ref[...] = ((gt | eq) & valid).astype(F32)

            real_row = lax.broadcasted_iota(I32, (1, SUBLANES, 1), 1) < ds

            @pl.when(jnp.max(jnp.where(real_row, count(eq) - need, 0.0)) > 0)
            def _():
                nbits = int((n_pages + 1) * LANES - 1).bit_length()
                cut = _tie_cut(lambda c: count(eq & (idx < c)), (1, SUBLANES, 1), need, nbits)
                mask_ref[...] = ((gt | (eq & (idx <= cut))) & valid).astype(F32)

    def attend(mask8, kb, vb):
        maskg = jnp.concatenate([mask8] * group, axis=0) > 0.5
        for n in range(N_KV_HEADS):
            qn = (q_ref[n] * (HEAD_DIM ** -0.5)).astype(BF16)
            logits = _dot(qn, kb[n * HEAD_DIM:(n + 1) * HEAD_DIM, :])
            logits = jnp.where(maskg, logits, NEG_INF)
            m_old = m_ref[n][:, 0:1]
            m_new = jnp.maximum(m_old, jnp.max(logits, axis=1, keepdims=True))
            pr = jnp.where(maskg, jnp.exp(logits - m_new), 0.0)
            alpha = jnp.exp(m_old - m_new)
            l_new = alpha * l_ref[n][:, 0:1] + jnp.sum(pr, axis=1, keepdims=True)
            acc_ref[n] = alpha * acc_ref[n] + _dot_nt(pr.astype(BF16), vb[n * HEAD_DIM:(n + 1) * HEAD_DIM, :])
            m_ref[n] = jnp.broadcast_to(m_new, m_ref.shape[1:])
            l_ref[n] = jnp.broadcast_to(l_new, l_ref.shape[1:])

    @in_phase(1)
    def _():
        @pl.when(p == 0)
        def _():
            m_ref[...] = jnp.full(m_ref.shape, NEG_INF, F32)
            l_ref[...] = jnp.zeros(l_ref.shape, F32)
            acc_ref[...] = jnp.zeros(acc_ref.shape, F32)
            attend(mask_ref[n_pages], kn_ref[...].astype(BF16), vn_ref[...].astype(BF16))

        for gi in range(g):
            kcat_ref[:, gi * LANES:(gi + 1) * LANES] = kc[gi][...].astype(BF16)
            vcat_ref[:, gi * LANES:(gi + 1) * LANES] = vc[gi][...].astype(BF16)
        mask = jnp.concatenate([mask_ref[p * g + gi] for gi in range(g)], axis=1)
        attend(mask, kcat_ref[...], vcat_ref[...])

        @pl.when(p == n_steps - 1)
        def _():
            o_ref[...] = acc_ref[...] / l_ref[...][:, :, 0:HEAD_DIM]


def _sattn(page_table, qi_arr, w_arr, q_arr, kin, kn, vn, cache_kidx, cache_k, cache_v, *, ds):
    db, n_pages = page_table.shape
    page = cache_k.shape[2]
    assert page == LANES
    g = max(c for c in (1, 2, 4, 8, 16, 32, 64) if n_pages % c == 0)
    n_steps = n_pages // g
    topk = min(TOPK_MAX, (n_pages * page + ds) // 4)
    group = N_HEADS // N_KV_HEADS
    hk = N_KV_HEADS * HEAD_DIM
    rows_i = IDX_HEADS * SUBLANES
    rows_q = group * SUBLANES
    pt = page_table.reshape(-1)
    per_b = lambda *shape: pl.BlockSpec((None,) + shape, lambda b, ph, p, pt: (b,) + tuple(0 for _ in shape))

    def kic_map(gi, b, ph, p, pt):
        return (pt[b * n_pages + (p * (1 - ph) + (n_steps - 1) * ph) * g + gi], 0, 0)

    def kvc_map(gi, b, ph, p, pt):
        return (pt[b * n_pages + p * ph * g + gi], 0, 0)

    kic_specs = [pl.BlockSpec((None, IDX_DIM, page), functools.partial(kic_map, gi)) for gi in range(g)]
    kvc_specs = [pl.BlockSpec((None, hk, page), functools.partial(kvc_map, gi)) for gi in range(g)]
    grid_spec = pltpu.PrefetchScalarGridSpec(
        num_scalar_prefetch=1,
        grid=(db, 1 if n_steps == 1 else 2, n_steps),
        in_specs=[per_b(rows_i, IDX_DIM), per_b(rows_i, LANES), per_b(N_KV_HEADS, rows_q, HEAD_DIM),
                  per_b(IDX_DIM, page), per_b(hk, page), per_b(hk, page)] + kic_specs + kvc_specs + kvc_specs,
        out_specs=per_b(N_KV_HEADS, rows_q, HEAD_DIM),
        scratch_shapes=[pltpu.VMEM((n_pages + 1, SUBLANES, LANES), F32),
                        pltpu.VMEM((n_pages + 1, SUBLANES, LANES), F32),
                        pltpu.VMEM((N_KV_HEADS, rows_q, LANES), F32),
                        pltpu.VMEM((N_KV_HEADS, rows_q, LANES), F32),
                        pltpu.VMEM((N_KV_HEADS, rows_q, HEAD_DIM), F32),
                        pltpu.VMEM((IDX_DIM, g * page), BF16),
                        pltpu.VMEM((hk, g * page), BF16),
                        pltpu.VMEM((hk, g * page), BF16)],
    )
    return pl.pallas_call(
        functools.partial(_sattn_body, n_pages, g, ds, topk),
        grid_spec=grid_spec,
        out_shape=jax.ShapeDtypeStruct((db, N_KV_HEADS, rows_q, HEAD_DIM), F32),
        compiler_params=_cparams(("arbitrary", "arbitrary", "arbitrary")),
    )(pt, qi_arr, w_arr, q_arr, kin, kn, vn, *([cache_kidx] * g), *([cache_k] * g), *([cache_v] * g))


def _finish_body(x_ref, yag_ref, sgb_ref, attn_ref, gate_ref, shift_ref, scale_ref, n2w_ref, wao_ref, wout_ref, wq_ref,
                 x1_ref, h2t_ref, pq_ref):
    yb = _dot(attn_ref[...].astype(BF16), wao_ref[...])
    mix = yag_ref[...] + sgb_ref[...] * yb
    x1 = x_ref[...] + gate_ref[...] * _dot(mix.astype(BF16), wout_ref[...])
    x1_ref[...] = x1
    ms = jnp.mean(x1 * x1, axis=-1, keepdims=True)
    h2 = x1 * lax.rsqrt(ms + NORM_EPS) * n2w_ref[...]
    h2 = h2 * (1.0 + scale_ref[...]) + shift_ref[...]
    h2t_ref[...] = h2.T.astype(BF16)
    pq_ref[...] = _dot(h2.astype(BF16), wq_ref[...]).astype(BF16)


def _finish(x2, yag, sgb, attn, gate, shift, scale, n2w, wao, wout, wq, *, tm, seq_rows, per_row_mod):
    rows, d = x2.shape
    nt = rows // tm
    tiles_per_seq = max(seq_rows // tm, 1)
    if per_row_mod:
        mod_spec = pl.BlockSpec((tm, d), lambda i: (i, 0))
    else:
        mod_spec = pl.BlockSpec((None, 1, d), lambda i: (i // tiles_per_seq, 0, 0))
    const = lambda shape: pl.BlockSpec(shape, lambda i: tuple(0 for _ in shape))
    row = lambda w: pl.BlockSpec((tm, w), lambda i: (i, 0))
    nq = wq.shape[1]
    return pl.pallas_call(
        _finish_body,
        grid=(nt,),
        in_specs=[row(d), row(d), row(d), row(attn.shape[1]), mod_spec, mod_spec, mod_spec, const((1, d)),
                  const(wao.shape), const(wout.shape), const(wq.shape)],
        out_specs=[row(d), pl.BlockSpec((d, tm), lambda i: (0, i)), row(nq)],
        out_shape=[jax.ShapeDtypeStruct((rows, d), F32), jax.ShapeDtypeStruct((d, rows), BF16),
                   jax.ShapeDtypeStruct((rows, nq), BF16)],
        compiler_params=_cparams(("arbitrary",)),
    )(x2, yag, sgb, attn, gate, shift, scale, n2w, wao, wout, wq)


def _bitonic_pairs(n):
    pairs = []
    k = 2
    while k <= n:
        j = k // 2
        while j >= 1:
            for i in range(n):
                l = i ^ j
                if l > i:
                    pairs.append((i, l) if (i & k) == 0 else (l, i))
            j //= 2
        k *= 2
    return pairs


def _top_sorted(x, n):
    nv = 16
    rows, lanes = x.shape
    assert rows % SUBLANES == 0 and rows <= nv * SUBLANES and n <= nv + 1
    v = [x[q * SUBLANES:(q + 1) * SUBLANES, :] for q in range(rows // SUBLANES)]
    v += [jnp.full((SUBLANES, lanes), -jnp.inf, F32)] * (nv - len(v))
    for hi, lo in _bitonic_pairs(nv):
        a, b = v[hi], v[lo]
        v[hi], v[lo] = jnp.maximum(a, b), jnp.minimum(a, b)
    v.append(jnp.full((SUBLANES, lanes), -jnp.inf, F32))
    vals = []
    for r in range(n):
        m = jnp.max(v[0], axis=0, keepdims=True)
        vals.append(m)
        if r + 1 < n:
            won = v[0] >= m
            depth = min(n - 1 - r, nv)
            for q in range(depth):
                v[q] = jnp.where(won, v[q + 1], v[q])
    return jnp.concatenate(vals, axis=0)


def _gelu_tanh(x):
    c = np.float32(np.sqrt(2.0 / np.pi))
    return x * (0.5 * (1.0 + jnp.tanh(c * (x + 0.044715 * (x * x * x)))))


def _peer_select(lg, pq_ref, sk_ref, ct_ref, ea_ref, eb_ref):
    half = PEER_QDIM // 2
    rows = pl.ds(pl.multiple_of(lg * LANES, LANES), LANES)
    for h in range(PEER_HEADS):
        sa = _dot_nt(sk_ref[2 * h], pq_ref[rows, (2 * h) * half:(2 * h + 1) * half])
        sb = _dot_nt(sk_ref[2 * h + 1], pq_ref[rows, (2 * h + 1) * half:(2 * h + 2) * half])
        ta = _top_sorted(sa, PEER_TOPK + 1)
        tb = _top_sorted(sb, PEER_TOPK + 1)
        tail = jnp.concatenate([ta[0:1] + tb[16:17], ta[16:17] + tb[0:1],
                                jnp.full((SUBLANES - 2, LANES), -jnp.inf, F32)], axis=0)
        cand = jnp.concatenate(
            [ta[0:1] + tb[0:16]] + [ta[r:r + 1] + tb[0:8] for r in range(1, 8)] + [ta[8:16] + tb[0:1], tail], axis=0)
        top = _top_sorted(cand, PEER_TOPK + 1)
        thr = 0.5 * (top[PEER_TOPK - 1:PEER_TOPK] + top[PEER_TOPK:PEER_TOPK + 1])
        z = jnp.sum(jnp.exp(top[0:PEER_TOPK] - top[0:1]), axis=0, keepdims=True)
        ct_ref[lg, h] = jnp.exp((thr - tb[0:1]) - sa) / z
        ea_ref[lg, h] = jnp.exp(sa - ta[0:1])
        eb_ref[lg, h] = (jnp.exp(sb - tb[0:1]) / z).reshape(eb_ref.shape[2:])


def _peer_body(ce, tm, nc, h2t_ref, pq_ref, x1_ref, gate_ref, sk_ref, u_ref, vt_ref, out_ref,
               ct_ref, ea_ref, eb_ref, yt_ref, p_ref, act_ref):
    c = pl.program_id(1)
    kk = PEER_KEYS
    n_lg = tm // LANES
    n_sub = ce // kk
    pair = 2 if n_sub % 2 == 0 else 1
    jrows = kk // 2
    n_half = 2 if n_lg % 2 == 0 else 1
    hw = tm // n_half

    @pl.when(c == 0)
    def _():
        yt_ref[...] = jnp.zeros_like(yt_ref)

        def body(lg, carry):
            _peer_select(lg, pq_ref, sk_ref, ct_ref, ea_ref, eb_ref)
            return carry

        lax.fori_loop(0, n_lg, body, 0)

    def expert_act(n):
        act_ref[:, n * hw:(n + 1) * hw] = _dot(u_ref[...], h2t_ref[:, n * hw:(n + 1) * hw])

    expert_act(0)
    for n in range(n_half):
        for lg in range(n * n_lg // n_half, (n + 1) * n_lg // n_half):
            if n + 1 < n_half and lg == (n + 1) * n_lg // n_half - 1:
                expert_act(n + 1)
            cols = slice(lg * LANES, (lg + 1) * LANES)
            for ip in range(n_sub // pair):
                subs = [pair * ip + t for t in range(pair)]
                for jh in range(2):
                    vrows = slice(jh * jrows // SUBLANES, (jh + 1) * jrows // SUBLANES)
                    w = [None] * pair
                    for h in range(PEER_HEADS):
                        ebh = eb_ref[lg, h, vrows]
                        for t, s in enumerate(subs):
                            i = c * n_sub + s
                            ctv = jnp.broadcast_to(ct_ref[lg, h, pl.ds(i, 1), :], (SUBLANES, LANES))[None]
                            eav = jnp.broadcast_to(ea_ref[lg, h, pl.ds(i, 1), :], (SUBLANES, LANES))[None]
                            term = jnp.where(ebh >= ctv, ebh, 0.0) * eav
                            w[t] = term if w[t] is None else w[t] + term
                    for t, s in enumerate(subs):
                        rows = slice(s * kk + jh * jrows, s * kk + (jh + 1) * jrows)
                        prod = w[t].reshape(jrows, LANES) * _gelu_tanh(act_ref[rows, cols])
                        p_ref[rows, cols] = prod.astype(BF16)
        yt_ref[:, n * hw:(n + 1) * hw] += _dot(vt_ref[...], p_ref[:, n * hw:(n + 1) * hw])

    @pl.when(c == nc - 1)
    def _():
        out_ref[...] = x1_ref[...] + gate_ref[...] * yt_ref[...].T


def _peer(h2t, pq, x1, gate, sk, u_bf, vt_blk, *, tm, seq_rows, per_row_mod):
    rows, d = x1.shape
    ne = u_bf.shape[0]
    ce = vt_blk.shape[2]
    nt = rows // tm
    nc = ne // ce
    assert tm % LANES == 0 and rows % tm == 0 and ne % ce == 0 and ce % PEER_KEYS == 0
    tiles_per_seq = max(seq_rows // tm, 1)
    if per_row_mod:
        mod_spec = pl.BlockSpec((tm, d), lambda i, c: (i, 0))
    else:
        mod_spec = pl.BlockSpec((None, 1, d), lambda i, c: (i // tiles_per_seq, 0, 0))
    row = lambda w: pl.BlockSpec((tm, w), lambda i, c: (i, 0))
    sel_shape = (tm // LANES, PEER_HEADS, PEER_KEYS, LANES)
    vreg_shape = (tm // LANES, PEER_HEADS, PEER_KEYS // SUBLANES, SUBLANES, LANES)
    return pl.pallas_call(
        functools.partial(_peer_body, ce, tm, nc),
        grid=(nt, nc),
        in_specs=[pl.BlockSpec((d, tm), lambda i, c: (0, i)), row(pq.shape[1]), row(d), mod_spec,
                  pl.BlockSpec(sk.shape, lambda i, c: (0, 0, 0)),
                  pl.BlockSpec((ce, d), lambda i, c: (c, 0)),
                  pl.BlockSpec((None, d, ce), lambda i, c: (c, 0, 0))],
        out_specs=row(d),
        out_shape=jax.ShapeDtypeStruct((rows, d), F32),
        scratch_shapes=[pltpu.VMEM(sel_shape, F32), pltpu.VMEM(sel_shape, F32), pltpu.VMEM(vreg_shape, F32),
                        pltpu.VMEM((d, tm), F32), pltpu.VMEM((ce, tm), BF16), pltpu.VMEM((ce, tm), F32)],
        compiler_params=_cparams(("arbitrary", "arbitrary")),
    )(h2t, pq, x1, gate, sk, u_bf, vt_blk)


def _rope_tables(pos):
    half = HEAD_DIM // 2
    freqs = ROPE_THETA ** (-jnp.arange(half, dtype=F32) / half)
    ang = pos.astype(F32)[:, None] * freqs[None, :]
    cos, sin = jnp.cos(ang), jnp.sin(ang)
    cos2 = jnp.concatenate([cos, cos, cos, cos], axis=1)
    sin2 = jnp.concatenate([-sin, sin, -sin, sin], axis=1)
    return cos2, sin2


def _pick_tile(rows, pref):
    t = min(pref, rows)
    while rows % t:
        t //= 2
    return t


def kernel(x_prompt, x_sample, cache_k, cache_v, cache_kidx, state_conv, page_table, c_prompt, c_sample, w_ada, b_ada,
           norm1_w, w_in, conv_w, q_norm_w, k_norm_w, w_conv_out, w_attn_out, w_out, norm2_w, peer_w_q, peer_sub_keys,
           peer_u, peer_v):
    bsz, seq, d = x_prompt.shape
    db, ds, _ = x_sample.shape
    depth = w_ada.shape[0]
    dc = conv_w.shape[2]
    hq = N_HEADS * HEAD_DIM
    hk = N_KV_HEADS * HEAD_DIM
    hi_w = IDX_HEADS * IDX_DIM
    n_pages = page_table.shape[1]
    page = cache_k.shape[2]
    past = n_pages * page
    group = N_HEADS // N_KV_HEADS

    pos_p = jnp.arange(seq, dtype=I32)
    pos_s = past + jnp.arange(ds, dtype=I32)
    cos_p, sin_p = _rope_tables(pos_p)
    cos_s, sin_s = _rope_tables(jnp.tile(pos_s, db))
    gsum = jnp.kron(jnp.eye(N_HEADS, dtype=F32), jnp.ones((HEAD_DIM, HEAD_DIM), F32)).astype(BF16)

    xp = x_prompt.reshape(bsz * seq, d)
    xs = x_sample.reshape(db * ds, d)
    outs = {n: [] for n in ("kp", "vp", "kip", "cp", "ks", "vs", "kis", "cs")}

    for layer in range(depth):
        mod = _adaln(jnp.concatenate([c_prompt, c_sample], axis=0), w_ada[layer], b_ada[layer])
        mod_p = [m.reshape(bsz, 1, d) for m in jnp.split(mod[:bsz], N_MOD, axis=-1)]
        mod_s = [jnp.repeat(m, ds, axis=0) for m in jnp.split(mod[bsz:], N_MOD, axis=-1)]

        w = w_in[layer]
        o_ki = 3 * dc + hq + 2 * hk + hi_w
        o_g = o_ki + IDX_DIM + IDX_HEADS
        slab = jnp.concatenate([w[:, o_ki:o_g], jnp.zeros((d, LANES - IDX_DIM - IDX_HEADS), F32)], axis=1)
        w_pad = jnp.concatenate([w[:, :o_ki], slab, w[:, o_g:]], axis=1).astype(BF16)
        n1w = norm1_w[layer].reshape(1, d)
        qnw = jnp.tile(q_norm_w[layer], N_HEADS).reshape(1, hq)
        knw = jnp.tile(k_norm_w[layer], N_HEADS).reshape(1, hq)
        wco = w_conv_out[layer].astype(BF16)
        wao = w_attn_out[layer].astype(BF16)
        wout = w_out[layer].astype(BF16)
        wq = peer_w_q[layer].astype(BF16)
        n2w = norm2_w[layer].reshape(1, d)
        sk = jnp.transpose(peer_sub_keys[layer], (1, 0, 2, 3)).reshape(2 * PEER_HEADS, PEER_KEYS, PEER_QDIM // 2)
        sk = sk.astype(BF16)
        u_bf = peer_u[layer].astype(BF16)
        ce = min(2048, u_bf.shape[0])
        vt_bf = peer_v[layer].astype(BF16).reshape(-1, ce, d).transpose(0, 2, 1)

        tm_p = _pick_tile(seq, 256)
        zero_ext = jnp.zeros((1, dc), F32)
        u, q, k, v, qi, kiw, yag, sgb, kt, vt, wit = _inproj(
            xp, mod_p[0], mod_p[1], n1w, w_pad, cos_p, sin_p, qnw, knw, gsum, conv_w[layer], wco, zero_ext, zero_ext,
            tm=tm_p, seq_rows=seq, per_row_mod=False)
        ki = kiw[:, :IDX_DIM]
        attn = _pattn(q, qi, wit, k, vt, kiw, bsz=bsz, t=seq)
        tm_f = _pick_tile(seq, 512)
        x1, h2t, pq = _finish(xp, yag, sgb, attn, mod_p[2], mod_p[3], mod_p[4], n2w, wao, wout,
                              wq, tm=tm_f, seq_rows=seq, per_row_mod=False)
        xp = _peer(h2t, pq, x1, mod_p[5], sk, u_bf, vt_bf, tm=tm_f, seq_rows=seq, per_row_mod=False)
        outs["kp"].append(kt.reshape(bsz, N_KV_HEADS, HEAD_DIM, seq).transpose(0, 3, 1, 2))
        outs["vp"].append(vt.reshape(bsz, N_KV_HEADS, HEAD_DIM, seq).transpose(0, 3, 1, 2))
        outs["kip"].append(ki.reshape(bsz, seq, IDX_DIM))
        outs["cp"].append(u.reshape(bsz, seq, dc)[:, seq - 2:, :])

        rows_s = db * ds
        st = state_conv[layer]
        posr = jnp.tile(jnp.arange(ds), db)
        ext1 = jnp.repeat(st[:, 1, :], ds, axis=0)
        ext2 = jnp.where((posr == 0)[:, None], jnp.repeat(st[:, 0, :], ds, axis=0), ext1)
        u, q, k, v, qi, kiw, yag, sgb, _, _, _ = _inproj(
            xs, mod_s[0], mod_s[1], n1w, w_pad, cos_s, sin_s, qnw, knw, gsum, conv_w[layer], wco, ext1, ext2,
            tm=rows_s, seq_rows=ds, per_row_mod=True)
        ki = kiw[:, :IDX_DIM]
        wi = kiw[:, IDX_DIM:IDX_DIM + IDX_HEADS]
        pad_q = lambda a: jnp.pad(a, ((0, 0), (0, 0), (0, SUBLANES - ds), (0, 0)))
        qi_arr = pad_q(qi.reshape(db, ds, IDX_HEADS, IDX_DIM).transpose(0, 2, 1, 3)).reshape(db, IDX_HEADS * SUBLANES, IDX_DIM)
        w_arr = pad_q(wi.reshape(db, ds, IDX_HEADS, 1).transpose(0, 2, 1, 3)).reshape(db, IDX_HEADS * SUBLANES, 1)
        w_arr = jnp.broadcast_to(w_arr, (db, IDX_HEADS * SUBLANES, LANES))
        q_arr = pad_q(q.reshape(db, ds, N_HEADS, HEAD_DIM).transpose(0, 2, 1, 3)).reshape(db, N_KV_HEADS, group * SUBLANES, HEAD_DIM)
        pad_page = lambda a: jnp.pad(a.reshape(db, ds, -1), ((0, 0), (0, page - ds), (0, 0))).transpose(0, 2, 1)
        cki = jnp.transpose(cache_kidx[layer], (0, 2, 1))
        ck = jnp.transpose(cache_k[layer], (0, 2, 3, 1)).reshape(-1, hk, page)
        cv = jnp.transpose(cache_v[layer], (0, 2, 3, 1)).reshape(-1, hk, page)
        o = _sattn(page_table, qi_arr, w_arr, q_arr, pad_page(ki), pad_page(k), pad_page(v), cki, ck, cv, ds=ds)
        attn = o.reshape(db, N_HEADS, SUBLANES, HEAD_DIM)[:, :, :ds, :].transpose(0, 2, 1, 3).reshape(rows_s, hq)
        x1, h2t, pq = _finish(xs, yag, sgb, attn, mod_s[2], mod_s[3], mod_s[4], n2w, wao, wout, wq,
                              tm=rows_s, seq_rows=ds, per_row_mod=True)
        xs = _peer(h2t, pq, x1, mod_s[5], sk, u_bf, vt_bf, tm=rows_s, seq_rows=ds, per_row_mod=True)
        outs["ks"].append(k.reshape(db, ds, N_KV_HEADS, HEAD_DIM))
        outs["vs"].append(v.reshape(db, ds, N_KV_HEADS, HEAD_DIM))
        outs["kis"].append(ki.reshape(db, ds, IDX_DIM))
        full = jnp.concatenate([st, u.reshape(db, ds, dc)], axis=1)
        outs["cs"].append(full[:, full.shape[1] - 2:, :])

    stack = lambda n: jnp.stack(outs[n])
    return (xp.reshape(bsz, seq, d), xs.reshape(db, ds, d), stack("kp"), stack("vp"), stack("kip"), stack("cp"),
            stack("ks"), stack("vs"), stack("kis"), stack("cs"))
```

```python
import functools

import numpy as np
import jax
import jax.numpy as jnp
from jax import lax
from jax.experimental import pallas as pl
from jax.experimental.pallas import tpu as pltpu

F32 = jnp.float32
BF16 = jnp.bfloat16
I32 = jnp.int32

N_HEADS = 8
N_KV_HEADS = 2
HEAD_DIM = 64
IDX_HEADS = 8
IDX_DIM = 64
IDX_SCALE = (IDX_DIM ** -0.5) * (IDX_HEADS ** -0.5)
TOPK_MAX = 256
Q_BLOCK = 128
ROPE_THETA = 10000.0
PEER_HEADS = 8
PEER_KEYS = 128
PEER_QDIM = 256
PEER_TOPK = 16
N_MOD = 6
NORM_EPS = 1e-6
NEG_INF = -1e30
INT_MIN = -2 ** 31

LANES = 128
SUBLANES = 8
VMEM_LIMIT = 56 * 1024 * 1024


def _cparams(sem):
    return pltpu.CompilerParams(dimension_semantics=sem, vmem_limit_bytes=VMEM_LIMIT)


def _dot(a, b):
    return jnp.dot(a, b, preferred_element_type=F32)


def _dot_nt(a, b):
    return lax.dot_general(a, b, (((1,), (1,)), ((), ())), preferred_element_type=F32)


def _adaln_body(c_ref, w_ref, b_ref, o_ref):
    o_ref[...] = jnp.dot(c_ref[...], w_ref[...], preferred_element_type=F32,
                         precision=lax.Precision.HIGHEST) + b_ref[...]


def _adaln(c, w_ada, b_ada):
    m, d = c.shape
    n = w_ada.shape[1]
    tn = d
    return pl.pallas_call(
        _adaln_body,
        grid=(n // tn,),
        in_specs=[pl.BlockSpec((m, d), lambda j: (0, 0)),
                  pl.BlockSpec((d, tn), lambda j: (0, j)),
                  pl.BlockSpec((1, tn), lambda j: (0, j))],
        out_specs=pl.BlockSpec((m, tn), lambda j: (0, j)),
        out_shape=jax.ShapeDtypeStruct((m, n), F32),
        compiler_params=_cparams(("arbitrary",)),
    )(c, w_ada, b_ada.reshape(1, n))


def _rope_lanes(z, cos, sin_signed):
    w = z.shape[1]
    reps = w // LANES
    cosw = jnp.concatenate([cos] * reps, axis=1) if reps > 1 else cos
    sinw = jnp.concatenate([sin_signed] * reps, axis=1) if reps > 1 else sin_signed
    half = HEAD_DIM // 2
    lane = lax.broadcasted_iota(I32, (1, w), 1)
    first = (lane % HEAD_DIM) < half
    swapped = jnp.where(first, pltpu.roll(z, w - half, 1), pltpu.roll(z, half, 1))
    return z * cosw + swapped * sinw


def _head_rms(z, nw, gsum):
    z2 = z * z
    hi = z2.astype(BF16)
    lo = (z2 - hi.astype(F32)).astype(BF16)
    ssum = _dot(hi, gsum) + _dot(lo, gsum)
    return z * lax.rsqrt(ssum * (1.0 / HEAD_DIM) + NORM_EPS) * nw


def _inproj_body(tm, seq_rows, dc, x_ref, shift_ref, scale_ref, n1w_ref, w_ref, cos_ref, sin_ref, qnw_ref, knw_ref,
                 gsum_ref, convw_ref, wco_ref, ext1_ref, ext2_ref,
                 u_ref, q_ref, k_ref, v_ref, qi_ref, kiw_ref, yag_ref, sgb_ref, kt_ref, vt_ref, wit_ref, carry_ref,
                 buf_ref):
    i = pl.program_id(0)
    hq = N_HEADS * HEAD_DIM
    hk = N_KV_HEADS * HEAD_DIM
    hi_w = IDX_HEADS * IDX_DIM
    d = x_ref.shape[1]
    offs = np.cumsum([0, dc, dc, dc, hq, hk, hk, hi_w, LANES, d, d])

    def proj(s):
        return _dot(hb, w_ref[:, int(offs[s]):int(offs[s + 1])])

    x = x_ref[...]
    ms = jnp.mean(x * x, axis=-1, keepdims=True)
    h = x * lax.rsqrt(ms + NORM_EPS) * n1w_ref[...]
    h = h * (1.0 + scale_ref[...]) + shift_ref[...]
    hb = h.astype(BF16)

    u = proj(2) * proj(0)
    u_ref[...] = u

    @pl.when(i == 0)
    def _():
        carry_ref[...] = jnp.zeros_like(carry_ref)

    buf_ref[0:SUBLANES, :] = carry_ref[...]
    buf_ref[SUBLANES:SUBLANES + tm, :] = u
    carry_ref[...] = u[tm - SUBLANES:tm, :]
    pos = (i * tm + lax.broadcasted_iota(I32, (tm, 1), 0)) % seq_rows
    u1 = jnp.where(pos >= 1, buf_ref[SUBLANES - 1:SUBLANES - 1 + tm, :], ext1_ref[...])
    u2 = jnp.where(pos >= 2, buf_ref[SUBLANES - 2:SUBLANES - 2 + tm, :], ext2_ref[...])
    yc = convw_ref[2:3, :] * u + convw_ref[0:1, :] * u2 + convw_ref[1:2, :] * u1
    a_pre = proj(1) * yc
    ya = _dot(a_pre.astype(BF16), wco_ref[...])
    yag_ref[...] = jax.nn.sigmoid(proj(8)) * ya
    sgb_ref[...] = jax.nn.sigmoid(proj(9))

    cos = cos_ref[...]
    sin = sin_ref[...]
    gsum = gsum_ref[...]
    q_ref[...] = _rope_lanes(_head_rms(proj(3), qnw_ref[...], gsum), cos, sin)
    k = _rope_lanes(_head_rms(proj(4), knw_ref[:, 0:hk], gsum[0:hk, 0:hk]), cos, sin)
    k_ref[...] = k
    kt_ref[...] = k.T
    v = proj(5)
    v_ref[...] = v
    vt_ref[...] = v.T
    qi_ref[...] = _rope_lanes(proj(6), cos, sin)
    slab = proj(7)
    lane = lax.broadcasted_iota(I32, (1, LANES), 1)
    kiw = jnp.where(lane < IDX_DIM, _rope_lanes(slab, cos, sin), slab * IDX_SCALE)
    kiw_ref[...] = kiw
    wit_ref[...] = kiw.T[IDX_DIM:IDX_DIM + IDX_HEADS, :]


def _inproj(x2, shift, scale, n1w, w_pad, cos, sin, qnw, knw, gsum, convw, wco, ext1, ext2, *, tm, seq_rows,
            per_row_mod):
    rows, d = x2.shape
    dc = convw.shape[1]
    hq = N_HEADS * HEAD_DIM
    hk = N_KV_HEADS * HEAD_DIM
    hi_w = IDX_HEADS * IDX_DIM
    nt = rows // tm
    pos_tiles = cos.shape[0] // tm
    tiles_per_seq = max(seq_rows // tm, 1)
    if per_row_mod:
        mod_spec = pl.BlockSpec((tm, d), lambda i: (i, 0))
        ext_spec = pl.BlockSpec((tm, dc), lambda i: (i, 0))
    else:
        mod_spec = pl.BlockSpec((None, 1, d), lambda i: (i // tiles_per_seq, 0, 0))
        ext_spec = pl.BlockSpec((1, dc), lambda i: (0, 0))
    const = lambda shape: pl.BlockSpec(shape, lambda i: tuple(0 for _ in shape))
    row = lambda w: pl.BlockSpec((tm, w), lambda i: (i, 0))
    out_widths = [dc, hq, hk, hk, hi_w, LANES, d, d]
    n_seq = max(rows // (tiles_per_seq * tm), 1)
    kvt_spec = pl.BlockSpec((None, hk, tm), lambda i: (i // tiles_per_seq, 0, i % tiles_per_seq))
    kvt_shape = jax.ShapeDtypeStruct((n_seq, hk, tiles_per_seq * tm), F32)
    return pl.pallas_call(
        functools.partial(_inproj_body, tm, seq_rows, dc),
        grid=(nt,),
        in_specs=[row(d), mod_spec, mod_spec, const((1, d)), const(w_pad.shape),
                  pl.BlockSpec((tm, LANES), lambda i: (i % pos_tiles, 0)),
                  pl.BlockSpec((tm, LANES), lambda i: (i % pos_tiles, 0)),
                  const((1, hq)), const((1, hq)), const((hq, hq)), const(convw.shape), const(wco.shape),
                  ext_spec, ext_spec],
        out_specs=[row(w) for w in out_widths] + [kvt_spec, kvt_spec, pl.BlockSpec((IDX_HEADS, tm), lambda i: (0, i))],
        out_shape=[jax.ShapeDtypeStruct((rows, w), F32) for w in out_widths]
                  + [kvt_shape, kvt_shape, jax.ShapeDtypeStruct((IDX_HEADS, rows), F32)],
        scratch_shapes=[pltpu.VMEM((SUBLANES, dc), F32), pltpu.VMEM((tm + SUBLANES, dc), F32)],
        compiler_params=_cparams(("arbitrary",)),
    )(x2, shift, scale, n1w, w_pad, cos, sin, qnw, knw, gsum, convw, wco, ext1, ext2)


def _sortable_key(score):
    bits = lax.bitcast_convert_type(score, I32)
    key = jnp.where(bits < 0, bits ^ jnp.int32(0x7FFFFFFF), bits)
    return jnp.where(score == 0.0, jnp.int32(0), key)


def _kth_largest_key(count_ge, shape, k):
    ans = jnp.where(count_ge(jnp.zeros(shape, I32)) >= k, jnp.int32(0), jnp.int32(INT_MIN))

    def step(it, ans):
        cand = ans + jnp.left_shift(jnp.int32(1), jnp.int32(30) - it)
        return jnp.where(count_ge(cand) >= k, cand, ans)

    return lax.fori_loop(0, 31, step, ans)


def _tie_cut(count_eq_below, shape, need, nbits):
    def step(it, c):
        cand = c + jnp.left_shift(jnp.int32(1), jnp.int32(nbits - 1) - it)
        return jnp.where(count_eq_below(cand) < need, cand, c)

    return lax.fori_loop(0, nbits, step, jnp.zeros(shape, I32))


def _col_reduce(x, op, rows=64):
    n = x.shape[0]
    if n > rows and n % rows == 0:
        x = op(x.reshape(n // rows, rows, x.shape[1]), axis=0)
    return op(x, axis=0, keepdims=True)


def _pattn_class(tq, klen, topk, j, q_ref, qi_ref, wit_ref, k_ref, vt_ref, kiw_ref, o_ref, key_ref, cut_ref):
    qi = qi_ref[...]
    wit = wit_ref[...]
    ki = kiw_ref[0:klen, 0:IDX_DIM].astype(BF16)
    score = jnp.zeros((klen, tq), F32)
    hpm = 4
    for h0 in range(0, IDX_HEADS, hpm):
        qs = jnp.concatenate([qi[:, h * IDX_DIM:(h + 1) * IDX_DIM] for h in range(h0, h0 + hpm)], axis=0)
        s = _dot_nt(ki, qs.astype(BF16))
        for g in range(hpm):
            score = score + wit[h0 + g:h0 + g + 1, :] * jnp.maximum(s[:, g * tq:(g + 1) * tq], 0.0)
    kpos = lax.broadcasted_iota(I32, (klen, 1), 0)
    qpos = j * tq + lax.broadcasted_iota(I32, (1, tq), 1)
    causal = kpos <= qpos
    score = jnp.where(causal, score, NEG_INF)
    key_ref[0:klen, :] = _sortable_key(score)

    def count_ge(c):
        return _col_reduce((key_ref[0:klen, :] >= c).astype(F32), jnp.sum)

    thr = _kth_largest_key(count_ge, (1, tq), topk)
    key = key_ref[0:klen, :]
    gt = key > thr
    eq = key == thr
    need = topk - _col_reduce(gt.astype(F32), jnp.sum)
    n_eq = _col_reduce(eq.astype(F32), jnp.sum)
    cut_ref[...] = jnp.full(cut_ref.shape, klen, I32)

    @pl.when(jnp.max(n_eq - need) > 0)
    def _():
        def count_eq_below(c):
            hit = (key_ref[0:klen, :] == thr) & (kpos < c)
            return _col_reduce(hit.astype(F32), jnp.sum)

        cut = _tie_cut(count_eq_below, (1, tq), need, int(klen - 1).bit_length())
        cut_ref[...] = jnp.broadcast_to(cut, cut_ref.shape)

    cut = cut_ref[0:1, :]
    bias = jnp.where((gt | (eq & (kpos <= cut))) & causal, 0.0, NEG_INF)

    q = q_ref[...] * (HEAD_DIM ** -0.5)
    k = k_ref[0:klen, :].astype(BF16)
    group = N_HEADS // N_KV_HEADS
    outs = []
    for h in range(N_HEADS):
        n = h // group
        if h % group == 0:
            qs = jnp.concatenate([q[:, g * HEAD_DIM:(g + 1) * HEAD_DIM] for g in range(h, h + group)], axis=0)
            logits_n = _dot_nt(k[:, n * HEAD_DIM:(n + 1) * HEAD_DIM], qs.astype(BF16))
        logits = logits_n[:, (h % group) * tq:(h % group + 1) * tq] + bias
        m = _col_reduce(logits, jnp.max)
        p = jnp.exp(logits - m)
        l = _col_reduce(p, jnp.sum)
        vt = vt_ref[n * HEAD_DIM:(n + 1) * HEAD_DIM, 0:klen].astype(BF16)
        outs.append(_dot(vt, p.astype(BF16)) / l)
    o_ref[...] = jnp.concatenate(outs, axis=0).T


def _pattn_body(tq, t, topk, n_cls, q_ref, qi_ref, wit_ref, k_ref, vt_ref, kiw_ref, o_ref, key_ref, cut_ref):
    j = pl.program_id(1)
    blocks_per_cls = (t // tq) // n_cls
    for c in range(n_cls):
        @pl.when((j >= c * blocks_per_cls) & (j < (c + 1) * blocks_per_cls))
        def _(c=c):
            _pattn_class(tq, (c + 1) * blocks_per_cls * tq, topk, j, q_ref, qi_ref, wit_ref, k_ref, vt_ref, kiw_ref,
                         o_ref, key_ref, cut_ref)


def _pattn(q, qi, wit, k, vt, kiw, *, bsz, t):
    hq = q.shape[1]
    tq = Q_BLOCK
    nqb = t // tq
    n_cls = max(c for c in (1, 2, 4, 8, 16) if nqb % c == 0)
    topk = min(TOPK_MAX, t // 4)
    qspec = lambda w: pl.BlockSpec((tq, w), lambda bi, j: (bi * nqb + j, 0))
    kspec = lambda w: pl.BlockSpec((t, w), lambda bi, j: (bi, 0))
    return pl.pallas_call(
        functools.partial(_pattn_body, tq, t, topk, n_cls),
        grid=(bsz, nqb),
        in_specs=[qspec(hq), qspec(qi.shape[1]),
                  pl.BlockSpec((wit.shape[0], tq), lambda bi, j: (0, bi * nqb + j)),
                  kspec(k.shape[1]),
                  pl.BlockSpec((None, vt.shape[1], t), lambda bi, j: (bi, 0, 0)),
                  kspec(kiw.shape[1])],
        out_specs=qspec(hq),
        out_shape=jax.ShapeDtypeStruct((bsz * t, hq), F32),
        scratch_shapes=[pltpu.VMEM((t, tq), I32), pltpu.VMEM((SUBLANES, tq), I32)],
        compiler_params=_cparams(("arbitrary", "arbitrary")),
    )(q, qi, wit, k, vt, kiw)


def _sattn_body(n_pages, g, ds, topk, pt_ref, qi_ref, w_ref, q_ref, kin_ref, kn_ref, vn_ref, *rest):
    kic, kc, vc = rest[0:g], rest[g:2 * g], rest[2 * g:3 * g]
    o_ref, score_ref, mask_ref, m_ref, l_ref, acc_ref, kicat_ref, kcat_ref, vcat_ref = rest[3 * g:]
    ph = pl.program_id(1)
    p = pl.program_id(2)
    n_steps = n_pages // g
    group = N_HEADS // N_KV_HEADS
    sub = lax.broadcasted_iota(I32, (SUBLANES, LANES), 0)
    lane = lax.broadcasted_iota(I32, (SUBLANES, LANES), 1)
    new_valid = (lane <= sub) & (lane < ds)

    def scores(ki_bf, npg):
        s = _dot(qi_ref[...].astype(BF16), ki_bf)
        w = w_ref[...]
        w = jnp.concatenate([w] * npg, axis=1) if npg > 1 else w
        acc = jnp.zeros((SUBLANES, npg * LANES), F32)
        for h in range(IDX_HEADS):
            r = slice(h * SUBLANES, (h + 1) * SUBLANES)
            acc = acc + w[r, :] * jnp.maximum(s[r, :], 0.0)
        return acc

    merged = n_steps == 1

    def in_phase(k):
        return (lambda f: f()) if merged else pl.when(ph == k)

    @in_phase(0)
    def _():
        for gi in range(g):
            kicat_ref[:, gi * LANES:(gi + 1) * LANES] = kic[gi][...].astype(BF16)
        sc = scores(kicat_ref[...], g)
        for gi in range(g):
            score_ref[p * g + gi] = sc[:, gi * LANES:(gi + 1) * LANES]

        @pl.when(p == 0)
        def _():
            score_ref[n_pages] = jnp.where(new_valid, scores(kin_ref[...].astype(BF16), 1), NEG_INF)

        @pl.when(p == n_steps - 1)
        def _():
            key = _sortable_key(score_ref[...])
            slab = lax.broadcasted_iota(I32, key.shape, 0)
            idx = slab * LANES + lax.broadcasted_iota(I32, key.shape, 2)

            def count(hit):
                x = hit.astype(F32)
                parts = [jnp.sum(x[s:s + 16], axis=0) for s in range(0, x.shape[0], 16)]
                c = functools.reduce(lambda u, v: u + v, parts)
                return jnp.sum(c, axis=1, keepdims=True)[None]

            thr = _kth_largest_key(lambda c: count(key >= c), (1, SUBLANES, 1), topk)
            gt = key > thr
            eq = key == thr
            need = topk - count(gt)
            valid = (slab < n_pages) | new_valid[None]
            mask_ref[...] = ((gt | eq) & valid).astype(F32)

            real_row = lax.broadcasted_iota(I32, (1, SUBLANES, 1), 1) < ds

            @pl.when(jnp.max(jnp.where(real_row, count(eq) - need, 0.0)) > 0)
            def _():
                nbits = int((n_pages + 1) * LANES - 1).bit_length()
                cut = _tie_cut(lambda c: count(eq & (idx < c)), (1, SUBLANES, 1), need, nbits)
                mask_ref[...] = ((gt | (eq & (idx <= cut))) & valid).astype(F32)

    def attend(mask8, kb, vb):
        maskg = jnp.concatenate([mask8] * group, axis=0) > 0.5
        for n in range(N_KV_HEADS):
            qn = (q_ref[n] * (HEAD_DIM ** -0.5)).astype(BF16)
            logits = _dot(qn, kb[n * HEAD_DIM:(n + 1) * HEAD_DIM, :])
            logits = jnp.where(maskg, logits, NEG_INF)
            m_old = m_ref[n][:, 0:1]
            m_new = jnp.maximum(m_old, jnp.max(logits, axis=1, keepdims=True))
            pr = jnp.where(maskg, jnp.exp(logits - m_new), 0.0)
            alpha = jnp.exp(m_old - m_new)
            l_new = alpha * l_ref[n][:, 0:1] + jnp.sum(pr, axis=1, keepdims=True)
            acc_ref[n] = alpha * acc_ref[n] + _dot_nt(pr.astype(BF16), vb[n * HEAD_DIM:(n + 1) * HEAD_DIM, :])
            m_ref[n] = jnp.broadcast_to(m_new, m_ref.shape[1:])
            l_ref[n] = jnp.broadcast_to(l_new, l_ref.shape[1:])

    @in_phase(1)
    def _():
        @pl.when(p == 0)
        def _():
            m_ref[...] = jnp.full(m_ref.shape, NEG_INF, F32)
            l_ref[...] = jnp.zeros(l_ref.shape, F32)
            acc_ref[...] = jnp.zeros(acc_ref.shape, F32)
            attend(mask_ref[n_pages], kn_ref[...].astype(BF16), vn_ref[...].astype(BF16))

        for gi in range(g):
            kcat_ref[:, gi * LANES:(gi + 1) * LANES] = kc[gi][...].astype(BF16)
            vcat_ref[:, gi * LANES:(gi + 1) * LANES] = vc[gi][...].astype(BF16)
        mask = jnp.concatenate([mask_ref[p * g + gi] for gi in range(g)], axis=1)
        attend(mask, kcat_ref[...], vcat_ref[...])

        @pl.when(p == n_steps - 1)
        def _():
            o_ref[...] = acc_ref[...] / l_ref[...][:, :, 0:HEAD_DIM]


def _sattn(page_table, qi_arr, w_arr, q_arr, kin, kn, vn, cache_kidx, cache_k, cache_v, *, ds):
    db, n_pages = page_table.shape
    page = cache_k.shape[2]
    assert page == LANES
    g = max(c for c in (1, 2, 4, 8, 16, 32, 64) if n_pages % c == 0)
    n_steps = n_pages // g
    topk = min(TOPK_MAX, (n_pages * page + ds) // 4)
    group = N_HEADS // N_KV_HEADS
    hk = N_KV_HEADS * HEAD_DIM
    rows_i = IDX_HEADS * SUBLANES
    rows_q = group * SUBLANES
    pt = page_table.reshape(-1)
    per_b = lambda *shape: pl.BlockSpec((None,) + shape, lambda b, ph, p, pt: (b,) + tuple(0 for _ in shape))

    def kic_map(gi, b, ph, p, pt):
        return (pt[b * n_pages + (p * (1 - ph) + (n_steps - 1) * ph) * g + gi], 0, 0)

    def kvc_map(gi, b, ph, p, pt):
        return (pt[b * n_pages + p * ph * g + gi], 0, 0)

    kic_specs = [pl.BlockSpec((None, IDX_DIM, page), functools.partial(kic_map, gi)) for gi in range(g)]
    kvc_specs = [pl.BlockSpec((None, hk, page), functools.partial(kvc_map, gi)) for gi in range(g)]
    grid_spec = pltpu.PrefetchScalarGridSpec(
        num_scalar_prefetch=1,
        grid=(db, 1 if n_steps == 1 else 2, n_steps),
        in_specs=[per_b(rows_i, IDX_DIM), per_b(rows_i, LANES), per_b(N_KV_HEADS, rows_q, HEAD_DIM),
                  per_b(IDX_DIM, page), per_b(hk, page), per_b(hk, page)] + kic_specs + kvc_specs + kvc_specs,
        out_specs=per_b(N_KV_HEADS, rows_q, HEAD_DIM),
        scratch_shapes=[pltpu.VMEM((n_pages + 1, SUBLANES, LANES), F32),
                        pltpu.VMEM((n_pages + 1, SUBLANES, LANES), F32),
                        pltpu.VMEM((N_KV_HEADS, rows_q, LANES), F32),
                        pltpu.VMEM((N_KV_HEADS, rows_q, LANES), F32),
                        pltpu.VMEM((N_KV_HEADS, rows_q, HEAD_DIM), F32),
                        pltpu.VMEM((IDX_DIM, g * page), BF16),
                        pltpu.VMEM((hk, g * page), BF16),
                        pltpu.VMEM((hk, g * page), BF16)],
    )
    return pl.pallas_call(
        functools.partial(_sattn_body, n_pages, g, ds, topk),
        grid_spec=grid_spec,
        out_shape=jax.ShapeDtypeStruct((db, N_KV_HEADS, rows_q, HEAD_DIM), F32),
        compiler_params=_cparams(("arbitrary", "arbitrary", "arbitrary")),
    )(pt, qi_arr, w_arr, q_arr, kin, kn, vn, *([cache_kidx] * g), *([cache_k] * g), *([cache_v] * g))


def _finish_body(x_ref, yag_ref, sgb_ref, attn_ref, gate_ref, shift_ref, scale_ref, n2w_ref, wao_ref, wout_ref, wq_ref,
                 x1_ref, h2t_ref, pq_ref):
    yb = _dot(attn_ref[...].astype(BF16), wao_ref[...])
    mix = yag_ref[...] + sgb_ref[...] * yb
    x1 = x_ref[...] + gate_ref[...] * _dot(mix.astype(BF16), wout_ref[...])
    x1_ref[...] = x1
    ms = jnp.mean(x1 * x1, axis=-1, keepdims=True)
    h2 = x1 * lax.rsqrt(ms + NORM_EPS) * n2w_ref[...]
    h2 = h2 * (1.0 + scale_ref[...]) + shift_ref[...]
    h2t_ref[...] = h2.T.astype(BF16)
    pq_ref[...] = _dot(h2.astype(BF16), wq_ref[...]).astype(BF16)


def _finish(x2, yag, sgb, attn, gate, shift, scale, n2w, wao, wout, wq, *, tm, seq_rows, per_row_mod):
    rows, d = x2.shape
    nt = rows // tm
    tiles_per_seq = max(seq_rows // tm, 1)
    if per_row_mod:
        mod_spec = pl.BlockSpec((tm, d), lambda i: (i, 0))
    else:
        mod_spec = pl.BlockSpec((None, 1, d), lambda i: (i // tiles_per_seq, 0, 0))
    const = lambda shape: pl.BlockSpec(shape, lambda i: tuple(0 for _ in shape))
    row = lambda w: pl.BlockSpec((tm, w), lambda i: (i, 0))
    nq = wq.shape[1]
    return pl.pallas_call(
        _finish_body,
        grid=(nt,),
        in_specs=[row(d), row(d), row(d), row(attn.shape[1]), mod_spec, mod_spec, mod_spec, const((1, d)),
                  const(wao.shape), const(wout.shape), const(wq.shape)],
        out_specs=[row(d), pl.BlockSpec((d, tm), lambda i: (0, i)), row(nq)],
        out_shape=[jax.ShapeDtypeStruct((rows, d), F32), jax.ShapeDtypeStruct((d, rows), BF16),
                   jax.ShapeDtypeStruct((rows, nq), BF16)],
        compiler_params=_cparams(("arbitrary",)),
    )(x2, yag, sgb, attn, gate, shift, scale, n2w, wao, wout, wq)


def _bitonic_pairs(n):
    pairs = []
    k = 2
    while k <= n:
        j = k // 2
        while j >= 1:
            for i in range(n):
                l = i ^ j
                if l > i:
                    pairs.append((i, l) if (i & k) == 0 else (l, i))
            j //= 2
        k *= 2
    return pairs


def _top_sorted(x, n):
    nv = 16
    rows, lanes = x.shape
    assert rows % SUBLANES == 0 and rows <= nv * SUBLANES and n <= nv + 1
    v = [x[q * SUBLANES:(q + 1) * SUBLANES, :] for q in range(rows // SUBLANES)]
    v += [jnp.full((SUBLANES, lanes), -jnp.inf, F32)] * (nv - len(v))
    for hi, lo in _bitonic_pairs(nv):
        a, b = v[hi], v[lo]
        v[hi], v[lo] = jnp.maximum(a, b), jnp.minimum(a, b)
    v.append(jnp.full((SUBLANES, lanes), -jnp.inf, F32))
    vals = []
    for r in range(n):
        m = jnp.max(v[0], axis=0, keepdims=True)
        vals.append(m)
        if r + 1 < n:
            won = v[0] >= m
            depth = min(n - 1 - r, nv)
            for q in range(depth):
                v[q] = jnp.where(won, v[q + 1], v[q])
    return jnp.concatenate(vals, axis=0)


def _gelu_tanh(x):
    c = np.float32(np.sqrt(2.0 / np.pi))
    return x * (0.5 * (1.0 + jnp.tanh(c * (x + 0.044715 * (x * x * x)))))


def _peer_select(lg, pq_ref, sk_ref, ct_ref, ea_ref, eb_ref):
    half = PEER_QDIM // 2
    rows = pl.ds(pl.multiple_of(lg * LANES, LANES), LANES)
    for h in range(PEER_HEADS):
        sa = _dot_nt(sk_ref[2 * h], pq_ref[rows, (2 * h) * half:(2 * h + 1) * half])
        sb = _dot_nt(sk_ref[2 * h + 1], pq_ref[rows, (2 * h + 1) * half:(2 * h + 2) * half])
        ta = _top_sorted(sa, PEER_TOPK + 1)
        tb = _top_sorted(sb, PEER_TOPK + 1)
        tail = jnp.concatenate([ta[0:1] + tb[16:17], ta[16:17] + tb[0:1],
                                jnp.full((SUBLANES - 2, LANES), -jnp.inf, F32)], axis=0)
        cand = jnp.concatenate(
            [ta[0:1] + tb[0:16]] + [ta[r:r + 1] + tb[0:8] for r in range(1, 8)] + [ta[8:16] + tb[0:1], tail], axis=0)
        top = _top_sorted(cand, PEER_TOPK + 1)
        thr = 0.5 * (top[PEER_TOPK - 1:PEER_TOPK] + top[PEER_TOPK:PEER_TOPK + 1])
        z = jnp.sum(jnp.exp(top[0:PEER_TOPK] - top[0:1]), axis=0, keepdims=True)
        ct_ref[lg, h] = jnp.exp((thr - tb[0:1]) - sa) / z
        ea_ref[lg, h] = jnp.exp(sa - ta[0:1])
        eb_ref[lg, h] = (jnp.exp(sb - tb[0:1]) / z).reshape(eb_ref.shape[2:])


def _peer_body(ce, tm, nc, h2t_ref, pq_ref, x1_ref, gate_ref, sk_ref, u_ref, vt_ref, out_ref,
               ct_ref, ea_ref, eb_ref, yt_ref, p_ref, act_ref):
    c = pl.program_id(1)
    kk = PEER_KEYS
    n_lg = tm // LANES
    n_sub = ce // kk
    pair = 2 if n_sub % 2 == 0 else 1
    jrows = kk // 2
    n_half = 2 if n_lg % 2 == 0 else 1
    hw = tm // n_half

    @pl.when(c == 0)
    def _():
        yt_ref[...] = jnp.zeros_like(yt_ref)

        def body(lg, carry):
            _peer_select(lg, pq_ref, sk_ref, ct_ref, ea_ref, eb_ref)
            return carry

        lax.fori_loop(0, n_lg, body, 0)

    for n in range(n_half):
        act_ref[:, n * hw:(n + 1) * hw] = _dot(u_ref[...], h2t_ref[:, n * hw:(n + 1) * hw])
    for n in range(n_half):
        for lg in range(n * n_lg // n_half, (n + 1) * n_lg // n_half):
            cols = slice(lg * LANES, (lg + 1) * LANES)
            for ip in range(n_sub // pair):
                subs = [pair * ip + t for t in range(pair)]
                for jh in range(2):
                    vrows = slice(jh * jrows // SUBLANES, (jh + 1) * jrows // SUBLANES)
                    w = [None] * pair
                    for h in range(PEER_HEADS):
                        ebh = eb_ref[lg, h, vrows]
                        for t, s in enumerate(subs):
                            i = c * n_sub + s
                            ctv = jnp.broadcast_to(ct_ref[lg, h, pl.ds(i, 1), :], (SUBLANES, LANES))[None]
                            eav = jnp.broadcast_to(ea_ref[lg, h, pl.ds(i, 1), :], (SUBLANES, LANES))[None]
                            term = jnp.where(ebh >= ctv, ebh, 0.0) * eav
                            w[t] = term if w[t] is None else w[t] + term
                    for t, s in enumerate(subs):
                        rows = slice(s * kk + jh * jrows, s * kk + (jh + 1) * jrows)
                        prod = w[t].reshape(jrows, LANES) * _gelu_tanh(act_ref[rows, cols])
                        p_ref[rows, cols] = prod.astype(BF16)
        yt_ref[:, n * hw:(n + 1) * hw] += _dot(vt_ref[...], p_ref[:, n * hw:(n + 1) * hw])

    @pl.when(c == nc - 1)
    def _():
        out_ref[...] = x1_ref[...] + gate_ref[...] * yt_ref[...].T


def _peer(h2t, pq, x1, gate, sk, u_bf, vt_blk, *, tm, seq_rows, per_row_mod):
    rows, d = x1.shape
    ne = u_bf.shape[0]
    ce = vt_blk.shape[2]
    nt = rows // tm
    nc = ne // ce
    assert tm % LANES == 0 and rows % tm == 0 and ne % ce == 0 and ce % PEER_KEYS == 0
    tiles_per_seq = max(seq_rows // tm, 1)
    if per_row_mod:
        mod_spec = pl.BlockSpec((tm, d), lambda i, c: (i, 0))
    else:
        mod_spec = pl.BlockSpec((None, 1, d), lambda i, c: (i // tiles_per_seq, 0, 0))
    row = lambda w: pl.BlockSpec((tm, w), lambda i, c: (i, 0))
    sel_shape = (tm // LANES, PEER_HEADS, PEER_KEYS, LANES)
    vreg_shape = (tm // LANES, PEER_HEADS, PEER_KEYS // SUBLANES, SUBLANES, LANES)
    return pl.pallas_call(
        functools.partial(_peer_body, ce, tm, nc),
        grid=(nt, nc),
        in_specs=[pl.BlockSpec((d, tm), lambda i, c: (0, i)), row(pq.shape[1]), row(d), mod_spec,
                  pl.BlockSpec(sk.shape, lambda i, c: (0, 0, 0)),
                  pl.BlockSpec((ce, d), lambda i, c: (c, 0)),
                  pl.BlockSpec((None, d, ce), lambda i, c: (c, 0, 0))],
        out_specs=row(d),
        out_shape=jax.ShapeDtypeStruct((rows, d), F32),
        scratch_shapes=[pltpu.VMEM(sel_shape, F32), pltpu.VMEM(sel_shape, F32), pltpu.VMEM(vreg_shape, F32),
                        pltpu.VMEM((d, tm), F32), pltpu.VMEM((ce, tm), BF16), pltpu.VMEM((ce, tm), F32)],
        compiler_params=_cparams(("arbitrary", "arbitrary")),
    )(h2t, pq, x1, gate, sk, u_bf, vt_blk)


def _rope_tables(pos):
    half = HEAD_DIM // 2
    freqs = ROPE_THETA ** (-jnp.arange(half, dtype=F32) / half)
    ang = pos.astype(F32)[:, None] * freqs[None, :]
    cos, sin = jnp.cos(ang), jnp.sin(ang)
    cos2 = jnp.concatenate([cos, cos, cos, cos], axis=1)
    sin2 = jnp.concatenate([-sin, sin, -sin, sin], axis=1)
    return cos2, sin2


def _pick_tile(rows, pref):
    t = min(pref, rows)
    while rows % t:
        t //= 2
    return t


def kernel(x_prompt, x_sample, cache_k, cache_v, cache_kidx, state_conv, page_table, c_prompt, c_sample, w_ada, b_ada,
           norm1_w, w_in, conv_w, q_norm_w, k_norm_w, w_conv_out, w_attn_out, w_out, norm2_w, peer_w_q, peer_sub_keys,
           peer_u, peer_v):
    bsz, seq, d = x_prompt.shape
    db, ds, _ = x_sample.shape
    depth = w_ada.shape[0]
    dc = conv_w.shape[2]
    hq = N_HEADS * HEAD_DIM
    hk = N_KV_HEADS * HEAD_DIM
    hi_w = IDX_HEADS * IDX_DIM
    n_pages = page_table.shape[1]
    page = cache_k.shape[2]
    past = n_pages * page
    group = N_HEADS // N_KV_HEADS

    pos_p = jnp.arange(seq, dtype=I32)
    pos_s = past + jnp.arange(ds, dtype=I32)
    cos_p, sin_p = _rope_tables(pos_p)
    cos_s, sin_s = _rope_tables(jnp.tile(pos_s, db))
    gsum = jnp.kron(jnp.eye(N_HEADS, dtype=F32), jnp.ones((HEAD_DIM, HEAD_DIM), F32)).astype(BF16)

    xp = x_prompt.reshape(bsz * seq, d)
    xs = x_sample.reshape(db * ds, d)
    outs = {n: [] for n in ("kp", "vp", "kip", "cp", "ks", "vs", "kis", "cs")}

    for layer in range(depth):
        mod = _adaln(jnp.concatenate([c_prompt, c_sample], axis=0), w_ada[layer], b_ada[layer])
        mod_p = [m.reshape(bsz, 1, d) for m in jnp.split(mod[:bsz], N_MOD, axis=-1)]
        mod_s = [jnp.repeat(m, ds, axis=0) for m in jnp.split(mod[bsz:], N_MOD, axis=-1)]

        w = w_in[layer]
        o_ki = 3 * dc + hq + 2 * hk + hi_w
        o_g = o_ki + IDX_DIM + IDX_HEADS
        slab = jnp.concatenate([w[:, o_ki:o_g], jnp.zeros((d, LANES - IDX_DIM - IDX_HEADS), F32)], axis=1)
        w_pad = jnp.concatenate([w[:, :o_ki], slab, w[:, o_g:]], axis=1).astype(BF16)
        n1w = norm1_w[layer].reshape(1, d)
        qnw = jnp.tile(q_norm_w[layer], N_HEADS).reshape(1, hq)
        knw = jnp.tile(k_norm_w[layer], N_HEADS).reshape(1, hq)
        wco = w_conv_out[layer].astype(BF16)
        wao = w_attn_out[layer].astype(BF16)
        wout = w_out[layer].astype(BF16)
        wq = peer_w_q[layer].astype(BF16)
        n2w = norm2_w[layer].reshape(1, d)
        sk = jnp.transpose(peer_sub_keys[layer], (1, 0, 2, 3)).reshape(2 * PEER_HEADS, PEER_KEYS, PEER_QDIM // 2)
        sk = sk.astype(BF16)
        u_bf = peer_u[layer].astype(BF16)
        ce = min(2048, u_bf.shape[0])
        vt_bf = peer_v[layer].astype(BF16).reshape(-1, ce, d).transpose(0, 2, 1)

        tm_p = _pick_tile(seq, 512)
        zero_ext = jnp.zeros((1, dc), F32)
        u, q, k, v, qi, kiw, yag, sgb, kt, vt, wit = _inproj(
            xp, mod_p[0], mod_p[1], n1w, w_pad, cos_p, sin_p, qnw, knw, gsum, conv_w[layer], wco, zero_ext, zero_ext,
            tm=tm_p, seq_rows=seq, per_row_mod=False)
        ki = kiw[:, :IDX_DIM]
        attn = _pattn(q, qi, wit, k, vt, kiw, bsz=bsz, t=seq)
        tm_f = _pick_tile(seq, 512)
        x1, h2t, pq = _finish(xp, yag, sgb, attn, mod_p[2], mod_p[3], mod_p[4], n2w, wao, wout,
                              wq, tm=tm_f, seq_rows=seq, per_row_mod=False)
        xp = _peer(h2t, pq, x1, mod_p[5], sk, u_bf, vt_bf, tm=tm_f, seq_rows=seq, per_row_mod=False)
        outs["kp"].append(kt.reshape(bsz, N_KV_HEADS, HEAD_DIM, seq).transpose(0, 3, 1, 2))
        outs["vp"].append(vt.reshape(bsz, N_KV_HEADS, HEAD_DIM, seq).transpose(0, 3, 1, 2))
        outs["kip"].append(ki.reshape(bsz, seq, IDX_DIM))
        outs["cp"].append(u.reshape(bsz, seq, dc)[:, seq - 2:, :])

        rows_s = db * ds
        st = state_conv[layer]
        posr = jnp.tile(jnp.arange(ds), db)
        ext1 = jnp.repeat(st[:, 1, :], ds, axis=0)
        ext2 = jnp.where((posr == 0)[:, None], jnp.repeat(st[:, 0, :], ds, axis=0), ext1)
        u, q, k, v, qi, kiw, yag, sgb, _, _, _ = _inproj(
            xs, mod_s[0], mod_s[1], n1w, w_pad, cos_s, sin_s, qnw, knw, gsum, conv_w[layer], wco, ext1, ext2,
            tm=rows_s, seq_rows=ds, per_row_mod=True)
        ki = kiw[:, :IDX_DIM]
        wi = kiw[:, IDX_DIM:IDX_DIM + IDX_HEADS]
        pad_q = lambda a: jnp.pad(a, ((0, 0), (0, 0), (0, SUBLANES - ds), (0, 0)))
        qi_arr = pad_q(qi.reshape(db, ds, IDX_HEADS, IDX_DIM).transpose(0, 2, 1, 3)).reshape(db, IDX_HEADS * SUBLANES, IDX_DIM)
        w_arr = pad_q(wi.reshape(db, ds, IDX_HEADS, 1).transpose(0, 2, 1, 3)).reshape(db, IDX_HEADS * SUBLANES, 1)
        w_arr = jnp.broadcast_to(w_arr, (db, IDX_HEADS * SUBLANES, LANES))
        q_arr = pad_q(q.reshape(db, ds, N_HEADS, HEAD_DIM).transpose(0, 2, 1, 3)).reshape(db, N_KV_HEADS, group * SUBLANES, HEAD_DIM)
        pad_page = lambda a: jnp.pad(a.reshape(db, ds, -1), ((0, 0), (0, page - ds), (0, 0))).transpose(0, 2, 1)
        cki = jnp.transpose(cache_kidx[layer], (0, 2, 1))
        ck = jnp.transpose(cache_k[layer], (0, 2, 3, 1)).reshape(-1, hk, page)
        cv = jnp.transpose(cache_v[layer], (0, 2, 3, 1)).reshape(-1, hk, page)
        o = _sattn(page_table, qi_arr, w_arr, q_arr, pad_page(ki), pad_page(k), pad_page(v), cki, ck, cv, ds=ds)
        attn = o.reshape(db, N_HEADS, SUBLANES, HEAD_DIM)[:, :, :ds, :].transpose(0, 2, 1, 3).reshape(rows_s, hq)
        x1, h2t, pq = _finish(xs, yag, sgb, attn, mod_s[2], mod_s[3], mod_s[4], n2w, wao, wout, wq,
                              tm=rows_s, seq_rows=ds, per_row_mod=True)
        xs = _peer(h2t, pq, x1, mod_s[5], sk, u_bf, vt_bf, tm=rows_s, seq_rows=ds, per_row_mod=True)
        outs["ks"].append(k.reshape(db, ds, N_KV_HEADS, HEAD_DIM))
        outs["vs"].append(v.reshape(db, ds, N_KV_HEADS, HEAD_DIM))
        outs["kis"].append(ki.reshape(db, ds, IDX_DIM))
        full = jnp.concatenate([st, u.reshape(db, ds, dc)], axis=1)
        outs["cs"].append(full[:, full.shape[1] - 2:, :])

    stack = lambda n: jnp.stack(outs[n])
    return (xp.reshape(bsz, seq, d), xs.reshape(db, ds, d), stack("kp"), stack("vp"), stack("kip"), stack("cp"),
            stack("ks"), stack("vs"), stack("kis"), stack("cs"))
```

```python
import functools

import numpy as np
import jax
import jax.numpy as jnp
from jax import lax
from jax.experimental import pallas as pl
from jax.experimental.pallas import tpu as pltpu

F32 = jnp.float32
BF16 = jnp.bfloat16
I32 = jnp.int32

N_HEADS = 8
N_KV_HEADS = 2
HEAD_DIM = 64
IDX_HEADS = 8
IDX_DIM = 64
IDX_SCALE = (IDX_DIM ** -0.5) * (IDX_HEADS ** -0.5)
TOPK_MAX = 256
Q_BLOCK = 128
ROPE_THETA = 10000.0
PEER_HEADS = 8
PEER_KEYS = 128
PEER_QDIM = 256
PEER_TOPK = 16
N_MOD = 6
NORM_EPS = 1e-6
NEG_INF = -1e30
INT_MIN = -2 ** 31

LANES = 128
SUBLANES = 8
VMEM_LIMIT = 56 * 1024 * 1024


def _cparams(sem):
    return pltpu.CompilerParams(dimension_semantics=sem, vmem_limit_bytes=VMEM_LIMIT)


def _dot(a, b):
    return jnp.dot(a, b, preferred_element_type=F32)


def _dot_nt(a, b):
    return lax.dot_general(a, b, (((1,), (1,)), ((), ())), preferred_element_type=F32)


def _adaln_body(c_ref, w_ref, b_ref, o_ref):
    o_ref[...] = jnp.dot(c_ref[...], w_ref[...], preferred_element_type=F32,
                         precision=lax.Precision.HIGHEST) + b_ref[...]


def _adaln(c, w_ada, b_ada):
    m, d = c.shape
    n = w_ada.shape[1]
    tn = d
    return pl.pallas_call(
        _adaln_body,
        grid=(n // tn,),
        in_specs=[pl.BlockSpec((m, d), lambda j: (0, 0)),
                  pl.BlockSpec((d, tn), lambda j: (0, j)),
                  pl.BlockSpec((1, tn), lambda j: (0, j))],
        out_specs=pl.BlockSpec((m, tn), lambda j: (0, j)),
        out_shape=jax.ShapeDtypeStruct((m, n), F32),
        compiler_params=_cparams(("arbitrary",)),
    )(c, w_ada, b_ada.reshape(1, n))


def _rope_lanes(z, cos, sin_signed):
    w = z.shape[1]
    reps = w // LANES
    cosw = jnp.concatenate([cos] * reps, axis=1) if reps > 1 else cos
    sinw = jnp.concatenate([sin_signed] * reps, axis=1) if reps > 1 else sin_signed
    half = HEAD_DIM // 2
    lane = lax.broadcasted_iota(I32, (1, w), 1)
    first = (lane % HEAD_DIM) < half
    swapped = jnp.where(first, pltpu.roll(z, w - half, 1), pltpu.roll(z, half, 1))
    return z * cosw + swapped * sinw


def _head_rms(z, nw, gsum):
    z2 = z * z
    hi = z2.astype(BF16)
    lo = (z2 - hi.astype(F32)).astype(BF16)
    ssum = _dot(hi, gsum) + _dot(lo, gsum)
    return z * lax.rsqrt(ssum * (1.0 / HEAD_DIM) + NORM_EPS) * nw


def _inproj_body(tm, seq_rows, dc, x_ref, shift_ref, scale_ref, n1w_ref, w_ref, cos_ref, sin_ref, qnw_ref, knw_ref,
                 gsum_ref, convw_ref, wco_ref, ext1_ref, ext2_ref,
                 u_ref, q_ref, k_ref, v_ref, qi_ref, kiw_ref, yag_ref, sgb_ref, kt_ref, vt_ref, wit_ref, carry_ref,
                 buf_ref):
    i = pl.program_id(0)
    hq = N_HEADS * HEAD_DIM
    hk = N_KV_HEADS * HEAD_DIM
    hi_w = IDX_HEADS * IDX_DIM
    d = x_ref.shape[1]
    offs = np.cumsum([0, dc, dc, dc, hq, hk, hk, hi_w, LANES, d, d])

    def proj(s):
        return _dot(hb, w_ref[:, int(offs[s]):int(offs[s + 1])])

    x = x_ref[...]
    ms = jnp.mean(x * x, axis=-1, keepdims=True)
    h = x * lax.rsqrt(ms + NORM_EPS) * n1w_ref[...]
    h = h * (1.0 + scale_ref[...]) + shift_ref[...]
    hb = h.astype(BF16)

    u = proj(2) * proj(0)
    u_ref[...] = u

    @pl.when(i == 0)
    def _():
        carry_ref[...] = jnp.zeros_like(carry_ref)

    buf_ref[0:SUBLANES, :] = carry_ref[...]
    buf_ref[SUBLANES:SUBLANES + tm, :] = u
    carry_ref[...] = u[tm - SUBLANES:tm, :]
    pos = (i * tm + lax.broadcasted_iota(I32, (tm, 1), 0)) % seq_rows
    u1 = jnp.where(pos >= 1, buf_ref[SUBLANES - 1:SUBLANES - 1 + tm, :], ext1_ref[...])
    u2 = jnp.where(pos >= 2, buf_ref[SUBLANES - 2:SUBLANES - 2 + tm, :], ext2_ref[...])
    yc = convw_ref[2:3, :] * u + convw_ref[0:1, :] * u2 + convw_ref[1:2, :] * u1
    a_pre = proj(1) * yc
    ya = _dot(a_pre.astype(BF16), wco_ref[...])
    yag_ref[...] = jax.nn.sigmoid(proj(8)) * ya
    sgb_ref[...] = jax.nn.sigmoid(proj(9))

    cos = cos_ref[...]
    sin = sin_ref[...]
    gsum = gsum_ref[...]
    q_ref[...] = _rope_lanes(_head_rms(proj(3), qnw_ref[...], gsum), cos, sin)
    k = _rope_lanes(_head_rms(proj(4), knw_ref[:, 0:hk], gsum[0:hk, 0:hk]), cos, sin)
    k_ref[...] = k
    kt_ref[...] = k.T
    v = proj(5)
    v_ref[...] = v
    vt_ref[...] = v.T
    qi_ref[...] = _rope_lanes(proj(6), cos, sin)
    slab = proj(7)
    lane = lax.broadcasted_iota(I32, (1, LANES), 1)
    kiw = jnp.where(lane < IDX_DIM, _rope_lanes(slab, cos, sin), slab * IDX_SCALE)
    kiw_ref[...] = kiw
    wit_ref[...] = kiw.T[IDX_DIM:IDX_DIM + IDX_HEADS, :]


def _inproj(x2, shift, scale, n1w, w_pad, cos, sin, qnw, knw, gsum, convw, wco, ext1, ext2, *, tm, seq_rows,
            per_row_mod):
    rows, d = x2.shape
    dc = convw.shape[1]
    hq = N_HEADS * HEAD_DIM
    hk = N_KV_HEADS * HEAD_DIM
    hi_w = IDX_HEADS * IDX_DIM
    nt = rows // tm
    pos_tiles = cos.shape[0] // tm
    tiles_per_seq = max(seq_rows // tm, 1)
    if per_row_mod:
        mod_spec = pl.BlockSpec((tm, d), lambda i: (i, 0))
        ext_spec = pl.BlockSpec((tm, dc), lambda i: (i, 0))
    else:
        mod_spec = pl.BlockSpec((None, 1, d), lambda i: (i // tiles_per_seq, 0, 0))
        ext_spec = pl.BlockSpec((1, dc), lambda i: (0, 0))
    const = lambda shape: pl.BlockSpec(shape, lambda i: tuple(0 for _ in shape))
    row = lambda w: pl.BlockSpec((tm, w), lambda i: (i, 0))
    out_widths = [dc, hq, hk, hk, hi_w, LANES, d, d]
    n_seq = max(rows // (tiles_per_seq * tm), 1)
    kvt_spec = pl.BlockSpec((None, hk, tm), lambda i: (i // tiles_per_seq, 0, i % tiles_per_seq))
    kvt_shape = jax.ShapeDtypeStruct((n_seq, hk, tiles_per_seq * tm), F32)
    return pl.pallas_call(
        functools.partial(_inproj_body, tm, seq_rows, dc),
        grid=(nt,),
        in_specs=[row(d), mod_spec, mod_spec, const((1, d)), const(w_pad.shape),
                  pl.BlockSpec((tm, LANES), lambda i: (i % pos_tiles, 0)),
                  pl.BlockSpec((tm, LANES), lambda i: (i % pos_tiles, 0)),
                  const((1, hq)), const((1, hq)), const((hq, hq)), const(convw.shape), const(wco.shape),
                  ext_spec, ext_spec],
        out_specs=[row(w) for w in out_widths] + [kvt_spec, kvt_spec, pl.BlockSpec((IDX_HEADS, tm), lambda i: (0, i))],
        out_shape=[jax.ShapeDtypeStruct((rows, w), F32) for w in out_widths]
                  + [kvt_shape, kvt_shape, jax.ShapeDtypeStruct((IDX_HEADS, rows), F32)],
        scratch_shapes=[pltpu.VMEM((SUBLANES, dc), F32), pltpu.VMEM((tm + SUBLANES, dc), F32)],
        compiler_params=_cparams(("arbitrary",)),
    )(x2, shift, scale, n1w, w_pad, cos, sin, qnw, knw, gsum, convw, wco, ext1, ext2)


def _sortable_key(score):
    bits = lax.bitcast_convert_type(score, I32)
    key = jnp.where(bits < 0, bits ^ jnp.int32(0x7FFFFFFF), bits)
    return jnp.where(score == 0.0, jnp.int32(0), key)


def _kth_largest_key(count_ge, shape, k):
    ans = jnp.where(count_ge(jnp.zeros(shape, I32)) >= k, jnp.int32(0), jnp.int32(INT_MIN))

    def step(it, ans):
        cand = ans + jnp.left_shift(jnp.int32(1), jnp.int32(30) - it)
        return jnp.where(count_ge(cand) >= k, cand, ans)

    return lax.fori_loop(0, 31, step, ans)


def _tie_cut(count_eq_below, shape, need, nbits):
    def step(it, c):
        cand = c + jnp.left_shift(jnp.int32(1), jnp.int32(nbits - 1) - it)
        return jnp.where(count_eq_below(cand) < need, cand, c)

    return lax.fori_loop(0, nbits, step, jnp.zeros(shape, I32))


def _col_reduce(x, op, rows=64):
    n = x.shape[0]
    if n > rows and n % rows == 0:
        x = op(x.reshape(n // rows, rows, x.shape[1]), axis=0)
    return op(x, axis=0, keepdims=True)


def _pattn_class(tq, klen, topk, j, q_ref, qi_ref, wit_ref, k_ref, vt_ref, kiw_ref, o_ref, key_ref, cut_ref):
    qi = qi_ref[...]
    wit = wit_ref[...]
    ki = kiw_ref[0:klen, 0:IDX_DIM].astype(BF16)
    score = jnp.zeros((klen, tq), F32)
    hpm = 4
    for h0 in range(0, IDX_HEADS, hpm):
        qs = jnp.concatenate([qi[:, h * IDX_DIM:(h + 1) * IDX_DIM] for h in range(h0, h0 + hpm)], axis=0)
        s = _dot_nt(ki, qs.astype(BF16))
        for g in range(hpm):
            score = score + wit[h0 + g:h0 + g + 1, :] * jnp.maximum(s[:, g * tq:(g + 1) * tq], 0.0)
    kpos = lax.broadcasted_iota(I32, (klen, 1), 0)
    qpos = j * tq + lax.broadcasted_iota(I32, (1, tq), 1)
    causal = kpos <= qpos
    score = jnp.where(causal, score, NEG_INF)
    key_ref[0:klen, :] = _sortable_key(score)

    def count_ge(c):
        return _col_reduce((key_ref[0:klen, :] >= c).astype(F32), jnp.sum)

    thr = _kth_largest_key(count_ge, (1, tq), topk)
    key = key_ref[0:klen, :]
    gt = key > thr
    eq = key == thr
    need = topk - _col_reduce(gt.astype(F32), jnp.sum)
    n_eq = _col_reduce(eq.astype(F32), jnp.sum)
    cut_ref[...] = jnp.full(cut_ref.shape, klen, I32)

    @pl.when(jnp.max(n_eq - need) > 0)
    def _():
        def count_eq_below(c):
            hit = (key_ref[0:klen, :] == thr) & (kpos < c)
            return _col_reduce(hit.astype(F32), jnp.sum)

        cut = _tie_cut(count_eq_below, (1, tq), need, int(klen - 1).bit_length())
        cut_ref[...] = jnp.broadcast_to(cut, cut_ref.shape)

    cut = cut_ref[0:1, :]
    bias = jnp.where((gt | (eq & (kpos <= cut))) & causal, 0.0, NEG_INF)

    q = q_ref[...] * (HEAD_DIM ** -0.5)
    k = k_ref[0:klen, :].astype(BF16)
    group = N_HEADS // N_KV_HEADS
    outs = []
    for h in range(N_HEADS):
        n = h // group
        if h % group == 0:
            qs = jnp.concatenate([q[:, g * HEAD_DIM:(g + 1) * HEAD_DIM] for g in range(h, h + group)], axis=0)
            logits_n = _dot_nt(k[:, n * HEAD_DIM:(n + 1) * HEAD_DIM], qs.astype(BF16))
        logits = logits_n[:, (h % group) * tq:(h % group + 1) * tq] + bias
        m = _col_reduce(logits, jnp.max)
        p = jnp.exp(logits - m)
        l = _col_reduce(p, jnp.sum)
        vt = vt_ref[n * HEAD_DIM:(n + 1) * HEAD_DIM, 0:klen].astype(BF16)
        outs.append(_dot(vt, p.astype(BF16)) / l)
    o_ref[...] = jnp.concatenate(outs, axis=0).T


def _pattn_body(tq, t, topk, n_cls, q_ref, qi_ref, wit_ref, k_ref, vt_ref, kiw_ref, o_ref, key_ref, cut_ref):
    j = pl.program_id(1)
    blocks_per_cls = (t // tq) // n_cls
    for c in range(n_cls):
        @pl.when((j >= c * blocks_per_cls) & (j < (c + 1) * blocks_per_cls))
        def _(c=c):
            _pattn_class(tq, (c + 1) * blocks_per_cls * tq, topk, j, q_ref, qi_ref, wit_ref, k_ref, vt_ref, kiw_ref,
                         o_ref, key_ref, cut_ref)


def _pattn(q, qi, wit, k, vt, kiw, *, bsz, t):
    hq = q.shape[1]
    tq = Q_BLOCK
    nqb = t // tq
    n_cls = max(c for c in (1, 2, 4, 8) if nqb % c == 0)
    topk = min(TOPK_MAX, t // 4)
    qspec = lambda w: pl.BlockSpec((tq, w), lambda bi, j: (bi * nqb + j, 0))
    kspec = lambda w: pl.BlockSpec((t, w), lambda bi, j: (bi, 0))
    return pl.pallas_call(
        functools.partial(_pattn_body, tq, t, topk, n_cls),
        grid=(bsz, nqb),
        in_specs=[qspec(hq), qspec(qi.shape[1]),
                  pl.BlockSpec((wit.shape[0], tq), lambda bi, j: (0, bi * nqb + j)),
                  kspec(k.shape[1]),
                  pl.BlockSpec((None, vt.shape[1], t), lambda bi, j: (bi, 0, 0)),
                  kspec(kiw.shape[1])],
        out_specs=qspec(hq),
        out_shape=jax.ShapeDtypeStruct((bsz * t, hq), F32),
        scratch_shapes=[pltpu.VMEM((t, tq), I32), pltpu.VMEM((SUBLANES, tq), I32)],
        compiler_params=_cparams(("arbitrary", "arbitrary")),
    )(q, qi, wit, k, vt, kiw)


def _sattn_body(n_pages, g, ds, topk, pt_ref, qi_ref, w_ref, q_ref, kin_ref, kn_ref, vn_ref, *rest):
    kic, kc, vc = rest[0:g], rest[g:2 * g], rest[2 * g:3 * g]
    o_ref, score_ref, mask_ref, m_ref, l_ref, acc_ref, kicat_ref, kcat_ref, vcat_ref = rest[3 * g:]
    ph = pl.program_id(1)
    p = pl.program_id(2)
    n_steps = n_pages // g
    group = N_HEADS // N_KV_HEADS
    sub = lax.broadcasted_iota(I32, (SUBLANES, LANES), 0)
    lane = lax.broadcasted_iota(I32, (SUBLANES, LANES), 1)
    new_valid = (lane <= sub) & (lane < ds)

    def scores(ki_bf, npg):
        s = _dot(qi_ref[...].astype(BF16), ki_bf)
        w = w_ref[...]
        w = jnp.concatenate([w] * npg, axis=1) if npg > 1 else w
        acc = jnp.zeros((SUBLANES, npg * LANES), F32)
        for h in range(IDX_HEADS):
            r = slice(h * SUBLANES, (h + 1) * SUBLANES)
            acc = acc + w[r, :] * jnp.maximum(s[r, :], 0.0)
        return acc

    merged = n_steps == 1

    def in_phase(k):
        return (lambda f: f()) if merged else pl.when(ph == k)

    @in_phase(0)
    def _():
        for gi in range(g):
            kicat_ref[:, gi * LANES:(gi + 1) * LANES] = kic[gi][...].astype(BF16)
        sc = scores(kicat_ref[...], g)
        for gi in range(g):
            score_ref[p * g + gi] = sc[:, gi * LANES:(gi + 1) * LANES]

        @pl.when(p == 0)
        def _():
            score_ref[n_pages] = jnp.where(new_valid, scores(kin_ref[...].astype(BF16), 1), NEG_INF)

        @pl.when(p == n_steps - 1)
        def _():
            key = _sortable_key(score_ref[...])
            slab = lax.broadcasted_iota(I32, key.shape, 0)
            idx = slab * LANES + lax.broadcasted_iota(I32, key.shape, 2)

            def count(hit):
                x = hit.astype(F32)
                parts = [jnp.sum(x[s:s + 16], axis=0) for s in range(0, x.shape[0], 16)]
                c = functools.reduce(lambda u, v: u + v, parts)
                return jnp.sum(c, axis=1, keepdims=True)[None]

            thr = _kth_largest_key(lambda c: count(key >= c), (1, SUBLANES, 1), topk)
            gt = key > thr
            eq = key == thr
            need = topk - count(gt)
            valid = (slab < n_pages) | new_valid[None]
            mask_ref[...] = ((gt | eq) & valid).astype(F32)

            real_row = lax.broadcasted_iota(I32, (1, SUBLANES, 1), 1) < ds

            @pl.when(jnp.max(jnp.where(real_row, count(eq) - need, 0.0)) > 0)
            def _():
                nbits = int((n_pages + 1) * LANES - 1).bit_length()
                cut = _tie_cut(lambda c: count(eq & (idx < c)), (1, SUBLANES, 1), need, nbits)
                mask_ref[...] = ((gt | (eq & (idx <= cut))) & valid).astype(F32)

    def attend(mask8, kb, vb):
        maskg = jnp.concatenate([mask8] * group, axis=0) > 0.5
        for n in range(N_KV_HEADS):
            qn = (q_ref[n] * (HEAD_DIM ** -0.5)).astype(BF16)
            logits = _dot(qn, kb[n * HEAD_DIM:(n + 1) * HEAD_DIM, :])
            logits = jnp.where(maskg, logits, NEG_INF)
            m_old = m_ref[n][:, 0:1]
            m_new = jnp.maximum(m_old, jnp.max(logits, axis=1, keepdims=True))
            pr = jnp.where(maskg, jnp.exp(logits - m_new), 0.0)
            alpha = jnp.exp(m_old - m_new)
            l_new = alpha * l_ref[n][:, 0:1] + jnp.sum(pr, axis=1, keepdims=True)
            acc_ref[n] = alpha * acc_ref[n] + _dot_nt(pr.astype(BF16), vb[n * HEAD_DIM:(n + 1) * HEAD_DIM, :])
            m_ref[n] = jnp.broadcast_to(m_new, m_ref.shape[1:])
            l_ref[n] = jnp.broadcast_to(l_new, l_ref.shape[1:])

    @in_phase(1)
    def _():
        @pl.when(p == 0)
        def _():
            m_ref[...] = jnp.full(m_ref.shape, NEG_INF, F32)
            l_ref[...] = jnp.zeros(l_ref.shape, F32)
            acc_ref[...] = jnp.zeros(acc_ref.shape, F32)
            attend(mask_ref[n_pages], kn_ref[...].astype(BF16), vn_ref[...].astype(BF16))

        for gi in range(g):
            kcat_ref[:, gi * LANES:(gi + 1) * LANES] = kc[gi][...].astype(BF16)
            vcat_ref[:, gi * LANES:(gi + 1) * LANES] = vc[gi][...].astype(BF16)
        mask = jnp.concatenate([mask_ref[p * g + gi] for gi in range(g)], axis=1)
        attend(mask, kcat_ref[...], vcat_ref[...])

        @pl.when(p == n_steps - 1)
        def _():
            o_ref[...] = acc_ref[...] / l_ref[...][:, :, 0:HEAD_DIM]


def _sattn(page_table, qi_arr, w_arr, q_arr, kin, kn, vn, cache_kidx, cache_k, cache_v, *, ds):
    db, n_pages = page_table.shape
    page = cache_k.shape[2]
    assert page == LANES
    g = max(c for c in (1, 2, 4, 8, 16, 32, 64) if n_pages % c == 0)
    n_steps = n_pages // g
    topk = min(TOPK_MAX, (n_pages * page + ds) // 4)
    group = N_HEADS // N_KV_HEADS
    hk = N_KV_HEADS * HEAD_DIM
    rows_i = IDX_HEADS * SUBLANES
    rows_q = group * SUBLANES
    pt = page_table.reshape(-1)
    per_b = lambda *shape: pl.BlockSpec((None,) + shape, lambda b, ph, p, pt: (b,) + tuple(0 for _ in shape))

    def kic_map(gi, b, ph, p, pt):
        return (pt[b * n_pages + (p * (1 - ph) + (n_steps - 1) * ph) * g + gi], 0, 0)

    def kvc_map(gi, b, ph, p, pt):
        return (pt[b * n_pages + p * ph * g + gi], 0, 0)

    kic_specs = [pl.BlockSpec((None, IDX_DIM, page), functools.partial(kic_map, gi)) for gi in range(g)]
    kvc_specs = [pl.BlockSpec((None, hk, page), functools.partial(kvc_map, gi)) for gi in range(g)]
    grid_spec = pltpu.PrefetchScalarGridSpec(
        num_scalar_prefetch=1,
        grid=(db, 1 if n_steps == 1 else 2, n_steps),
        in_specs=[per_b(rows_i, IDX_DIM), per_b(rows_i, LANES), per_b(N_KV_HEADS, rows_q, HEAD_DIM),
                  per_b(IDX_DIM, page), per_b(hk, page), per_b(hk, page)] + kic_specs + kvc_specs + kvc_specs,
        out_specs=per_b(N_KV_HEADS, rows_q, HEAD_DIM),
        scratch_shapes=[pltpu.VMEM((n_pages + 1, SUBLANES, LANES), F32),
                        pltpu.VMEM((n_pages + 1, SUBLANES, LANES), F32),
                        pltpu.VMEM((N_KV_HEADS, rows_q, LANES), F32),
                        pltpu.VMEM((N_KV_HEADS, rows_q, LANES), F32),
                        pltpu.VMEM((N_KV_HEADS, rows_q, HEAD_DIM), F32),
                        pltpu.VMEM((IDX_DIM, g * page), BF16),
                        pltpu.VMEM((hk, g * page), BF16),
                        pltpu.VMEM((hk, g * page), BF16)],
    )
    return pl.pallas_call(
        functools.partial(_sattn_body, n_pages, g, ds, topk),
        grid_spec=grid_spec,
        out_shape=jax.ShapeDtypeStruct((db, N_KV_HEADS, rows_q, HEAD_DIM), F32),
        compiler_params=_cparams(("arbitrary", "arbitrary", "arbitrary")),
    )(pt, qi_arr, w_arr, q_arr, kin, kn, vn, *([cache_kidx] * g), *([cache_k] * g), *([cache_v] * g))


def _finish_body(x_ref, yag_ref, sgb_ref, attn_ref, gate_ref, shift_ref, scale_ref, n2w_ref, wao_ref, wout_ref, wq_ref,
                 x1_ref, h2t_ref, pq_ref):
    yb = _dot(attn_ref[...].astype(BF16), wao_ref[...])
    mix = yag_ref[...] + sgb_ref[...] * yb
    x1 = x_ref[...] + gate_ref[...] * _dot(mix.astype(BF16), wout_ref[...])
    x1_ref[...] = x1
    ms = jnp.mean(x1 * x1, axis=-1, keepdims=True)
    h2 = x1 * lax.rsqrt(ms + NORM_EPS) * n2w_ref[...]
    h2 = h2 * (1.0 + scale_ref[...]) + shift_ref[...]
    h2t_ref[...] = h2.T.astype(BF16)
    pq_ref[...] = _dot(h2.astype(BF16), wq_ref[...]).astype(BF16)


def _finish(x2, yag, sgb, attn, gate, shift, scale, n2w, wao, wout, wq, *, tm, seq_rows, per_row_mod):
    rows, d = x2.shape
    nt = rows // tm
    tiles_per_seq = max(seq_rows // tm, 1)
    if per_row_mod:
        mod_spec = pl.BlockSpec((tm, d), lambda i: (i, 0))
    else:
        mod_spec = pl.BlockSpec((None, 1, d), lambda i: (i // tiles_per_seq, 0, 0))
    const = lambda shape: pl.BlockSpec(shape, lambda i: tuple(0 for _ in shape))
    row = lambda w: pl.BlockSpec((tm, w), lambda i: (i, 0))
    nq = wq.shape[1]
    return pl.pallas_call(
        _finish_body,
        grid=(nt,),
        in_specs=[row(d), row(d), row(d), row(attn.shape[1]), mod_spec, mod_spec, mod_spec, const((1, d)),
                  const(wao.shape), const(wout.shape), const(wq.shape)],
        out_specs=[row(d), pl.BlockSpec((d, tm), lambda i: (0, i)), row(nq)],
        out_shape=[jax.ShapeDtypeStruct((rows, d), F32), jax.ShapeDtypeStruct((d, rows), BF16),
                   jax.ShapeDtypeStruct((rows, nq), BF16)],
        compiler_params=_cparams(("arbitrary",)),
    )(x2, yag, sgb, attn, gate, shift, scale, n2w, wao, wout, wq)


def _bitonic_pairs(n):
    pairs = []
    k = 2
    while k <= n:
        j = k // 2
        while j >= 1:
            for i in range(n):
                l = i ^ j
                if l > i:
                    pairs.append((i, l) if (i & k) == 0 else (l, i))
            j //= 2
        k *= 2
    return pairs


def _top_sorted(x, n):
    nv = 16
    rows, lanes = x.shape
    assert rows % SUBLANES == 0 and rows <= nv * SUBLANES and n <= nv + 1
    v = [x[q * SUBLANES:(q + 1) * SUBLANES, :] for q in range(rows // SUBLANES)]
    v += [jnp.full((SUBLANES, lanes), -jnp.inf, F32)] * (nv - len(v))
    for hi, lo in _bitonic_pairs(nv):
        a, b = v[hi], v[lo]
        v[hi], v[lo] = jnp.maximum(a, b), jnp.minimum(a, b)
    v.append(jnp.full((SUBLANES, lanes), -jnp.inf, F32))
    vals = []
    for r in range(n):
        m = jnp.max(v[0], axis=0, keepdims=True)
        vals.append(m)
        if r + 1 < n:
            won = v[0] >= m
            depth = min(n - 1 - r, nv)
            for q in range(depth):
                v[q] = jnp.where(won, v[q + 1], v[q])
    return jnp.concatenate(vals, axis=0)


def _gelu_tanh(x):
    c = np.float32(np.sqrt(2.0 / np.pi))
    return x * (0.5 * (1.0 + jnp.tanh(c * (x + 0.044715 * (x * x * x)))))


def _peer_select(lg, pq_ref, sk_ref, ct_ref, ea_ref, eb_ref):
    half = PEER_QDIM // 2
    rows = pl.ds(pl.multiple_of(lg * LANES, LANES), LANES)
    for h in range(PEER_HEADS):
        sa = _dot_nt(sk_ref[2 * h], pq_ref[rows, (2 * h) * half:(2 * h + 1) * half])
        sb = _dot_nt(sk_ref[2 * h + 1], pq_ref[rows, (2 * h + 1) * half:(2 * h + 2) * half])
        ta = _top_sorted(sa, PEER_TOPK + 1)
        tb = _top_sorted(sb, PEER_TOPK + 1)
        tail = jnp.concatenate([ta[0:1] + tb[16:17], ta[16:17] + tb[0:1],
                                jnp.full((SUBLANES - 2, LANES), -jnp.inf, F32)], axis=0)
        cand = jnp.concatenate(
            [ta[0:1] + tb[0:16]] + [ta[r:r + 1] + tb[0:8] for r in range(1, 8)] + [ta[8:16] + tb[0:1], tail], axis=0)
        top = _top_sorted(cand, PEER_TOPK + 1)
        thr = 0.5 * (top[PEER_TOPK - 1:PEER_TOPK] + top[PEER_TOPK:PEER_TOPK + 1])
        z = jnp.sum(jnp.exp(top[0:PEER_TOPK] - top[0:1]), axis=0, keepdims=True)
        ct_ref[lg, h] = jnp.exp((thr - tb[0:1]) - sa) / z
        ea_ref[lg, h] = jnp.exp(sa - ta[0:1])
        eb_ref[lg, h] = (jnp.exp(sb - tb[0:1]) / z).reshape(eb_ref.shape[2:])


def _peer_body(ce, tm, nc, h2t_ref, pq_ref, x1_ref, gate_ref, sk_ref, u_ref, vt_ref, out_ref,
               ct_ref, ea_ref, eb_ref, yt_ref, p_ref, act_ref):
    c = pl.program_id(1)
    kk = PEER_KEYS
    n_lg = tm // LANES
    n_sub = ce // kk
    pair = 2 if n_sub % 2 == 0 else 1
    jrows = kk // 2
    n_half = 2 if n_lg % 2 == 0 else 1
    hw = tm // n_half

    @pl.when(c == 0)
    def _():
        yt_ref[...] = jnp.zeros_like(yt_ref)

        def body(lg, carry):
            _peer_select(lg, pq_ref, sk_ref, ct_ref, ea_ref, eb_ref)
            return carry

        lax.fori_loop(0, n_lg, body, 0)

    for n in range(n_half):
        act_ref[:, n * hw:(n + 1) * hw] = _dot(u_ref[...], h2t_ref[:, n * hw:(n + 1) * hw])
    for n in range(n_half):
        for lg in range(n * n_lg // n_half, (n + 1) * n_lg // n_half):
            cols = slice(lg * LANES, (lg + 1) * LANES)
            for ip in range(n_sub // pair):
                subs = [pair * ip + t for t in range(pair)]
                for jh in range(2):
                    vrows = slice(jh * jrows // SUBLANES, (jh + 1) * jrows // SUBLANES)
                    w = [None] * pair
                    for h in range(PEER_HEADS):
                        ebh = eb_ref[lg, h, vrows]
                        for t, s in enumerate(subs):
                            i = c * n_sub + s
                            ctv = jnp.broadcast_to(ct_ref[lg, h, pl.ds(i, 1), :], (SUBLANES, LANES))[None]
                            eav = jnp.broadcast_to(ea_ref[lg, h, pl.ds(i, 1), :], (SUBLANES, LANES))[None]
                            term = jnp.where(ebh >= ctv, ebh, 0.0) * eav
                            w[t] = term if w[t] is None else w[t] + term
                    for t, s in enumerate(subs):
                        rows = slice(s * kk + jh * jrows, s * kk + (jh + 1) * jrows)
                        prod = w[t].reshape(jrows, LANES) * _gelu_tanh(act_ref[rows, cols])
                        p_ref[rows, cols] = prod.astype(BF16)
        yt_ref[:, n * hw:(n + 1) * hw] += _dot(vt_ref[...], p_ref[:, n * hw:(n + 1) * hw])

    @pl.when(c == nc - 1)
    def _():
        out_ref[...] = x1_ref[...] + gate_ref[...] * yt_ref[...].T


def _peer(h2t, pq, x1, gate, sk, u_bf, vt_blk, *, tm, seq_rows, per_row_mod):
    rows, d = x1.shape
    ne = u_bf.shape[0]
    ce = vt_blk.shape[2]
    nt = rows // tm
    nc = ne // ce
    assert tm % LANES == 0 and rows % tm == 0 and ne % ce == 0 and ce % PEER_KEYS == 0
    tiles_per_seq = max(seq_rows // tm, 1)
    if per_row_mod:
        mod_spec = pl.BlockSpec((tm, d), lambda i, c: (i, 0))
    else:
        mod_spec = pl.BlockSpec((None, 1, d), lambda i, c: (i // tiles_per_seq, 0, 0))
    row = lambda w: pl.BlockSpec((tm, w), lambda i, c: (i, 0))
    sel_shape = (tm // LANES, PEER_HEADS, PEER_KEYS, LANES)
    vreg_shape = (tm // LANES, PEER_HEADS, PEER_KEYS // SUBLANES, SUBLANES, LANES)
    return pl.pallas_call(
        functools.partial(_peer_body, ce, tm, nc),
        grid=(nt, nc),
        in_specs=[pl.BlockSpec((d, tm), lambda i, c: (0, i)), row(pq.shape[1]), row(d), mod_spec,
                  pl.BlockSpec(sk.shape, lambda i, c: (0, 0, 0)),
                  pl.BlockSpec((ce, d), lambda i, c: (c, 0)),
                  pl.BlockSpec((None, d, ce), lambda i, c: (c, 0, 0))],
        out_specs=row(d),
        out_shape=jax.ShapeDtypeStruct((rows, d), F32),
        scratch_shapes=[pltpu.VMEM(sel_shape, F32), pltpu.VMEM(sel_shape, F32), pltpu.VMEM(vreg_shape, F32),
                        pltpu.VMEM((d, tm), F32), pltpu.VMEM((ce, tm), BF16), pltpu.VMEM((ce, tm), F32)],
        compiler_params=_cparams(("arbitrary", "arbitrary")),
    )(h2t, pq, x1, gate, sk, u_bf, vt_blk)


def _rope_tables(pos):
    half = HEAD_DIM // 2
    freqs = ROPE_THETA ** (-jnp.arange(half, dtype=F32) / half)
    ang = pos.astype(F32)[:, None] * freqs[None, :]
    cos, sin = jnp.cos(ang), jnp.sin(ang)
    cos2 = jnp.concatenate([cos, cos, cos, cos], axis=1)
    sin2 = jnp.concatenate([-sin, sin, -sin, sin], axis=1)
    return cos2, sin2


def _pick_tile(rows, pref):
    t = min(pref, rows)
    while rows % t:
        t //= 2
    return t


def kernel(x_prompt, x_sample, cache_k, cache_v, cache_kidx, state_conv, page_table, c_prompt, c_sample, w_ada, b_ada,
           norm1_w, w_in, conv_w, q_norm_w, k_norm_w, w_conv_out, w_attn_out, w_out, norm2_w, peer_w_q, peer_sub_keys,
           peer_u, peer_v):
    bsz, seq, d = x_prompt.shape
    db, ds, _ = x_sample.shape
    depth = w_ada.shape[0]
    dc = conv_w.shape[2]
    hq = N_HEADS * HEAD_DIM
    hk = N_KV_HEADS * HEAD_DIM
    hi_w = IDX_HEADS * IDX_DIM
    n_pages = page_table.shape[1]
    page = cache_k.shape[2]
    past = n_pages * page
    group = N_HEADS // N_KV_HEADS

    pos_p = jnp.arange(seq, dtype=I32)
    pos_s = past + jnp.arange(ds, dtype=I32)
    cos_p, sin_p = _rope_tables(pos_p)
    cos_s, sin_s = _rope_tables(jnp.tile(pos_s, db))
    gsum = jnp.kron(jnp.eye(N_HEADS, dtype=F32), jnp.ones((HEAD_DIM, HEAD_DIM), F32)).astype(BF16)

    xp = x_prompt.reshape(bsz * seq, d)
    xs = x_sample.reshape(db * ds, d)
    outs = {n: [] for n in ("kp", "vp", "kip", "cp", "ks", "vs", "kis", "cs")}

    for layer in range(depth):
        mod = _adaln(jnp.concatenate([c_prompt, c_sample], axis=0), w_ada[layer], b_ada[layer])
        mod_p = [m.reshape(bsz, 1, d) for m in jnp.split(mod[:bsz], N_MOD, axis=-1)]
        mod_s = [jnp.repeat(m, ds, axis=0) for m in jnp.split(mod[bsz:], N_MOD, axis=-1)]

        w = w_in[layer]
        o_ki = 3 * dc + hq + 2 * hk + hi_w
        o_g = o_ki + IDX_DIM + IDX_HEADS
        slab = jnp.concatenate([w[:, o_ki:o_g], jnp.zeros((d, LANES - IDX_DIM - IDX_HEADS), F32)], axis=1)
        w_pad = jnp.concatenate([w[:, :o_ki], slab, w[:, o_g:]], axis=1).astype(BF16)
        n1w = norm1_w[layer].reshape(1, d)
        qnw = jnp.tile(q_norm_w[layer], N_HEADS).reshape(1, hq)
        knw = jnp.tile(k_norm_w[layer], N_HEADS).reshape(1, hq)
        wco = w_conv_out[layer].astype(BF16)
        wao = w_attn_out[layer].astype(BF16)
        wout = w_out[layer].astype(BF16)
        wq = peer_w_q[layer].astype(BF16)
        n2w = norm2_w[layer].reshape(1, d)
        sk = jnp.transpose(peer_sub_keys[layer], (1, 0, 2, 3)).reshape(2 * PEER_HEADS, PEER_KEYS, PEER_QDIM // 2)
        sk = sk.astype(BF16)
        u_bf = peer_u[layer].astype(BF16)
        ce = min(2048, u_bf.shape[0])
        vt_bf = peer_v[layer].astype(BF16).reshape(-1, ce, d).transpose(0, 2, 1)

        tm_p = _pick_tile(seq, 256)
        zero_ext = jnp.zeros((1, dc), F32)
        u, q, k, v, qi, kiw, yag, sgb, kt, vt, wit = _inproj(
            xp, mod_p[0], mod_p[1], n1w, w_pad, cos_p, sin_p, qnw, knw, gsum, conv_w[layer], wco, zero_ext, zero_ext,
            tm=tm_p, seq_rows=seq, per_row_mod=False)
        ki = kiw[:, :IDX_DIM]
        attn = _pattn(q, qi, wit, k, vt, kiw, bsz=bsz, t=seq)
        tm_f = _pick_tile(seq, 512)
        x1, h2t, pq = _finish(xp, yag, sgb, attn, mod_p[2], mod_p[3], mod_p[4], n2w, wao, wout,
                              wq, tm=tm_f, seq_rows=seq, per_row_mod=False)
        xp = _peer(h2t, pq, x1, mod_p[5], sk, u_bf, vt_bf, tm=tm_f, seq_rows=seq, per_row_mod=False)
        outs["kp"].append(kt.reshape(bsz, N_KV_HEADS, HEAD_DIM, seq).transpose(0, 3, 1, 2))
        outs["vp"].append(vt.reshape(bsz, N_KV_HEADS, HEAD_DIM, seq).transpose(0, 3, 1, 2))
        outs["kip"].append(ki.reshape(bsz, seq, IDX_DIM))
        outs["cp"].append(u.reshape(bsz, seq, dc)[:, seq - 2:, :])

        rows_s = db * ds
        st = state_conv[layer]
        posr = jnp.tile(jnp.arange(ds), db)
        ext1 = jnp.repeat(st[:, 1, :], ds, axis=0)
        ext2 = jnp.where((posr == 0)[:, None], jnp.repeat(st[:, 0, :], ds, axis=0), ext1)
        u, q, k, v, qi, kiw, yag, sgb, _, _, _ = _inproj(
            xs, mod_s[0], mod_s[1], n1w, w_pad, cos_s, sin_s, qnw, knw, gsum, conv_w[layer], wco, ext1, ext2,
            tm=rows_s, seq_rows=ds, per_row_mod=True)
        ki = kiw[:, :IDX_DIM]
        wi = kiw[:, IDX_DIM:IDX_DIM + IDX_HEADS]
        pad_q = lambda a: jnp.pad(a, ((0, 0), (0, 0), (0, SUBLANES - ds), (0, 0)))
        qi_arr = pad_q(qi.reshape(db, ds, IDX_HEADS, IDX_DIM).transpose(0, 2, 1, 3)).reshape(db, IDX_HEADS * SUBLANES, IDX_DIM)
        w_arr = pad_q(wi.reshape(db, ds, IDX_HEADS, 1).transpose(0, 2, 1, 3)).reshape(db, IDX_HEADS * SUBLANES, 1)
        w_arr = jnp.broadcast_to(w_arr, (db, IDX_HEADS * SUBLANES, LANES))
        q_arr = pad_q(q.reshape(db, ds, N_HEADS, HEAD_DIM).transpose(0, 2, 1, 3)).reshape(db, N_KV_HEADS, group * SUBLANES, HEAD_DIM)
        pad_page = lambda a: jnp.pad(a.reshape(db, ds, -1), ((0, 0), (0, page - ds), (0, 0))).transpose(0, 2, 1)
        cki = jnp.transpose(cache_kidx[layer], (0, 2, 1))
        ck = jnp.transpose(cache_k[layer], (0, 2, 3, 1)).reshape(-1, hk, page)
        cv = jnp.transpose(cache_v[layer], (0, 2, 3, 1)).reshape(-1, hk, page)
        o = _sattn(page_table, qi_arr, w_arr, q_arr, pad_page(ki), pad_page(k), pad_page(v), cki, ck, cv, ds=ds)
        attn = o.reshape(db, N_HEADS, SUBLANES, HEAD_DIM)[:, :, :ds, :].transpose(0, 2, 1, 3).reshape(rows_s, hq)
        x1, h2t, pq = _finish(xs, yag, sgb, attn, mod_s[2], mod_s[3], mod_s[4], n2w, wao, wout, wq,
                              tm=rows_s, seq_rows=ds, per_row_mod=True)
        xs = _peer(h2t, pq, x1, mod_s[5], sk, u_bf, vt_bf, tm=rows_s, seq_rows=ds, per_row_mod=True)
        outs["ks"].append(k.reshape(db, ds, N_KV_HEADS, HEAD_DIM))
        outs["vs"].append(v.reshape(db, ds, N_KV_HEADS, HEAD_DIM))
        outs["kis"].append(ki.reshape(db, ds, IDX_DIM))
        full = jnp.concatenate([st, u.reshape(db, ds, dc)], axis=1)
        outs["cs"].append(full[:, full.shape[1] - 2:, :])

    stack = lambda n: jnp.stack(outs[n])
    return (xp.reshape(bsz, seq, d), xs.reshape(db, ds, d), stack("kp"), stack("vp"), stack("kip"), stack("cp"),
            stack("ks"), stack("vs"), stack("kis"), stack("cs"))
```
